```python
import jax
import jax.numpy as jnp
from jax import lax
import numpy as np

D_MODEL = 1024
BATCH = 8
SEQ = 4096
DEPTH = 2

HEAD_DIM = 64
ROPE_DIM = HEAD_DIM // 4
ROPE_THETA = 500000.0
NORM_EPS = 1e-6
NEG_INF = -1e30
FORCE_SCORE = 1e6

NSA_HEADS = 8
NSA_KV_GROUPS = 2
NSA_HPG = NSA_HEADS // NSA_KV_GROUPS
CMP_LEN = 32
CMP_STRIDE = 16
SLC_BLOCK = 64
SLC_TOPK = 16
WINDOW = 512

FOX_HEADS = 8

Q_BLOCK = 128
SLC_SUB = 32

NSA_Q = NSA_HEADS * HEAD_DIM
NSA_KV = NSA_KV_GROUPS * HEAD_DIM
NSA_GATES = 3 * NSA_HEADS
FOX_W = FOX_HEADS * HEAD_DIM
A_SPLITS = (NSA_Q, NSA_KV, NSA_KV, NSA_KV, NSA_KV, NSA_KV, NSA_KV, NSA_GATES, FOX_W, FOX_W, FOX_W, FOX_HEADS)
IN_A_WIDTH = sum(A_SPLITS)
MIX_A_WIDTH = NSA_Q + FOX_W

GLA_HEADS = 4
GLA_DK = D_MODEL // 2
GLA_DV = D_MODEL
GLA_DK_HEAD = GLA_DK // GLA_HEADS
GLA_DV_HEAD = GLA_DV // GLA_HEADS
GLA_RANK = 16
GLA_TAU = 16.0
GLA_CHUNK = 64
C_SPLITS = (GLA_DK, GLA_DK, GLA_DV, GLA_DV, GLA_RANK)
IN_C_WIDTH = sum(C_SPLITS)

D_FF = ((8 * D_MODEL // 3 + 127) // 128) * 128
N_EXPERTS = 8
TOP_K = 2

N_EVEN = (DEPTH + 1) // 2
N_ODD = DEPTH // 2

kernel_name = 'hybrid_nsa_fox_gla_moe_block'


def _split(t, sizes):
    return jnp.split(t, np.cumsum(sizes)[:-1].tolist(), axis=-1)


def rmsnorm(x, g):
    xf = x.astype(jnp.float32)
    y = xf * lax.rsqrt(jnp.mean(xf * xf, axis=-1, keepdims=True) + NORM_EPS)
    return (y * g.astype(jnp.float32)).astype(x.dtype)


def rope_tables(pos):
    inv = ROPE_THETA ** (-jnp.arange(0, ROPE_DIM, 2, dtype=jnp.float32) / ROPE_DIM)
    ang = pos.astype(jnp.float32)[:, None] * inv[None, :]
    return jnp.cos(ang), jnp.sin(ang)


def partial_rope(x, cos, sin):
    r = x[..., :ROPE_DIM].astype(jnp.float32)
    half = ROPE_DIM // 2
    x1, x2 = r[..., :half], r[..., half:]
    c, s = cos[None, :, None, :], sin[None, :, None, :]
    rot = jnp.concatenate([x1 * c - x2 * s, x1 * s + x2 * c], axis=-1).astype(x.dtype)
    return jnp.concatenate([rot, x[..., ROPE_DIM:]], axis=-1)


def masked_softmax(s, mask):
    s = jnp.where(mask, s.astype(jnp.float32), NEG_INF)
    p = jax.nn.softmax(s, axis=-1)
    return jnp.where(mask, p, 0.0)


def swiglu(x, w_gate, w_up, w_down):
    return (jax.nn.silu(x @ w_gate) * (x @ w_up)) @ w_down


def nsa_fox_mixer(h, w_in, nsa_q_norm, nsa_k_norm, pe_k, pe_v, ck1, ck2, cv1, cv2,
                  gate_bias, fox_q_norm, fox_k_norm, fox_f_bias, w_out):
    B, S, _ = h.shape
    G, R, dh = NSA_KV_GROUPS, NSA_HPG, HEAD_DIM
    n_blk = S // Q_BLOCK
    scale = dh ** -0.5
    proj = h @ w_in
    qa, kc, vc, ksl, vsl, kwn, vwn, ga, qb, kb, vb, fb = _split(proj, A_SPLITS)

    def heads(t, n):
        return t.reshape(B, S, n, dh)

    pos = jnp.arange(S)
    cos, sin = rope_tables(pos)

    qa = partial_rope(rmsnorm(heads(qa, NSA_HEADS), nsa_q_norm), cos, sin)
    ksl = partial_rope(rmsnorm(heads(ksl, G), nsa_k_norm), cos, sin)
    kwn = partial_rope(rmsnorm(heads(kwn, G), nsa_k_norm), cos, sin)
    vsl, vwn = heads(vsl, G), heads(vwn, G)

    n_cmp = (S - CMP_LEN) // CMP_STRIDE + 1
    cmp_idx = jnp.arange(n_cmp)[:, None] * CMP_STRIDE + jnp.arange(CMP_LEN)[None, :]

    def compress(tok, pe, w1, w2):
        blk = heads(tok, G)[:, cmp_idx] + pe[None, None, :, None, :]
        blk = blk.transpose(0, 1, 3, 2, 4).reshape(B, n_cmp, G, CMP_LEN * dh)
        return jax.nn.gelu(blk @ w1) @ w2

    cmp_end = jnp.arange(n_cmp) * CMP_STRIDE + CMP_LEN - 1
    ccos, csin = rope_tables(cmp_end)
    k_cmp = partial_rope(rmsnorm(compress(kc, pe_k, ck1, ck2), nsa_k_norm), ccos, csin)
    v_cmp = compress(vc, pe_v, cv1, cv2)

    n_slc = S // SLC_BLOCK
    k_sel = min(SLC_TOPK, n_slc)
    c_start = jnp.arange(n_cmp) * CMP_STRIDE
    s_start = jnp.arange(n_slc) * SLC_BLOCK
    overlap = jnp.maximum(
        jnp.minimum(c_start[:, None] + CMP_LEN, s_start[None, :] + SLC_BLOCK)
        - jnp.maximum(c_start[:, None], s_start[None, :]), 0).astype(jnp.float32) / CMP_LEN

    ks_blocks = ksl.reshape(B, n_slc, SLC_BLOCK, G, dh).transpose(0, 3, 1, 2, 4)
    vs_blocks = vsl.reshape(B, n_slc, SLC_BLOCK, G, dh).transpose(0, 3, 1, 2, 4)
    gather_blocks = jax.vmap(jax.vmap(lambda kbk, ix: kbk[ix]))

    k_pad = jnp.pad(kwn, ((0, 0), (WINDOW, 0), (0, 0), (0, 0)))
    v_pad = jnp.pad(vwn, ((0, 0), (WINDOW, 0), (0, 0), (0, 0)))
    blk_ids = jnp.arange(n_slc)

    def nsa_block(i):
        q0 = i * Q_BLOCK
        qg = lax.dynamic_slice_in_dim(qa, q0, Q_BLOCK, axis=1).reshape(B, Q_BLOCK, G, R, dh)
        tq = q0 + jnp.arange(Q_BLOCK)
        s_c = jnp.einsum('bqgrd,bngd->bgrqn', qg, k_cmp) * scale
        p_c = masked_softmax(s_c, cmp_end[None, :] <= tq[:, None])
        o_c = jnp.einsum('bgrqn,bngd->bqgrd', p_c.astype(v_cmp.dtype), v_cmp)
        imp = jnp.einsum('bgrqn,ns->bgqs', p_c, overlap)
        cur = tq // SLC_BLOCK
        forced = (blk_ids[None, :] == 0) | (blk_ids[None, :] == cur[:, None]) | (blk_ids[None, :] == cur[:, None] - 1)
        future = blk_ids[None, :] > cur[:, None]
        imp = jnp.where(forced, FORCE_SCORE, jnp.where(future, -FORCE_SCORE, imp))
        _, sel = lax.top_k(imp, k_sel)

        def slc_sub(j):
            qs = lax.dynamic_slice_in_dim(qg, j * SLC_SUB, SLC_SUB, axis=1)
            idx = lax.dynamic_slice_in_dim(sel, j * SLC_SUB, SLC_SUB, axis=2)
            ts = lax.dynamic_slice_in_dim(tq, j * SLC_SUB, SLC_SUB)
            kg = gather_blocks(ks_blocks, idx).reshape(B, G, SLC_SUB, k_sel * SLC_BLOCK, dh)
            vg = gather_blocks(vs_blocks, idx).reshape(B, G, SLC_SUB, k_sel * SLC_BLOCK, dh)
            kpos = (idx[..., None] * SLC_BLOCK + jnp.arange(SLC_BLOCK)).reshape(B, G, SLC_SUB, k_sel * SLC_BLOCK)
            m = kpos <= ts[None, None, :, None]
            s_s = jnp.einsum('bqgrd,bgqkd->bgrqk', qs, kg) * scale
            p_s = masked_softmax(s_s, m[:, :, None])
            return jnp.einsum('bgrqk,bgqkd->bqgrd', p_s.astype(vg.dtype), vg)

        o_s = lax.map(slc_sub, jnp.arange(Q_BLOCK // SLC_SUB))
        o_s = o_s.transpose(1, 0, 2, 3, 4, 5).reshape(B, Q_BLOCK, G, R, dh)

        kw = lax.dynamic_slice_in_dim(k_pad, q0, WINDOW + Q_BLOCK, axis=1)
        vw = lax.dynamic_slice_in_dim(v_pad, q0, WINDOW + Q_BLOCK, axis=1)
        kpos_w = q0 - WINDOW + jnp.arange(WINDOW + Q_BLOCK)
        dist = tq[:, None] - kpos_w[None, :]
        m_w = (dist >= 0) & (dist < WINDOW) & (kpos_w[None, :] >= 0)
        s_w = jnp.einsum('bqgrd,bkgd->bgrqk', qg, kw) * scale
        p_w = masked_softmax(s_w, m_w)
        o_w = jnp.einsum('bgrqk,bkgd->bqgrd', p_w.astype(vw.dtype), vw)
        return o_c, o_s, o_w

    o_c, o_s, o_w = lax.map(nsa_block, jnp.arange(n_blk))

    def unblock_a(o):
        return o.transpose(1, 0, 2, 3, 4, 5).reshape(B, S, NSA_HEADS, dh)

    gates = jax.nn.sigmoid((ga + gate_bias).astype(jnp.float32)).reshape(B, S, NSA_HEADS, 3)
    o_a = (gates[..., 0:1] * unblock_a(o_c) + gates[..., 1:2] * unblock_a(o_s)
           + gates[..., 2:3] * unblock_a(o_w)).astype(h.dtype).reshape(B, S, NSA_Q)

    qb = rmsnorm(heads(qb, FOX_HEADS), fox_q_norm)
    kb = rmsnorm(heads(kb, FOX_HEADS), fox_k_norm)
    vb = heads(vb, FOX_HEADS)
    log_f = jax.nn.log_sigmoid(fb.astype(jnp.float32) + fox_f_bias.astype(jnp.float32))
    cum_t = jnp.cumsum(log_f, axis=1).transpose(0, 2, 1)

    def fox_block(i):
        q0 = i * Q_BLOCK
        q = lax.dynamic_slice_in_dim(qb, q0, Q_BLOCK, axis=1)
        cq = lax.dynamic_slice_in_dim(cum_t, q0, Q_BLOCK, axis=2)
        tq = q0 + jnp.arange(Q_BLOCK)
        s_f = (jnp.einsum('bqhd,bkhd->bhqk', q, kb).astype(jnp.float32) * scale
               + cq[..., None] - cum_t[:, :, None, :])
        p_f = masked_softmax(s_f, pos[None, :] <= tq[:, None])
        return jnp.einsum('bhqk,bkhd->bqhd', p_f.astype(vb.dtype), vb)

    o_b = lax.map(fox_block, jnp.arange(n_blk)).transpose(1, 0, 2, 3, 4).reshape(B, S, FOX_W)

    return jnp.concatenate([o_a, o_b], axis=-1) @ w_out


def gla_mixer(h, w_in, w_a2, b_a, o_norm, w_out):
    B, S, _ = h.shape
    H, dk, dv, C = GLA_HEADS, GLA_DK_HEAD, GLA_DV_HEAD, GLA_CHUNK
    n_ch = S // C
    q, k, v, r, a1 = _split(h @ w_in, C_SPLITS)
    q = q.reshape(B, S, H, dk) * (dk ** -0.5)
    k = k.reshape(B, S, H, dk)
    v = v.reshape(B, S, H, dv)
    log_a = (jax.nn.log_sigmoid((a1 @ w_a2 + b_a).astype(jnp.float32)) / GLA_TAU).reshape(B, S, H, dk)

    def chunks(t):
        return t.astype(jnp.float32).reshape(B, n_ch, C, H, t.shape[-1]).transpose(1, 0, 3, 2, 4)

    causal = jnp.tril(jnp.ones((C, C), dtype=bool))[None, None, :, :, None]

    def step(state, inp):
        qc, kc, vc, gc = inp
        b = jnp.cumsum(gc, axis=2)
        o_inter = jnp.einsum('bhcd,bhde->bhce', qc * jnp.exp(b), state)
        dlt = b[:, :, :, None, :] - b[:, :, None, :, :]
        decay = jnp.where(causal, jnp.exp(jnp.where(causal, dlt, 0.0)), 0.0)
        att = jnp.einsum('bhid,bhjd,bhijd->bhij', qc, kc, decay)
        o_intra = jnp.einsum('bhij,bhje->bhie', att, vc)
        b_last = b[:, :, -1:, :]
        state = (jnp.exp(b_last[:, :, 0, :, None]) * state
                 + jnp.einsum('bhcd,bhce->bhde', kc * jnp.exp(b_last - b), vc))
        return state, o_inter + o_intra

    state0 = jnp.zeros((B, H, dk, dv), jnp.float32)
    _, o = lax.scan(step, state0, (chunks(q), chunks(k), chunks(v), chunks(log_a)))
    o = o.transpose(1, 0, 3, 2, 4).reshape(B, S, H, dv)
    o = rmsnorm(o, o_norm).astype(h.dtype).reshape(B, S, GLA_DV) * jax.nn.silu(r)
    return o @ w_out


def moe_ffn(h, router, w_gate, w_up, w_down):
    B, S, D = h.shape
    xt = h.reshape(B * S, D)
    logits = (xt @ router).astype(jnp.float32)
    top_v, top_i = lax.top_k(logits, TOP_K)
    top_w = jax.nn.softmax(top_v, axis=-1)
    gates = jnp.sum(jax.nn.one_hot(top_i, N_EXPERTS, dtype=jnp.float32) * top_w[..., None], axis=1)
    out = jnp.zeros((B * S, D), jnp.float32)
    for e in range(N_EXPERTS):
        out = out + gates[:, e:e + 1] * swiglu(xt, w_gate[e], w_up[e], w_down[e]).astype(jnp.float32)
    return out.astype(h.dtype).reshape(B, S, D)


def setup_inputs(seed: int = 0) -> dict:
    key = jax.random.key(seed)
    keys = iter(jax.random.split(key, 40))

    def w(shape, fan_in):
        return jax.random.normal(next(keys), shape, jnp.float32) * (fan_in ** -0.5)

    def gain(shape):
        return 1.0 + 0.05 * jax.random.normal(next(keys), shape, jnp.float32)

    def small(shape, s=0.02):
        return s * jax.random.normal(next(keys), shape, jnp.float32)

    NE, NO = N_EVEN, N_ODD
    return {
        'x': jax.random.normal(next(keys), (BATCH, SEQ, D_MODEL), jnp.float32),
        'a_norm': gain((NE, D_MODEL)),
        'a_w_in': w((NE, D_MODEL, IN_A_WIDTH), D_MODEL),
        'a_q_norm': gain((NE, HEAD_DIM)),
        'a_k_norm': gain((NE, HEAD_DIM)),
        'a_pe_k': small((NE, CMP_LEN, HEAD_DIM), 0.1),
        'a_pe_v': small((NE, CMP_LEN, HEAD_DIM), 0.1),
        'a_cmp_k_w1': w((NE, CMP_LEN * HEAD_DIM, HEAD_DIM), CMP_LEN * HEAD_DIM),
        'a_cmp_k_w2': w((NE, HEAD_DIM, HEAD_DIM), HEAD_DIM),
        'a_cmp_v_w1': w((NE, CMP_LEN * HEAD_DIM, HEAD_DIM), CMP_LEN * HEAD_DIM),
        'a_cmp_v_w2': w((NE, HEAD_DIM, HEAD_DIM), HEAD_DIM),
        'a_gate_bias': small((NE, NSA_GATES), 0.1),
        'a_fox_q_norm': gain((NE, HEAD_DIM)),
        'a_fox_k_norm': gain((NE, HEAD_DIM)),
        'a_fox_f_bias': 2.0 + small((NE, FOX_HEADS), 0.5),
        'a_w_out': w((NE, MIX_A_WIDTH, D_MODEL), MIX_A_WIDTH),
        'f_norm': gain((NE, D_MODEL)),
        'f_w_gate': w((NE, D_MODEL, D_FF), D_MODEL),
        'f_w_up': w((NE, D_MODEL, D_FF), D_MODEL),
        'f_w_down': w((NE, D_FF, D_MODEL), D_FF),
        'c_norm': gain((NO, D_MODEL)),
        'c_w_in': w((NO, D_MODEL, IN_C_WIDTH), D_MODEL),
        'c_w_a2': w((NO, GLA_RANK, GLA_DK), GLA_RANK),
        'c_b_a': small((NO, GLA_DK), 0.1),
        'c_o_norm': gain((NO, GLA_DV_HEAD)),
        'c_w_out': w((NO, GLA_DV, D_MODEL), GLA_DV),
        'm_norm': gain((NO, D_MODEL)),
        'm_router': w((NO, D_MODEL, N_EXPERTS), D_MODEL),
        'm_w_gate': w((NO, N_EXPERTS, D_MODEL, D_FF), D_MODEL),
        'm_w_up': w((NO, N_EXPERTS, D_MODEL, D_FF), D_MODEL),
        'm_w_down': w((NO, N_EXPERTS, D_FF, D_MODEL), D_FF),
    }


def reference(x, a_norm, a_w_in, a_q_norm, a_k_norm, a_pe_k, a_pe_v, a_cmp_k_w1, a_cmp_k_w2,
              a_cmp_v_w1, a_cmp_v_w2, a_gate_bias, a_fox_q_norm, a_fox_k_norm, a_fox_f_bias, a_w_out,
              f_norm, f_w_gate, f_w_up, f_w_down,
              c_norm, c_w_in, c_w_a2, c_b_a, c_o_norm, c_w_out,
              m_norm, m_router, m_w_gate, m_w_up, m_w_down):
    for layer in range(DEPTH):
        i = layer // 2
        if layer % 2 == 0:
            x = x + nsa_fox_mixer(rmsnorm(x, a_norm[i]), a_w_in[i], a_q_norm[i], a_k_norm[i],
                                  a_pe_k[i], a_pe_v[i], a_cmp_k_w1[i], a_cmp_k_w2[i],
                                  a_cmp_v_w1[i], a_cmp_v_w2[i], a_gate_bias[i],
                                  a_fox_q_norm[i], a_fox_k_norm[i], a_fox_f_bias[i], a_w_out[i])
            x = x + swiglu(rmsnorm(x, f_norm[i]), f_w_gate[i], f_w_up[i], f_w_down[i])
        else:
            x = x + gla_mixer(rmsnorm(x, c_norm[i]), c_w_in[i], c_w_a2[i], c_b_a[i], c_o_norm[i], c_w_out[i])
            x = x + moe_ffn(rmsnorm(x, m_norm[i]), m_router[i], m_w_gate[i], m_w_up[i], m_w_down[i])
    return x
```

```python
import functools

import numpy as np
import jax
import jax.numpy as jnp
from jax import lax
from jax.experimental import pallas as pl
from jax.experimental.pallas import tpu as pltpu

F32 = jnp.float32
BF16 = jnp.bfloat16

D_MODEL = 1024
HEAD_DIM = 64
ROPE_DIM = 16
ROPE_THETA = 500000.0
NORM_EPS = 1e-6
NEG = -1e30
FORCE_SCORE = 1e6

NSA_HEADS = 8
CMP_LEN = 32
CMP_STRIDE = 16
SLC_BLOCK = 64
SLC_TOPK = 16
WINDOW = 512
FOX_HEADS = 8

GLA_HEADS = 4
GLA_DK = 512
GLA_DV = 1024
GLA_RANK = 16
GLA_TAU = 16.0
GLA_CHUNK = 64
GLA_SUB = 16

D_FF = 2816
N_EXPERTS = 8
TOP_K = 2

LANES = 128
FF_CHUNK = 256
VMEM_LIMIT = 56 * 1024 * 1024

ROW_TILE = 512
ATT_Q = 128
FOX_Q = 512
FOX_K = 256
SLC_K = 256
EXPERT_TILE = 512
MOVE_TILE = 256

NT_DIMS = (((1,), (1,)), ((), ()))


def _cp(sem, **kw):
    return pltpu.CompilerParams(dimension_semantics=sem, vmem_limit_bytes=VMEM_LIMIT, **kw)


def _dot(a, b):
    return jnp.dot(a, b, preferred_element_type=F32)


def _dot_nt(a, b):
    return lax.dot_general(a, b, NT_DIMS, preferred_element_type=F32)


def _split2(x):
    hi = x.astype(BF16)
    lo = (x - hi.astype(F32)).astype(BF16)
    return hi, lo


def _split3(x):
    hi = x.astype(BF16)
    r = x - hi.astype(F32)
    mid = r.astype(BF16)
    lo = (r - mid.astype(F32)).astype(BF16)
    return hi, mid, lo


def _rms_rows(x, gain):
    return x * lax.rsqrt(jnp.mean(x * x, axis=-1, keepdims=True) + NORM_EPS) * gain


def _log_sigmoid(x):
    return jnp.minimum(x, 0.0) - jnp.log1p(jnp.exp(-jnp.abs(x)))


def _sigmoid(x):
    return 1.0 / (1.0 + jnp.exp(-x))


def _seg_norm(y, gain, bd):
    ms = _dot((y * y).astype(BF16), bd)
    return y * lax.rsqrt(ms + NORM_EPS) * gain


def _rope(y, cos, sin, perm):
    return y * cos + _dot(y.astype(BF16), perm) * sin


def _block_diag_mean(width, seg):
    i = np.arange(width)
    return ((i[:, None] // seg) == (i[None, :] // seg)).astype(np.float32) / seg


def _rope_perm():
    p = np.zeros((LANES, LANES), np.float32)
    half = ROPE_DIM // 2
    for j in range(LANES):
        d = j % HEAD_DIM
        if d < half:
            p[j + half, j] = -1.0
        elif d < ROPE_DIM:
            p[j - half, j] = 1.0
    return p


def _rope_tables(pos):
    inv = ROPE_THETA ** (-np.arange(0, ROPE_DIM, 2, dtype=np.float64) / ROPE_DIM)
    ang = pos.astype(np.float64)[:, None] * inv[None, :]
    cos8, sin8 = np.cos(ang), np.sin(ang)
    n = pos.shape[0]
    ones = np.ones((n, HEAD_DIM - ROPE_DIM))
    zeros = np.zeros((n, HEAD_DIM - ROPE_DIM))
    c = np.concatenate([cos8, cos8, ones], axis=1)
    s = np.concatenate([sin8, sin8, zeros], axis=1)
    return (jnp.asarray(np.concatenate([c, c], axis=1), F32),
            jnp.asarray(np.concatenate([s, s], axis=1), F32))


def _proj_a_kernel(x_ref, g_ref, w_ref, hg_ref, cos_ref, sin_ref, bd_ref, perm_ref,
                   qa_ref, kv_ref, fox_ref, gf_ref):
    xn = _rms_rows(x_ref[...], g_ref[...]).astype(BF16)
    bd = bd_ref[...]
    perm = perm_ref[...]
    cos = cos_ref[...]
    sin = sin_ref[...]

    def tile(acc, i):
        return acc[:, i * LANES:(i + 1) * LANES]

    acc = _dot(xn, w_ref[:, 0:512])
    for i in range(4):
        y = _rope(_seg_norm(tile(acc, i), hg_ref[0:1, :], bd), cos, sin, perm)
        qa_ref[:, i * LANES:(i + 1) * LANES] = (y * 0.125).astype(BF16)

    acc = _dot(xn, w_ref[:, 512:1280])
    for i in range(6):
        y = tile(acc, i)
        if i in (2, 4):
            y = _rope(_seg_norm(y, hg_ref[1:2, :], bd), cos, sin, perm)
        kv_ref[:, i * LANES:(i + 1) * LANES] = y.astype(BF16)

    for c in range(3):
        acc = _dot(xn, w_ref[:, 1280 + 512 * c:1280 + 512 * (c + 1)])
        for i in range(4):
            y = tile(acc, i)
            if c == 0:
                y = _seg_norm(y, hg_ref[2:3, :], bd) * 0.125
            elif c == 1:
                y = _seg_norm(y, hg_ref[3:4, :], bd)
            fox_ref[:, 512 * c + i * LANES:512 * c + (i + 1) * LANES] = y.astype(BF16)

    gf_ref[...] = _dot(xn, w_ref[:, 2816:2944])


def _proj_a(x2d, norm_g, w_perm, head_gains, cos, sin, seq):
    t = x2d.shape[0]
    tm = ROW_TILE
    n_pos = seq // tm
    bd = jnp.asarray(_block_diag_mean(LANES, HEAD_DIM), BF16)
    perm = jnp.asarray(_rope_perm(), BF16)
    full = lambda i: (0, 0)
    return pl.pallas_call(
        _proj_a_kernel,
        grid=(t // tm,),
        in_specs=[
            pl.BlockSpec((tm, D_MODEL), lambda i: (i, 0)),
            pl.BlockSpec((1, D_MODEL), full),
            pl.BlockSpec((D_MODEL, 2944), full),
            pl.BlockSpec((8, LANES), full),
            pl.BlockSpec((tm, LANES), lambda i: (i % n_pos, 0)),
            pl.BlockSpec((tm, LANES), lambda i: (i % n_pos, 0)),
            pl.BlockSpec((LANES, LANES), full),
            pl.BlockSpec((LANES, LANES), full),
        ],
        out_specs=[
            pl.BlockSpec((tm, 512), lambda i: (i, 0)),
            pl.BlockSpec((tm, 768), lambda i: (i, 0)),
            pl.BlockSpec((tm, 1536), lambda i: (i, 0)),
            pl.BlockSpec((tm, LANES), lambda i: (i, 0)),
        ],
        out_shape=[
            jax.ShapeDtypeStruct((t, 512), BF16),
            jax.ShapeDtypeStruct((t, 768), BF16),
            jax.ShapeDtypeStruct((t, 1536), BF16),
            jax.ShapeDtypeStruct((t, LANES), F32),
        ],
        compiler_params=_cp(("parallel",)),
        name="proj_a",
    )(x2d, norm_g, w_perm, head_gains, cos, sin, bd, perm)


CUM_CHUNK = 256


def _cum_kernel(gf_ref, bias_ref, tri_ref, out_ref):
    tri = tri_ref[...]
    n_chunk = gf_ref.shape[0] // CUM_CHUNK
    carry = jnp.zeros((1, LANES), F32)
    for c in range(n_chunk):
        rows = slice(c * CUM_CHUNK, (c + 1) * CUM_CHUNK)
        lf = _log_sigmoid(gf_ref[rows, :] + bias_ref[...])
        hi, mid, lo = _split3(lf)
        cs = _dot(tri, hi) + _dot(tri, mid) + _dot(tri, lo) + carry
        out_ref[rows, :] = cs
        carry = cs[CUM_CHUNK - 1:CUM_CHUNK, :]


def _fox_cumlog(gf3, bias_row):
    b, s, _ = gf3.shape
    tri = jnp.asarray(np.tril(np.ones((CUM_CHUNK, CUM_CHUNK), np.float32)), BF16)
    return pl.pallas_call(
        _cum_kernel,
        grid=(b,),
        in_specs=[
            pl.BlockSpec((None, s, LANES), lambda i: (i, 0, 0)),
            pl.BlockSpec((1, LANES), lambda i: (0, 0)),
            pl.BlockSpec((CUM_CHUNK, CUM_CHUNK), lambda i: (0, 0)),
        ],
        out_specs=pl.BlockSpec((None, s, LANES), lambda i: (i, 0, 0)),
        out_shape=jax.ShapeDtypeStruct((b, s, LANES), F32),
        compiler_params=_cp(("parallel",)),
        name="fox_cumlog",
    )(gf3, bias_row, tri)


def _gelu_tanh(x):
    return 0.5 * x * (1.0 + jnp.tanh(0.7978845608028654 * (x + 0.044715 * x * x * x)))


def _cmp_kernel(kx_ref, vx_ref, pek_ref, pev_ref, w1k_ref, w2k_ref, w1v_ref, w2v_ref,
                kg_ref, cos_ref, sin_ref, bd_ref, perm_ref, kc_ref, vc_ref):
    ncp = kx_ref.shape[0]

    def compress(x_ref, pe_ref, w1_ref, w2_ref):
        x = x_ref[...].astype(F32)
        first = _dot((x + pe_ref[0:1, :]).astype(BF16), w1_ref[0])
        second = _dot((x + pe_ref[1:2, :]).astype(BF16), w1_ref[1])
        pre = first + pltpu.roll(second, ncp - 1, 0)
        return _dot(_gelu_tanh(pre).astype(BF16), w2_ref[...])

    k = compress(kx_ref, pek_ref, w1k_ref, w2k_ref)
    k = _rope(_seg_norm(k, kg_ref[...], bd_ref[...]), cos_ref[...], sin_ref[...], perm_ref[...])
    kc_ref[...] = k.astype(BF16)
    vc_ref[...] = compress(vx_ref, pev_ref, w1v_ref, w2v_ref).astype(BF16)


def _nsa_compress(kx, vx, pek, pev, w1k, w2k, w1v, w2v, kgain, ccos, csin):
    b, ncp, width = kx.shape
    bd = jnp.asarray(_block_diag_mean(LANES, HEAD_DIM), BF16)
    perm = jnp.asarray(_rope_perm(), BF16)
    c2 = lambda i: (0, 0)
    c3 = lambda i: (0, 0, 0)
    return pl.pallas_call(
        _cmp_kernel,
        grid=(b,),
        in_specs=[
            pl.BlockSpec((None, ncp, width), lambda i: (i, 0, 0)),
            pl.BlockSpec((None, ncp, width), lambda i: (i, 0, 0)),
            pl.BlockSpec((2, width), c2),
            pl.BlockSpec((2, width), c2),
            pl.BlockSpec((2, width, LANES), c3),
            pl.BlockSpec((LANES, LANES), c2),
            pl.BlockSpec((2, width, LANES), c3),
            pl.BlockSpec((LANES, LANES), c2),
            pl.BlockSpec((1, LANES), c2),
            pl.BlockSpec((ncp, LANES), c2),
            pl.BlockSpec((ncp, LANES), c2),
            pl.BlockSpec((LANES, LANES), c2),
            pl.BlockSpec((LANES, LANES), c2),
        ],
        out_specs=[
            pl.BlockSpec((None, ncp, LANES), lambda i: (i, 0, 0)),
            pl.BlockSpec((None, ncp, LANES), lambda i: (i, 0, 0)),
        ],
        out_shape=[
            jax.ShapeDtypeStruct((b, ncp, LANES), BF16),
            jax.ShapeDtypeStruct((b, ncp, LANES), BF16),
        ],
        compiler_params=_cp(("parallel",)),
        name="nsa_compress",
    )(kx, vx, pek, pev, w1k, w2k, w1v, w2v, kgain, ccos, csin, bd, perm)


def _nsa_cmp_kernel(q_ref, kc_ref, vc_ref, ovt_ref, eye_ref, oc_ref, sb_ref, *, n_cmp, n_slc):
    tq = ATT_Q
    q0 = pl.program_id(1) * tq
    kc = kc_ref[...]
    vc = vc_ref[...]
    ncp = kc.shape[0]
    ovt = ovt_ref[...]
    lo_half = lax.broadcasted_iota(jnp.int32, (tq, LANES), 1) < HEAD_DIM

    n_lane = lax.broadcasted_iota(jnp.int32, (tq, ncp), 1)
    t_sub = q0 + lax.broadcasted_iota(jnp.int32, (tq, ncp), 0)
    mask = (n_lane * CMP_STRIDE + (CMP_LEN - 1) <= t_sub) & (n_lane < n_cmp)
    n_sub = lax.broadcasted_iota(jnp.int32, (ncp, tq), 0)
    t_lane = q0 + lax.broadcasted_iota(jnp.int32, (ncp, tq), 1)
    mask_t = (n_sub * CMP_STRIDE + (CMP_LEN - 1) <= t_lane) & (n_sub < n_cmp)

    imp_t = [jnp.zeros((64, tq), F32), jnp.zeros((64, tq), F32)]
    for t in range(4):
        qt = q_ref[:, t * LANES:(t + 1) * LANES]
        outs = []
        for g in range(2):
            qm = jnp.where(lo_half if g == 0 else jnp.logical_not(lo_half), qt, jnp.zeros_like(qt))
            s = jnp.where(mask, _dot_nt(qm, kc), NEG)
            p = jnp.where(mask, jnp.exp(s - jnp.max(s, axis=-1, keepdims=True)), 0.0)
            p = p / jnp.maximum(jnp.sum(p, axis=-1, keepdims=True), 1e-30)
            outs.append(_dot(p.astype(BF16), vc))
            st = jnp.where(mask_t, _dot_nt(kc, qm), NEG)
            pt = jnp.where(mask_t, jnp.exp(st - jnp.max(st, axis=0, keepdims=True)), 0.0)
            pt = pt / jnp.maximum(jnp.sum(pt, axis=0, keepdims=True), 1e-30)
            hi, lo = _split2(pt)
            imp_t[g] = imp_t[g] + _dot(ovt, hi) + _dot(ovt, lo)
        oc_ref[:, t * LANES:(t + 1) * LANES] = jnp.where(lo_half, outs[0], outs[1]).astype(BF16)

    blk = lax.broadcasted_iota(jnp.int32, (64, tq), 0)
    cur = (q0 + lax.broadcasted_iota(jnp.int32, (64, tq), 1)) >> 6
    forced = (blk == 0) | (blk == cur) | (blk == cur - 1)
    future = blk > cur
    exists = blk < n_slc
    biases = []
    for g in range(2):
        v = jnp.where(forced, FORCE_SCORE, jnp.where(future, -FORCE_SCORE, imp_t[g]))
        v = jnp.where(exists, v, -3e38)
        sel = jnp.zeros((64, tq), jnp.int32)
        for _ in range(min(SLC_TOPK, n_slc)):
            m = jnp.max(v, axis=0, keepdims=True)
            first = jnp.min(jnp.where(v == m, blk, 64), axis=0, keepdims=True)
            hit = blk == first
            sel = jnp.where(hit, 1, sel)
            v = jnp.where(hit, -3.2e38, v)
        keep = (sel == 1) & jnp.logical_not(future) & exists
        biases.append(jnp.where(keep, 0.0, NEG).astype(BF16))
    sb_t = jnp.concatenate([biases[1], biases[0]], axis=0)
    sb_ref[...] = _dot_nt(eye_ref[...], sb_t).astype(BF16)


def _nsa_cmp_select(qa3, kcmp, vcmp, ovt, n_cmp, n_slc):
    b, s, _ = qa3.shape
    ncp = kcmp.shape[1]
    eye = jnp.asarray(np.eye(ATT_Q, dtype=np.float32), BF16)
    kern = functools.partial(_nsa_cmp_kernel, n_cmp=n_cmp, n_slc=n_slc)
    return pl.pallas_call(
        kern,
        grid=(b, s // ATT_Q),
        in_specs=[
            pl.BlockSpec((None, ATT_Q, 512), lambda i, j: (i, j, 0)),
            pl.BlockSpec((None, ncp, LANES), lambda i, j: (i, 0, 0)),
            pl.BlockSpec((None, ncp, LANES), lambda i, j: (i, 0, 0)),
            pl.BlockSpec((64, ncp), lambda i, j: (0, 0)),
            pl.BlockSpec((ATT_Q, ATT_Q), lambda i, j: (0, 0)),
        ],
        out_specs=[
            pl.BlockSpec((None, ATT_Q, 512), lambda i, j: (i, j, 0)),
            pl.BlockSpec((None, ATT_Q, LANES), lambda i, j: (i, j, 0)),
        ],
        out_shape=[
            jax.ShapeDtypeStruct((b, s, 512), BF16),
            jax.ShapeDtypeStruct((b, s, LANES), BF16),
        ],
        compiler_params=_cp(("parallel", "parallel")),
        name="nsa_cmp_select",
    )(qa3, kcmp, vcmp, ovt, eye)


def _flash_update(carry, s, v):
    m, l, acc = carry
    m_new = jnp.maximum(m, jnp.max(s, axis=-1, keepdims=True))
    alpha = jnp.exp(m - m_new)
    p = jnp.exp(s - m_new)
    l = alpha * l + jnp.sum(p, axis=-1, keepdims=True)
    acc = alpha * acc + _dot(p.astype(BF16), v)
    return m_new, l, acc


def _flash_init(rows):
    return (jnp.full((rows, 1), NEG, F32), jnp.zeros((rows, 1), F32), jnp.zeros((rows, LANES), F32))


def _nsa_slc_kernel(q_ref, sb_ref, k0_ref, k1_ref, v_ref, o_ref):
    tq, tk = ATT_Q, SLC_K
    q0 = pl.program_id(1) * tq
    sb = sb_ref[...]
    lo_half = lax.broadcasted_iota(jnp.int32, (tq, LANES), 1) < HEAD_DIM
    rows = 4 * tq
    n_before = q0 // tk
    t_pos = q0 + (lax.broadcasted_iota(jnp.int32, (rows, tk), 0) & (tq - 1))
    k_off = lax.broadcasted_iota(jnp.int32, (rows, tk), 1)

    outs = []
    for g in range(2):
        k_ref = k0_ref if g == 0 else k1_ref
        parts = []
        for t in range(4):
            qt = q_ref[:, t * LANES:(t + 1) * LANES]
            parts.append(jnp.where(lo_half, qt, sb) if g == 0 else jnp.where(lo_half, sb, qt))
        qq = jnp.concatenate(parts, axis=0)

        def scores(j):
            off = pl.multiple_of(j * tk, tk)
            return _dot_nt(qq, k_ref[pl.ds(off, tk), :]), v_ref[pl.ds(off, tk), :], off

        def body(j, carry):
            s, v, _ = scores(j)
            return _flash_update(carry, s, v)

        carry = lax.fori_loop(0, n_before, body, _flash_init(rows))
        s, v, off = scores(n_before)
        s = jnp.where(off + k_off <= t_pos, s, NEG)
        _, l, acc = _flash_update(carry, s, v)
        outs.append(acc / l)
    for t in range(4):
        rs = slice(t * tq, (t + 1) * tq)
        o_ref[:, t * LANES:(t + 1) * LANES] = jnp.where(lo_half, outs[0][rs], outs[1][rs]).astype(BF16)


def _nsa_selected(qa3, sb, k_aug0, k_aug1, v2):
    b, s, _ = qa3.shape
    blk = lambda i, j: (i, j, 0)
    whole = lambda i, j: (i, 0, 0)
    return pl.pallas_call(
        _nsa_slc_kernel,
        grid=(b, s // ATT_Q),
        in_specs=[
            pl.BlockSpec((None, ATT_Q, 512), blk),
            pl.BlockSpec((None, ATT_Q, LANES), blk),
            pl.BlockSpec((None, s, LANES), whole),
            pl.BlockSpec((None, s, LANES), whole),
            pl.BlockSpec((None, s, LANES), whole),
        ],
        out_specs=pl.BlockSpec((None, ATT_Q, 512), blk),
        out_shape=jax.ShapeDtypeStruct((b, s, 512), BF16),
        compiler_params=_cp(("parallel", "parallel")),
        name="nsa_selected",
    )(qa3, sb, k_aug0, k_aug1, v2)


def _nsa_win_kernel(q_ref, k_ref, v_ref, o_ref):
    tq = ATT_Q
    span = WINDOW + tq
    q0 = pl.program_id(1) * tq
    start = pl.multiple_of(jnp.maximum(q0 - WINDOW, 0), tq)
    k = k_ref[pl.ds(start, span), :]
    v = v_ref[pl.ds(start, span), :]
    rows = 4 * tq
    lo_half = lax.broadcasted_iota(jnp.int32, (tq, LANES), 1) < HEAD_DIM
    t_pos = q0 + (lax.broadcasted_iota(jnp.int32, (rows, span), 0) & (tq - 1))
    dist = t_pos - (start + lax.broadcasted_iota(jnp.int32, (rows, span), 1))
    mask = (dist >= 0) & (dist < WINDOW)
    outs = []
    for g in range(2):
        keep = lo_half if g == 0 else jnp.logical_not(lo_half)
        parts = []
        for t in range(4):
            qt = q_ref[:, t * LANES:(t + 1) * LANES]
            parts.append(jnp.where(keep, qt, jnp.zeros_like(qt)))
        qq = jnp.concatenate(parts, axis=0)
        s = jnp.where(mask, _dot_nt(qq, k), NEG)
        p = jnp.where(mask, jnp.exp(s - jnp.max(s, axis=-1, keepdims=True)), 0.0)
        l = jnp.sum(p, axis=-1, keepdims=True)
        outs.append(_dot(p.astype(BF16), v) / l)
    for t in range(4):
        rs = slice(t * tq, (t + 1) * tq)
        o_ref[:, t * LANES:(t + 1) * LANES] = jnp.where(lo_half, outs[0][rs], outs[1][rs]).astype(BF16)


def _nsa_window(qa3, k2, v2):
    b, s, _ = qa3.shape
    blk = lambda i, j: (i, j, 0)
    whole = lambda i, j: (i, 0, 0)
    return pl.pallas_call(
        _nsa_win_kernel,
        grid=(b, s // ATT_Q),
        in_specs=[
            pl.BlockSpec((None, ATT_Q, 512), blk),
            pl.BlockSpec((None, s, LANES), whole),
            pl.BlockSpec((None, s, LANES), whole),
        ],
        out_specs=pl.BlockSpec((None, ATT_Q, 512), blk),
        out_shape=jax.ShapeDtypeStruct((b, s, 512), BF16),
        compiler_params=_cp(("parallel", "parallel")),
        name="nsa_window",
    )(qa3, k2, v2)


def _fox_kernel(q_ref, k_ref, v_ref, cc_ref, cr_ref, o_ref):
    tq, tk = FOX_Q, FOX_K
    q0 = pl.program_id(1) * tq
    n_before = q0 // tk
    n_diag = tq // tk
    lo_half = lax.broadcasted_iota(jnp.int32, (tq, LANES), 1) < HEAD_DIM
    t_pos = q0 + lax.broadcasted_iota(jnp.int32, (tq, tk), 0)
    k_off = lax.broadcasted_iota(jnp.int32, (tq, tk), 1)
    for pair in range(FOX_HEADS // 2):
        cols = slice(pair * LANES, (pair + 1) * LANES)
        qt = q_ref[:, cols]
        outs = []
        for hh in range(2):
            h = 2 * pair + hh
            qm = jnp.where(lo_half if hh == 0 else jnp.logical_not(lo_half), qt, jnp.zeros_like(qt))
            cq = cc_ref[:, h:h + 1]

            def scores(j):
                off = pl.multiple_of(j * tk, tk)
                s = _dot_nt(qm, k_ref[pl.ds(off, tk), cols])
                s = s + (cq - cr_ref[h:h + 1, pl.ds(off, tk)])
                return s, v_ref[pl.ds(off, tk), cols], off

            def body(j, carry):
                s, v, _ = scores(j)
                return _flash_update(carry, s, v)

            carry = lax.fori_loop(0, n_before, body, _flash_init(tq))
            for d in range(n_diag):
                s, v, off = scores(n_before + d)
                s = jnp.where(off + k_off <= t_pos, s, NEG)
                carry = _flash_update(carry, s, v)
            outs.append(carry[2] / carry[1])
        o_ref[:, cols] = jnp.where(lo_half, outs[0], outs[1]).astype(BF16)


def _fox_attention(fox3, cum_col, cum_row):
    b, s, _ = fox3.shape
    return pl.pallas_call(
        _fox_kernel,
        grid=(b, s // FOX_Q),
        in_specs=[
            pl.BlockSpec((None, FOX_Q, 512), lambda i, j: (i, j, 0)),
            pl.BlockSpec((None, s, 512), lambda i, j: (i, 0, 1)),
            pl.BlockSpec((None, s, 512), lambda i, j: (i, 0, 2)),
            pl.BlockSpec((None, FOX_Q, FOX_HEADS), lambda i, j: (i, j, 0)),
            pl.BlockSpec((None, FOX_HEADS, s), lambda i, j: (i, 0, 0)),
        ],
        out_specs=pl.BlockSpec((None, FOX_Q, 512), lambda i, j: (i, j, 0)),
        out_shape=jax.ShapeDtypeStruct((b, s, 512), BF16),
        compiler_params=_cp(("parallel", "parallel")),
        name="fox_attention",
    )(fox3, fox3, fox3, cum_col, cum_row)


def _out_a_kernel(oc_ref, os_ref, ow_ref, ob_ref, gf_ref, gb_ref, gx_ref, x_ref, w_ref, o_ref):
    gates = _sigmoid(gf_ref[...] + gb_ref[...])
    hi, lo = _split2(gates)
    o_a = None
    for j, br in enumerate((oc_ref, os_ref, ow_ref)):
        gexp = _dot(hi, gx_ref[j]) + _dot(lo, gx_ref[j])
        term = gexp * br[...].astype(F32)
        o_a = term if o_a is None else o_a + term
    y = _dot(o_a.astype(BF16), w_ref[0:512, :]) + _dot(ob_ref[...], w_ref[512:1024, :])
    o_ref[...] = x_ref[...] + y


def _out_a(oc, os_, ow, ob, gf, gate_bias_row, gate_expand, x2d, w_out_perm):
    t = x2d.shape[0]
    tm = ROW_TILE
    row = lambda i: (i, 0)
    c2 = lambda i: (0, 0)
    return pl.pallas_call(
        _out_a_kernel,
        grid=(t // tm,),
        in_specs=[
            pl.BlockSpec((tm, 512), row),
            pl.BlockSpec((tm, 512), row),
            pl.BlockSpec((tm, 512), row),
            pl.BlockSpec((tm, 512), row),
            pl.BlockSpec((tm, LANES), row),
            pl.BlockSpec((1, LANES), c2),
            pl.BlockSpec((3, LANES, 512), lambda i: (0, 0, 0)),
            pl.BlockSpec((tm, D_MODEL), row),
            pl.BlockSpec((D_MODEL, D_MODEL), c2),
        ],
        out_specs=pl.BlockSpec((tm, D_MODEL), row),
        out_shape=jax.ShapeDtypeStruct((t, D_MODEL), F32),
        compiler_params=_cp(("parallel",)),
        name="out_a",
    )(oc, os_, ow, ob, gf, gate_bias_row, gate_expand, x2d, w_out_perm)


def _swiglu_body(xn, wg_ref, wu_ref, wd_ref, h_ref):
    for c in range(D_FF // FF_CHUNK):
        cols = slice(c * FF_CHUNK, (c + 1) * FF_CHUNK)
        g = _dot(xn, wg_ref[:, cols])
        u = _dot(xn, wu_ref[:, cols])
        h_ref[:, cols] = (g * _sigmoid(g) * u).astype(BF16)
    return _dot(h_ref[...], wd_ref[...])


def _ffn_dense_kernel(x_ref, g_ref, wg_ref, wu_ref, wd_ref, o_ref, h_ref):
    x = x_ref[...]
    xn = _rms_rows(x, g_ref[...]).astype(BF16)
    o_ref[...] = x + _swiglu_body(xn, wg_ref, wu_ref, wd_ref, h_ref)


def _ffn_dense(x2d, norm_g, wg, wu, wd):
    t = x2d.shape[0]
    tm = ROW_TILE
    c2 = lambda i: (0, 0)
    return pl.pallas_call(
        _ffn_dense_kernel,
        grid=(t // tm,),
        in_specs=[
            pl.BlockSpec((tm, D_MODEL), lambda i: (i, 0)),
            pl.BlockSpec((1, D_MODEL), c2),
            pl.BlockSpec((D_MODEL, D_FF), c2),
            pl.BlockSpec((D_MODEL, D_FF), c2),
            pl.BlockSpec((D_FF, D_MODEL), c2),
        ],
        out_specs=pl.BlockSpec((tm, D_MODEL), lambda i: (i, 0)),
        out_shape=jax.ShapeDtypeStruct((t, D_MODEL), F32),
        scratch_shapes=[pltpu.VMEM((tm, D_FF), BF16)],
        compiler_params=_cp(("parallel",)),
        name="ffn_dense",
    )(x2d, norm_g, wg, wu, wd)


def _ffn_expert_kernel(te_ref, nu_ref, x_ref, wg_ref, wu_ref, wd_ref, o_ref, h_ref):
    @pl.when(pl.program_id(0) < nu_ref[0])
    def _():
        o_ref[...] = _swiglu_body(x_ref[...], wg_ref, wu_ref, wd_ref, h_ref).astype(BF16)

    @pl.when(pl.program_id(0) >= nu_ref[0])
    def _():
        o_ref[...] = jnp.zeros_like(o_ref)


def _ffn_experts(tile_expert, n_used, xs, wg, wu, wd):
    n_slot = xs.shape[0]
    tm = EXPERT_TILE
    n_tiles = n_slot // tm
    row = lambda i, te, nu: (jnp.minimum(i, nu[0] - 1), 0)
    wsel = lambda i, te, nu: (te[i], 0, 0)
    grid_spec = pltpu.PrefetchScalarGridSpec(
        num_scalar_prefetch=2,
        grid=(n_tiles,),
        in_specs=[
            pl.BlockSpec((tm, D_MODEL), row),
            pl.BlockSpec((None, D_MODEL, D_FF), wsel),
            pl.BlockSpec((None, D_MODEL, D_FF), wsel),
            pl.BlockSpec((None, D_FF, D_MODEL), wsel),
        ],
        out_specs=pl.BlockSpec((tm, D_MODEL), lambda i, te, nu: (i, 0)),
        scratch_shapes=[pltpu.VMEM((tm, D_FF), BF16)],
    )
    return pl.pallas_call(
        _ffn_expert_kernel,
        grid_spec=grid_spec,
        out_shape=jax.ShapeDtypeStruct((n_slot, D_MODEL), BF16),
        compiler_params=_cp(("arbitrary",)),
        name="ffn_experts",
    )(tile_expert, n_used, xs, wg, wu, wd)


def _proj_c_kernel(x_ref, g_ref, w_ref, wa_ref, ba_ref, qk_ref, v_ref, r_ref, la_ref):
    xn = _rms_rows(x_ref[...], g_ref[...]).astype(BF16)
    for c in range(2):
        qk_ref[:, 512 * c:512 * (c + 1)] = _dot(xn, w_ref[:, 512 * c:512 * (c + 1)]).astype(BF16)
    for c in range(2):
        v_ref[:, 512 * c:512 * (c + 1)] = _dot(xn, w_ref[:, 1024 + 512 * c:1536 + 512 * c]).astype(BF16)
    for c in range(2):
        r_ref[:, 512 * c:512 * (c + 1)] = _dot(xn, w_ref[:, 2048 + 512 * c:2560 + 512 * c]).astype(BF16)
    a1 = _dot(xn, w_ref[:, 3072:3200])
    hi, lo = _split2(a1)
    wa = wa_ref[...]
    pre = _dot(hi, wa) + _dot(lo, wa) + ba_ref[...]
    la_ref[...] = _log_sigmoid(pre) * (1.0 / GLA_TAU)


def _proj_c(x2d, norm_g, w_pad, wa_pad, ba_row):
    t = x2d.shape[0]
    tm = ROW_TILE
    row = lambda i: (i, 0)
    c2 = lambda i: (0, 0)
    return pl.pallas_call(
        _proj_c_kernel,
        grid=(t // tm,),
        in_specs=[
            pl.BlockSpec((tm, D_MODEL), row),
            pl.BlockSpec((1, D_MODEL), c2),
            pl.BlockSpec((D_MODEL, 3200), c2),
            pl.BlockSpec((LANES, GLA_DK), c2),
            pl.BlockSpec((1, GLA_DK), c2),
        ],
        out_specs=[
            pl.BlockSpec((tm, 1024), row),
            pl.BlockSpec((tm, 1024), row),
            pl.BlockSpec((tm, 1024), row),
            pl.BlockSpec((tm, GLA_DK), row),
        ],
        out_shape=[
            jax.ShapeDtypeStruct((t, 1024), BF16),
            jax.ShapeDtypeStruct((t, 1024), BF16),
            jax.ShapeDtypeStruct((t, 1024), BF16),
            jax.ShapeDtypeStruct((t, GLA_DK), F32),
        ],
        compiler_params=_cp(("parallel",)),
        name="proj_c",
    )(x2d, norm_g, w_pad, wa_pad, ba_row)


def _gla_kernel(q_ref, k_ref, v_ref, g_ref, tri_ref, eye_ref, o_ref, st_ref):
    c_len, sub = GLA_CHUNK, GLA_SUB
    n_sub = c_len // sub
    dk = q_ref.shape[1]
    dv = v_ref.shape[1]
    scale = dk ** -0.5
    tri = tri_ref[...]
    eye = eye_ref[...]
    st_ref[...] = jnp.zeros_like(st_ref)
    i_loc = lax.broadcasted_iota(jnp.int32, (c_len, 1), 0) & (sub - 1)

    def chunk(c, _):
        r0 = pl.multiple_of(c * c_len, c_len)
        q = q_ref[pl.ds(r0, c_len), :].astype(F32) * scale
        k = k_ref[pl.ds(r0, c_len), :].astype(F32)
        vb = v_ref[pl.ds(r0, c_len), :]
        v = vb.astype(F32)
        hi, mid, lo = _split3(g_ref[pl.ds(r0, c_len), :])
        b = _dot(tri, hi) + _dot(tri, mid) + _dot(tri, lo)
        b_last = b[c_len - 1:c_len, :]

        st_t = st_ref[...]
        o = _dot_nt((q * jnp.exp(b)).astype(BF16), st_t.astype(BF16))

        for i in range(1, n_sub):
            r = b[i * sub:i * sub + 1, :]
            qi = (q[i * sub:(i + 1) * sub] * jnp.exp(b[i * sub:(i + 1) * sub] - r)).astype(BF16)
            kj = (k[:i * sub] * jnp.exp(r - b[:i * sub])).astype(BF16)
            att = _dot_nt(qi, kj).astype(BF16)
            o_i = _dot(att, vb[:i * sub])
            pad = [jnp.zeros((i * sub, dv), F32), o_i]
            if i + 1 < n_sub:
                pad.append(jnp.zeros((c_len - (i + 1) * sub, dv), F32))
            o = o + jnp.concatenate(pad, axis=0)

        for j in range(sub):
            def rows(x):
                return jnp.concatenate(
                    [jnp.broadcast_to(x[s * sub + j:s * sub + j + 1, :], (sub, x.shape[1]))
                     for s in range(n_sub)], axis=0)
            e = jnp.exp(jnp.minimum(b - rows(b), 0.0))
            a = jnp.sum(q * e * rows(k), axis=-1, keepdims=True)
            a = jnp.where(i_loc >= j, a, 0.0)
            o = o + a * rows(v)

        o_ref[pl.ds(r0, c_len), :] = o.astype(BF16)

        kd = (k * jnp.exp(b_last - b)).astype(BF16)
        v_t = _dot_nt(eye, vb).astype(BF16)
        st_ref[...] = st_t * jnp.exp(b_last) + _dot(v_t, kd)
        return 0

    lax.fori_loop(0, q_ref.shape[0] // c_len, chunk, 0)


def _gla(qk3, v3, la3):
    b, s, _ = qk3.shape
    dk = GLA_DK // GLA_HEADS
    dv = GLA_DV // GLA_HEADS
    tri = jnp.asarray(np.tril(np.ones((GLA_CHUNK, GLA_CHUNK), np.float32)), BF16)
    eye = jnp.asarray(np.eye(dv, dtype=np.float32), BF16)
    return pl.pallas_call(
        _gla_kernel,
        grid=(b, GLA_HEADS),
        in_specs=[
            pl.BlockSpec((None, s, dk), lambda i, h: (i, 0, h)),
            pl.BlockSpec((None, s, dk), lambda i, h: (i, 0, GLA_HEADS + h)),
            pl.BlockSpec((None, s, dv), lambda i, h: (i, 0, h)),
            pl.BlockSpec((None, s, dk), lambda i, h: (i, 0, h)),
            pl.BlockSpec((GLA_CHUNK, GLA_CHUNK), lambda i, h: (0, 0)),
            pl.BlockSpec((dv, dv), lambda i, h: (0, 0)),
        ],
        out_specs=pl.BlockSpec((None, s, dv), lambda i, h: (i, 0, h)),
        out_shape=jax.ShapeDtypeStruct((b, s, GLA_DV), BF16),
        scratch_shapes=[pltpu.VMEM((dv, dk), F32)],
        compiler_params=_cp(("parallel", "parallel")),
        name="gla",
    )(qk3, qk3, v3, la3, tri, eye)


def _out_c_kernel(o_ref, r_ref, x_ref, on_ref, bd_ref, w_ref, mg_ref, rt_ref,
                  x3_ref, xn_ref, e1_ref, e2_ref, w1_ref, w2_ref):
    dv = GLA_DV // GLA_HEADS
    bd = bd_ref[...]
    parts = []
    for h in range(GLA_HEADS):
        cols = slice(h * dv, (h + 1) * dv)
        oh = _seg_norm(o_ref[:, cols].astype(F32), on_ref[...], bd)
        r = r_ref[:, cols].astype(F32)
        parts.append((oh * (r * _sigmoid(r))).astype(BF16))
    x3 = x_ref[...] + _dot(jnp.concatenate(parts, axis=1), w_ref[...])
    x3_ref[...] = x3

    xn = _rms_rows(x3, mg_ref[...])
    xn_ref[...] = xn.astype(BF16)
    xh, xl = _split2(xn)
    logits = _dot(xh, rt_ref[0]) + _dot(xl, rt_ref[0]) + _dot(xh, rt_ref[1])
    lane = lax.broadcasted_iota(jnp.int32, logits.shape, 1)
    logits = jnp.where(lane < N_EXPERTS, logits, -3e38)
    v1 = jnp.max(logits, axis=-1, keepdims=True)
    i1 = jnp.min(jnp.where(logits == v1, lane, LANES), axis=-1, keepdims=True)
    rest = jnp.where(lane == i1, -3e38, logits)
    v2 = jnp.max(rest, axis=-1, keepdims=True)
    i2 = jnp.min(jnp.where(rest == v2, lane, LANES), axis=-1, keepdims=True)
    ex = jnp.exp(v2 - v1)
    den = 1.0 + ex
    e1_ref[...] = jnp.broadcast_to(i1, logits.shape)
    e2_ref[...] = jnp.broadcast_to(i2, logits.shape)
    w1_ref[...] = jnp.broadcast_to(1.0 / den, logits.shape)
    w2_ref[...] = jnp.broadcast_to(ex / den, logits.shape)


def _out_c(o2d, r2d, x2d, onorm_row, w_out, mnorm_row, router2):
    t = x2d.shape[0]
    tm = ROW_TILE
    dv = GLA_DV // GLA_HEADS
    bd = jnp.asarray(_block_diag_mean(dv, dv), BF16)
    row = lambda i: (i, 0)
    c2 = lambda i: (0, 0)
    slab = pl.BlockSpec((tm, LANES), row)
    return pl.pallas_call(
        _out_c_kernel,
        grid=(t // tm,),
        in_specs=[
            pl.BlockSpec((tm, GLA_DV), row),
            pl.BlockSpec((tm, GLA_DV), row),
            pl.BlockSpec((tm, D_MODEL), row),
            pl.BlockSpec((1, dv), c2),
            pl.BlockSpec((dv, dv), c2),
            pl.BlockSpec((D_MODEL, D_MODEL), c2),
            pl.BlockSpec((1, D_MODEL), c2),
            pl.BlockSpec((2, D_MODEL, LANES), lambda i: (0, 0, 0)),
        ],
        out_specs=[
            pl.BlockSpec((tm, D_MODEL), row),
            pl.BlockSpec((tm, D_MODEL), row),
            slab, slab, slab, slab,
        ],
        out_shape=[
            jax.ShapeDtypeStruct((t, D_MODEL), F32),
            jax.ShapeDtypeStruct((t, D_MODEL), BF16),
            jax.ShapeDtypeStruct((t, LANES), jnp.int32),
            jax.ShapeDtypeStruct((t, LANES), jnp.int32),
            jax.ShapeDtypeStruct((t, LANES), F32),
            jax.ShapeDtypeStruct((t, LANES), F32),
        ],
        compiler_params=_cp(("parallel",)),
        name="out_c_router",
    )(o2d, r2d, x2d, onorm_row, bd, w_out, mnorm_row, router2)


def _row_copy(src_ref, src_i, dst_ref, dst_i, sem):
    return pltpu.make_async_copy(src_ref.at[src_i], dst_ref.at[dst_i], sem)


def _dispatch_kernel(dest_ref, x_ref, init_ref, xs_ref, sem):
    del init_ref
    n = x_ref.shape[0]

    def issue(r, _):
        for k in range(TOP_K):
            _row_copy(x_ref, r, xs_ref, dest_ref[0, 0, TOP_K * r + k], sem).start()
        return 0

    lax.fori_loop(0, n, issue, 0)

    def drain(r, _):
        for k in range(TOP_K):
            _row_copy(x_ref, r, xs_ref, 0, sem).wait()
        return 0

    lax.fori_loop(0, n, drain, 0)


def _moe_dispatch(dest3, xn3, n_slot):
    t = xn3.shape[0]
    tm = MOVE_TILE
    zeros = jnp.zeros((n_slot, 8, LANES), BF16)
    return pl.pallas_call(
        _dispatch_kernel,
        grid=(t // tm,),
        in_specs=[
            pl.BlockSpec((1, 1, TOP_K * tm), lambda i: (i, 0, 0), memory_space=pltpu.SMEM),
            pl.BlockSpec((tm, 8, LANES), lambda i: (i, 0, 0)),
            pl.BlockSpec(memory_space=pl.ANY),
        ],
        out_specs=pl.BlockSpec(memory_space=pl.ANY),
        out_shape=jax.ShapeDtypeStruct((n_slot, 8, LANES), BF16),
        scratch_shapes=[pltpu.SemaphoreType.DMA(())],
        input_output_aliases={2: 0},
        compiler_params=_cp(("arbitrary",), has_side_effects=True),
        name="moe_dispatch",
    )(dest3, xn3, zeros)


def _combine_kernel(dest_ref, ys_ref, x_ref, w1_ref, w2_ref, o_ref, ya_ref, yb_ref, sem_a, sem_b):
    n = x_ref.shape[0]

    def issue(r, _):
        _row_copy(ys_ref, dest_ref[0, 0, TOP_K * r], ya_ref, r, sem_a).start()
        _row_copy(ys_ref, dest_ref[0, 0, TOP_K * r + 1], yb_ref, r, sem_b).start()
        return 0

    lax.fori_loop(0, n, issue, 0)

    def drain(r, _):
        _row_copy(ys_ref, 0, ya_ref, r, sem_a).wait()
        _row_copy(ys_ref, 0, yb_ref, r, sem_b).wait()
        return 0

    lax.fori_loop(0, n, drain, 0)
    o_ref[...] = (x_ref[...] + w1_ref[...] * ya_ref[...].astype(F32)
                  + w2_ref[...] * yb_ref[...].astype(F32))


def _moe_combine(dest3, ys3, x3d, w1b, w2b):
    t = x3d.shape[0]
    tm = MOVE_TILE
    tok = lambda i: (i, 0, 0)
    return pl.pallas_call(
        _combine_kernel,
        grid=(t // tm,),
        in_specs=[
            pl.BlockSpec((1, 1, TOP_K * tm), tok, memory_space=pltpu.SMEM),
            pl.BlockSpec(memory_space=pl.ANY),
            pl.BlockSpec((tm, 8, LANES), tok),
            pl.BlockSpec((tm, 1, LANES), tok),
            pl.BlockSpec((tm, 1, LANES), tok),
        ],
        out_specs=pl.BlockSpec((tm, 8, LANES), tok),
        out_shape=jax.ShapeDtypeStruct((t, 8, LANES), F32),
        scratch_shapes=[
            pltpu.VMEM((tm, 8, LANES), BF16),
            pltpu.VMEM((tm, 8, LANES), BF16),
            pltpu.SemaphoreType.DMA(()),
            pltpu.SemaphoreType.DMA(()),
        ],
        compiler_params=_cp(("arbitrary",)),
        name="moe_combine",
    )(dest3, ys3, x3d, w1b, w2b)


def _nsa_head_perm():
    order = []
    for t in range(4):
        order += [t, 4 + t]
    cols = np.concatenate([np.arange(h * HEAD_DIM, (h + 1) * HEAD_DIM) for h in order])
    return order, cols


def _layer_even(x, a_norm, a_w_in, a_q_norm, a_k_norm, a_pe_k, a_pe_v, ck1, ck2, cv1, cv2,
                gate_bias, fq_norm, fk_norm, f_bias, a_w_out, f_norm, f_wg, f_wu, f_wd):
    b, s, d = x.shape
    t = b * s
    x2d = x.reshape(t, d)
    order, qcols = _nsa_head_perm()

    col = np.concatenate([
        qcols,
        np.arange(512, 1280),
        np.arange(1304, 2840),
        np.arange(1280, 1304),
        np.arange(2840, 2848),
    ])
    w = jnp.take(a_w_in, jnp.asarray(col), axis=1)
    w = jnp.pad(w, ((0, 0), (0, 2944 - w.shape[1]))).astype(BF16)

    tile2 = lambda g: jnp.concatenate([g, g])
    head_gains = jnp.stack([tile2(a_q_norm), tile2(a_k_norm), tile2(fq_norm), tile2(fk_norm)]
                           + [jnp.zeros((LANES,), F32)] * 4)
    cos, sin = _rope_tables(np.arange(s))
    qa, kv6, fox, gf = _proj_a(x2d, a_norm.reshape(1, d), w, head_gains, cos, sin, s)

    bias_row = jnp.zeros((1, LANES), F32).at[0, 24:32].set(f_bias)
    cum = _fox_cumlog(gf.reshape(b, s, LANES), bias_row)[:, :, 24:32]
    o_b = _fox_attention(fox.reshape(b, s, 1536), cum, cum.transpose(0, 2, 1))

    ncp = s // CMP_STRIDE
    n_cmp = (s - CMP_LEN) // CMP_STRIDE + 1
    n_slc = s // SLC_BLOCK
    kv3 = kv6.reshape(b, s, 768)
    kx = kv3[:, :, 0:128].reshape(b, ncp, CMP_STRIDE * LANES)
    vx = kv3[:, :, 128:256].reshape(b, ncp, CMP_STRIDE * LANES)

    def pe_rows(pe):
        p2 = jnp.concatenate([pe, pe], axis=1)
        return p2.reshape(2, CMP_STRIDE * LANES)

    def w1_blocks(w1):
        w4 = w1.reshape(2, CMP_STRIDE, HEAD_DIM, HEAD_DIM)
        z = jnp.zeros_like(w4)
        top = jnp.concatenate([w4, z], axis=-1)
        bot = jnp.concatenate([z, w4], axis=-1)
        return jnp.stack([top, bot], axis=2).reshape(2, CMP_STRIDE * LANES, LANES).astype(BF16)

    def w2_block(w2):
        z = jnp.zeros_like(w2)
        return jnp.concatenate([jnp.concatenate([w2, z], 1), jnp.concatenate([z, w2], 1)], 0).astype(BF16)

    ccos, csin = _rope_tables(np.arange(ncp) * CMP_STRIDE + CMP_LEN - 1)
    kcmp, vcmp = _nsa_compress(kx, vx, pe_rows(a_pe_k), pe_rows(a_pe_v), w1_blocks(ck1), w2_block(ck2),
                               w1_blocks(cv1), w2_block(cv2), tile2(a_k_norm).reshape(1, LANES), ccos, csin)

    c_start = np.arange(ncp) * CMP_STRIDE
    s_start = np.arange(64) * SLC_BLOCK
    ov = np.maximum(np.minimum(c_start[:, None] + CMP_LEN, s_start[None, :] + SLC_BLOCK)
                    - np.maximum(c_start[:, None], s_start[None, :]), 0).astype(np.float32) / CMP_LEN
    ov[n_cmp:, :] = 0.0
    ov[:, n_slc:] = 0.0
    qa3 = qa.reshape(b, s, 512)
    o_c, sb = _nsa_cmp_select(qa3, kcmp, vcmp, jnp.asarray(ov.T, BF16), n_cmp, n_slc)

    onehot = jnp.asarray((np.arange(s)[:, None] // SLC_BLOCK) == np.arange(64)[None, :], BF16)
    onehot = jnp.broadcast_to(onehot[None], (b, s, 64))
    ksl = kv3[:, :, 256:384]
    k_aug0 = jnp.concatenate([ksl[:, :, :64], onehot], axis=-1)
    k_aug1 = jnp.concatenate([onehot, ksl[:, :, 64:]], axis=-1)
    o_s = _nsa_selected(qa3, sb, k_aug0, k_aug1, kv3[:, :, 384:512])

    o_w = _nsa_window(qa3, kv3[:, :, 512:640], kv3[:, :, 640:768])

    gexp = np.zeros((3, LANES, 512), np.float32)
    for tile_i in range(4):
        for half, h in enumerate((tile_i, 4 + tile_i)):
            for j in range(3):
                gexp[j, h * 3 + j, tile_i * LANES + half * HEAD_DIM:tile_i * LANES + (half + 1) * HEAD_DIM] = 1.0
    gate_bias_row = jnp.zeros((1, LANES), F32).at[0, 0:24].set(gate_bias)
    w_out = jnp.concatenate([jnp.take(a_w_out[:512], jnp.asarray(qcols), axis=0), a_w_out[512:]], 0).astype(BF16)
    x1 = _out_a(o_c.reshape(t, 512), o_s.reshape(t, 512), o_w.reshape(t, 512), o_b.reshape(t, 512),
                gf, gate_bias_row, jnp.asarray(gexp, BF16), x2d, w_out)

    x2 = _ffn_dense(x1, f_norm.reshape(1, d), f_wg.astype(BF16), f_wu.astype(BF16), f_wd.astype(BF16))
    return x2.reshape(b, s, d)


def _layer_odd(x, c_norm, c_w_in, c_w_a2, c_b_a, c_o_norm, c_w_out, m_norm, m_router, m_wg, m_wu, m_wd):
    b, s, d = x.shape
    t = b * s
    x2d = x.reshape(t, d)
    w = jnp.pad(c_w_in, ((0, 0), (0, 3200 - c_w_in.shape[1]))).astype(BF16)
    wa = jnp.pad(c_w_a2, ((0, LANES - GLA_RANK), (0, 0))).astype(BF16)
    qk, v, r, la = _proj_c(x2d, c_norm.reshape(1, d), w, wa, c_b_a.reshape(1, GLA_DK))
    o = _gla(qk.reshape(b, s, 1024), v.reshape(b, s, 1024), la.reshape(b, s, GLA_DK))

    rt = jnp.pad(m_router, ((0, 0), (0, LANES - N_EXPERTS)))
    rt_hi = rt.astype(BF16)
    rt_lo = (rt - rt_hi.astype(F32)).astype(BF16)
    x3, xn, e1, e2, w1b, w2b = _out_c(o.reshape(t, GLA_DV), r, x2d, c_o_norm.reshape(1, -1),
                                      c_w_out.astype(BF16), m_norm.reshape(1, d), jnp.stack([rt_hi, rt_lo]))

    tm = EXPERT_TILE
    n_slot = TOP_K * t + N_EXPERTS * tm
    n_tiles = n_slot // tm
    flat_e = jnp.stack([e1[:, 0], e2[:, 0]], axis=1).reshape(-1)
    onehot = (flat_e[:, None] == jnp.arange(N_EXPERTS)[None, :]).astype(jnp.int32)
    csum = jnp.cumsum(onehot, axis=0)
    rank = jnp.sum(csum * onehot, axis=1) - 1
    counts = csum[-1]
    padded = ((counts + tm - 1) // tm) * tm
    ends = jnp.cumsum(padded)
    offs = ends - padded
    dest = (jnp.sum(onehot * offs[None, :], axis=1) + rank).astype(jnp.int32)
    n_used = (ends[-1] // tm).astype(jnp.int32)
    tile_start = jnp.arange(n_tiles, dtype=jnp.int32) * tm
    tile_e = jnp.sum((tile_start[:, None] >= ends[None, :]).astype(jnp.int32), axis=1)
    last_e = jnp.sum((((n_used - 1) * tm) >= ends).astype(jnp.int32))
    tile_e = jnp.where(jnp.arange(n_tiles) < n_used, tile_e, last_e).astype(jnp.int32)
    dest3 = dest.reshape(t // MOVE_TILE, 1, TOP_K * MOVE_TILE)

    xs = _moe_dispatch(dest3, xn.reshape(t, 8, LANES), n_slot)
    ys = _ffn_experts(tile_e, n_used.reshape(1), xs.reshape(n_slot, D_MODEL),
                      m_wg.astype(BF16), m_wu.astype(BF16), m_wd.astype(BF16))
    out = _moe_combine(dest3, ys.reshape(n_slot, 8, LANES), x3.reshape(t, 8, LANES),
                       w1b.reshape(t, 1, LANES), w2b.reshape(t, 1, LANES))
    return out.reshape(b, s, d)


def kernel(x, a_norm, a_w_in, a_q_norm, a_k_norm, a_pe_k, a_pe_v, a_cmp_k_w1, a_cmp_k_w2, a_cmp_v_w1, a_cmp_v_w2, a_gate_bias, a_fox_q_norm, a_fox_k_norm, a_fox_f_bias, a_w_out, f_norm, f_w_gate, f_w_up, f_w_down, c_norm, c_w_in, c_w_a2, c_b_a, c_o_norm, c_w_out, m_norm, m_router, m_w_gate, m_w_up, m_w_down):
    x = _layer_even(x, a_norm[0], a_w_in[0], a_q_norm[0], a_k_norm[0], a_pe_k[0], a_pe_v[0],
                    a_cmp_k_w1[0], a_cmp_k_w2[0], a_cmp_v_w1[0], a_cmp_v_w2[0], a_gate_bias[0],
                    a_fox_q_norm[0], a_fox_k_norm[0], a_fox_f_bias[0], a_w_out[0],
                    f_norm[0], f_w_gate[0], f_w_up[0], f_w_down[0])
    x = _layer_odd(x, c_norm[0], c_w_in[0], c_w_a2[0], c_b_a[0], c_o_norm[0], c_w_out[0],
                   m_norm[0], m_router[0], m_w_gate[0], m_w_up[0], m_w_down[0])
    return x
```

```python
import functools

import numpy as np
import jax
import jax.numpy as jnp
from jax import lax
from jax.experimental import pallas as pl
from jax.experimental.pallas import tpu as pltpu

F32 = jnp.float32
BF16 = jnp.bfloat16

D_MODEL = 1024
HEAD_DIM = 64
ROPE_DIM = 16
ROPE_THETA = 500000.0
NORM_EPS = 1e-6
NEG = -1e30
FORCE_SCORE = 1e6

NSA_HEADS = 8
CMP_LEN = 32
CMP_STRIDE = 16
SLC_BLOCK = 64
SLC_TOPK = 16
WINDOW = 512
FOX_HEADS = 8

GLA_HEADS = 4
GLA_DK = 512
GLA_DV = 1024
GLA_RANK = 16
GLA_TAU = 16.0
GLA_CHUNK = 64
GLA_SUB = 16

D_FF = 2816
N_EXPERTS = 8
TOP_K = 2

LANES = 128
FF_CHUNK = 256
VMEM_LIMIT = 56 * 1024 * 1024

ROW_TILE = 512
ATT_Q = 128
FOX_Q = 512
FOX_K = 256
SLC_K = 256
EXPERT_TILE = 512
MOVE_TILE = 256

NT_DIMS = (((1,), (1,)), ((), ()))


def _cp(sem, **kw):
    return pltpu.CompilerParams(dimension_semantics=sem, vmem_limit_bytes=VMEM_LIMIT, **kw)


def _dot(a, b):
    return jnp.dot(a, b, preferred_element_type=F32)


def _dot_nt(a, b):
    return lax.dot_general(a, b, NT_DIMS, preferred_element_type=F32)


def _split2(x):
    hi = x.astype(BF16)
    lo = (x - hi.astype(F32)).astype(BF16)
    return hi, lo


def _split3(x):
    hi = x.astype(BF16)
    r = x - hi.astype(F32)
    mid = r.astype(BF16)
    lo = (r - mid.astype(F32)).astype(BF16)
    return hi, mid, lo


def _rms_rows(x, gain):
    return x * lax.rsqrt(jnp.mean(x * x, axis=-1, keepdims=True) + NORM_EPS) * gain


def _log_sigmoid(x):
    return jnp.minimum(x, 0.0) - jnp.log1p(jnp.exp(-jnp.abs(x)))


def _sigmoid(x):
    return 1.0 / (1.0 + jnp.exp(-x))


def _seg_norm(y, gain, bd):
    ms = _dot((y * y).astype(BF16), bd)
    return y * lax.rsqrt(ms + NORM_EPS) * gain


def _rope(y, cos, sin, perm):
    return y * cos + _dot(y.astype(BF16), perm) * sin


def _block_diag_mean(width, seg):
    i = np.arange(width)
    return ((i[:, None] // seg) == (i[None, :] // seg)).astype(np.float32) / seg


def _rope_perm():
    p = np.zeros((LANES, LANES), np.float32)
    half = ROPE_DIM // 2
    for j in range(LANES):
        d = j % HEAD_DIM
        if d < half:
            p[j + half, j] = -1.0
        elif d < ROPE_DIM:
            p[j - half, j] = 1.0
    return p


def _rope_tables(pos):
    inv = ROPE_THETA ** (-np.arange(0, ROPE_DIM, 2, dtype=np.float64) / ROPE_DIM)
    ang = pos.astype(np.float64)[:, None] * inv[None, :]
    cos8, sin8 = np.cos(ang), np.sin(ang)
    n = pos.shape[0]
    ones = np.ones((n, HEAD_DIM - ROPE_DIM))
    zeros = np.zeros((n, HEAD_DIM - ROPE_DIM))
    c = np.concatenate([cos8, cos8, ones], axis=1)
    s = np.concatenate([sin8, sin8, zeros], axis=1)
    return (jnp.asarray(np.concatenate([c, c], axis=1), F32),
            jnp.asarray(np.concatenate([s, s], axis=1), F32))


PROJ_A_COLS = 3968
FOX_SLAB = 2560


def _proj_a_kernel(x_ref, g_ref, w_ref, hg_ref, cos_ref, sin_ref, bd_ref, bdlo_ref, perm_ref,
                   qa_ref, kv_ref, fox_ref, gf_ref):
    xn = _rms_rows(x_ref[...], g_ref[...]).astype(BF16)
    bd = bd_ref[...]
    perm = perm_ref[...]
    cos = cos_ref[...]
    sin = sin_ref[...]

    def tile(acc, i):
        return acc[:, i * LANES:(i + 1) * LANES]

    acc = _dot(xn, w_ref[:, 0:512])
    for i in range(4):
        y = _rope(_seg_norm(tile(acc, i), hg_ref[0:1, :], bd), cos, sin, perm)
        qa_ref[:, i * LANES:(i + 1) * LANES] = (y * 0.125).astype(BF16)

    acc = _dot(xn, w_ref[:, 512:1280])
    for i in range(6):
        y = tile(acc, i)
        if i in (2, 4):
            y = _rope(_seg_norm(y, hg_ref[1:2, :], bd), cos, sin, perm)
        kv_ref[:, i * LANES:(i + 1) * LANES] = y.astype(BF16)

    bd_lo = bdlo_ref[...]
    for c in range(4):
        acc = _dot(xn, w_ref[:, 1280 + 512 * c:1280 + 512 * (c + 1)])
        for i in range(4):
            y = tile(acc, i)
            if c < 2:
                y = _seg_norm(y, hg_ref[2:3, :], bd_lo) * 0.125 + hg_ref[4:5, :]
            else:
                y = _seg_norm(y, hg_ref[3:4, :], bd_lo)
            fox_ref[:, 512 * c + i * LANES:512 * c + (i + 1) * LANES] = y.astype(BF16)
    fox_ref[:, 2048:2560] = _dot(xn, w_ref[:, 3328:3840]).astype(BF16)

    gf_ref[...] = _dot(xn, w_ref[:, 3840:3968])


def _proj_a(x2d, norm_g, w_perm, head_gains, cos, sin, seq):
    t = x2d.shape[0]
    tm = ROW_TILE
    n_pos = seq // tm
    bd = jnp.asarray(_block_diag_mean(LANES, HEAD_DIM), BF16)
    lo = np.zeros((LANES, LANES), np.float32)
    lo[:HEAD_DIM, :] = 1.0 / HEAD_DIM
    perm = jnp.asarray(_rope_perm(), BF16)
    full = lambda i: (0, 0)
    return pl.pallas_call(
        _proj_a_kernel,
        grid=(t // tm,),
        in_specs=[
            pl.BlockSpec((tm, D_MODEL), lambda i: (i, 0)),
            pl.BlockSpec((1, D_MODEL), full),
            pl.BlockSpec((D_MODEL, PROJ_A_COLS), full),
            pl.BlockSpec((8, LANES), full),
            pl.BlockSpec((tm, LANES), lambda i: (i % n_pos, 0)),
            pl.BlockSpec((tm, LANES), lambda i: (i % n_pos, 0)),
            pl.BlockSpec((LANES, LANES), full),
            pl.BlockSpec((LANES, LANES), full),
            pl.BlockSpec((LANES, LANES), full),
        ],
        out_specs=[
            pl.BlockSpec((tm, 512), lambda i: (i, 0)),
            pl.BlockSpec((tm, 768), lambda i: (i, 0)),
            pl.BlockSpec((tm, FOX_SLAB), lambda i: (i, 0)),
            pl.BlockSpec((tm, LANES), lambda i: (i, 0)),
        ],
        out_shape=[
            jax.ShapeDtypeStruct((t, 512), BF16),
            jax.ShapeDtypeStruct((t, 768), BF16),
            jax.ShapeDtypeStruct((t, FOX_SLAB), BF16),
            jax.ShapeDtypeStruct((t, LANES), F32),
        ],
        compiler_params=_cp(("parallel",)),
        name="proj_a",
    )(x2d, norm_g, w_perm, head_gains, cos, sin, bd, jnp.asarray(lo, BF16), perm)


CUM_CHUNK = 256


FOX_BIAS_LANE = 64


def _cum_kernel(gf_ref, bias_ref, tri_ref, k_ref, place_ref, out_ref, ka_ref):
    tri = tri_ref[...]
    n_chunk = gf_ref.shape[0] // CUM_CHUNK
    carry = jnp.zeros((1, LANES), F32)
    for c in range(n_chunk):
        rows = slice(c * CUM_CHUNK, (c + 1) * CUM_CHUNK)
        lf = _log_sigmoid(gf_ref[rows, :] + bias_ref[...])
        hi, mid, lo = _split3(lf)
        cs = _dot(tri, hi) + _dot(tri, mid) + _dot(tri, lo) + carry
        out_ref[rows, :] = cs
        carry = cs[CUM_CHUNK - 1:CUM_CHUNK, :]
        c_hi, c_mid, c_lo = _split3(cs)
        placed = _dot(c_hi, place_ref[0]) + _dot(c_mid, place_ref[1]) + _dot(c_lo, place_ref[2])
        ka_ref[rows, :] = (k_ref[rows, :].astype(F32) + placed).astype(BF16)


def _fox_cumlog(gf3, bias_row, fox3):
    b, s, _ = gf3.shape
    tri = jnp.asarray(np.tril(np.ones((CUM_CHUNK, CUM_CHUNK), np.float32)), BF16)
    place = np.zeros((3, LANES, FOX_HEADS * LANES), np.float32)
    for h in range(FOX_HEADS):
        for piece in range(3):
            place[piece, 24 + h, h * LANES + FOX_BIAS_LANE + piece] = 1.0
    return pl.pallas_call(
        _cum_kernel,
        grid=(b,),
        in_specs=[
            pl.BlockSpec((None, s, LANES), lambda i: (i, 0, 0)),
            pl.BlockSpec((1, LANES), lambda i: (0, 0)),
            pl.BlockSpec((CUM_CHUNK, CUM_CHUNK), lambda i: (0, 0)),
            pl.BlockSpec((None, s, FOX_HEADS * LANES), lambda i: (i, 0, 1)),
            pl.BlockSpec((3, LANES, FOX_HEADS * LANES), lambda i: (0, 0, 0)),
        ],
        out_specs=[
            pl.BlockSpec((None, s, LANES), lambda i: (i, 0, 0)),
            pl.BlockSpec((None, s, FOX_HEADS * LANES), lambda i: (i, 0, 0)),
        ],
        out_shape=[
            jax.ShapeDtypeStruct((b, s, LANES), F32),
            jax.ShapeDtypeStruct((b, s, FOX_HEADS * LANES), BF16),
        ],
        compiler_params=_cp(("parallel",)),
        name="fox_cumlog",
    )(gf3, bias_row, tri, fox3, jnp.asarray(place, BF16))


def _gelu_tanh(x):
    return 0.5 * x * (1.0 + jnp.tanh(0.7978845608028654 * (x + 0.044715 * x * x * x)))


def _cmp_kernel(kx_ref, vx_ref, pek_ref, pev_ref, w1k_ref, w2k_ref, w1v_ref, w2v_ref,
                kg_ref, cos_ref, sin_ref, bd_ref, perm_ref, kc_ref, vc_ref):
    ncp = kx_ref.shape[0]

    def compress(x_ref, pe_ref, w1_ref, w2_ref):
        x = x_ref[...].astype(F32)
        first = _dot((x + pe_ref[0:1, :]).astype(BF16), w1_ref[0])
        second = _dot((x + pe_ref[1:2, :]).astype(BF16), w1_ref[1])
        pre = first + pltpu.roll(second, ncp - 1, 0)
        return _dot(_gelu_tanh(pre).astype(BF16), w2_ref[...])

    k = compress(kx_ref, pek_ref, w1k_ref, w2k_ref)
    k = _rope(_seg_norm(k, kg_ref[...], bd_ref[...]), cos_ref[...], sin_ref[...], perm_ref[...])
    kc_ref[...] = k.astype(BF16)
    vc_ref[...] = compress(vx_ref, pev_ref, w1v_ref, w2v_ref).astype(BF16)


def _nsa_compress(kx, vx, pek, pev, w1k, w2k, w1v, w2v, kgain, ccos, csin):
    b, ncp, width = kx.shape
    bd = jnp.asarray(_block_diag_mean(LANES, HEAD_DIM), BF16)
    perm = jnp.asarray(_rope_perm(), BF16)
    c2 = lambda i: (0, 0)
    c3 = lambda i: (0, 0, 0)
    return pl.pallas_call(
        _cmp_kernel,
        grid=(b,),
        in_specs=[
            pl.BlockSpec((None, ncp, width), lambda i: (i, 0, 0)),
            pl.BlockSpec((None, ncp, width), lambda i: (i, 0, 0)),
            pl.BlockSpec((2, width), c2),
            pl.BlockSpec((2, width), c2),
            pl.BlockSpec((2, width, LANES), c3),
            pl.BlockSpec((LANES, LANES), c2),
            pl.BlockSpec((2, width, LANES), c3),
            pl.BlockSpec((LANES, LANES), c2),
            pl.BlockSpec((1, LANES), c2),
            pl.BlockSpec((ncp, LANES), c2),
            pl.BlockSpec((ncp, LANES), c2),
            pl.BlockSpec((LANES, LANES), c2),
            pl.BlockSpec((LANES, LANES), c2),
        ],
        out_specs=[
            pl.BlockSpec((None, ncp, LANES), lambda i: (i, 0, 0)),
            pl.BlockSpec((None, ncp, LANES), lambda i: (i, 0, 0)),
        ],
        out_shape=[
            jax.ShapeDtypeStruct((b, ncp, LANES), BF16),
            jax.ShapeDtypeStruct((b, ncp, LANES), BF16),
        ],
        compiler_params=_cp(("parallel",)),
        name="nsa_compress",
    )(kx, vx, pek, pev, w1k, w2k, w1v, w2v, kgain, ccos, csin, bd, perm)


def _nsa_cmp_kernel(q_ref, kc_ref, vc_ref, ovt_ref, eye_ref, oc_ref, sb_ref, *, n_cmp, n_slc):
    tq = ATT_Q
    q0 = pl.program_id(1) * tq
    kc = kc_ref[...]
    vc = vc_ref[...]
    ncp = kc.shape[0]
    ovt = ovt_ref[...]
    lo_half = lax.broadcasted_iota(jnp.int32, (tq, LANES), 1) < HEAD_DIM

    n_lane = lax.broadcasted_iota(jnp.int32, (tq, ncp), 1)
    t_sub = q0 + lax.broadcasted_iota(jnp.int32, (tq, ncp), 0)
    mask = (n_lane * CMP_STRIDE + (CMP_LEN - 1) <= t_sub) & (n_lane < n_cmp)
    n_sub = lax.broadcasted_iota(jnp.int32, (ncp, tq), 0)
    t_lane = q0 + lax.broadcasted_iota(jnp.int32, (ncp, tq), 1)
    mask_t = (n_sub * CMP_STRIDE + (CMP_LEN - 1) <= t_lane) & (n_sub < n_cmp)

    imp_t = [jnp.zeros((64, tq), F32), jnp.zeros((64, tq), F32)]
    for t in range(4):
        qt = q_ref[:, t * LANES:(t + 1) * LANES]
        outs = []
        for g in range(2):
            qm = jnp.where(lo_half if g == 0 else jnp.logical_not(lo_half), qt, jnp.zeros_like(qt))
            s = jnp.where(mask, _dot_nt(qm, kc), NEG)
            p = jnp.where(mask, jnp.exp(s - jnp.max(s, axis=-1, keepdims=True)), 0.0)
            p = p / jnp.maximum(jnp.sum(p, axis=-1, keepdims=True), 1e-30)
            outs.append(_dot(p.astype(BF16), vc))
            st = jnp.where(mask_t, _dot_nt(kc, qm), NEG)
            pt = jnp.where(mask_t, jnp.exp(st - jnp.max(st, axis=0, keepdims=True)), 0.0)
            pt = pt / jnp.maximum(jnp.sum(pt, axis=0, keepdims=True), 1e-30)
            hi, lo = _split2(pt)
            imp_t[g] = imp_t[g] + _dot(ovt, hi) + _dot(ovt, lo)
        oc_ref[:, t * LANES:(t + 1) * LANES] = jnp.where(lo_half, outs[0], outs[1]).astype(BF16)

    blk = lax.broadcasted_iota(jnp.int32, (64, tq), 0)
    cur = (q0 + lax.broadcasted_iota(jnp.int32, (64, tq), 1)) >> 6
    forced = (blk == 0) | (blk == cur) | (blk == cur - 1)
    future = blk > cur
    exists = blk < n_slc
    biases = []
    for g in range(2):
        v = jnp.where(forced, FORCE_SCORE, jnp.where(future, -FORCE_SCORE, imp_t[g]))
        v = jnp.where(exists, v, -3e38)
        sel = jnp.zeros((64, tq), jnp.int32)
        for _ in range(min(SLC_TOPK, n_slc)):
            m = jnp.max(v, axis=0, keepdims=True)
            first = jnp.min(jnp.where(v == m, blk, 64), axis=0, keepdims=True)
            hit = blk == first
            sel = jnp.where(hit, 1, sel)
            v = jnp.where(hit, -3.2e38, v)
        keep = (sel == 1) & jnp.logical_not(future) & exists
        biases.append(jnp.where(keep, 0.0, NEG).astype(BF16))
    sb_t = jnp.concatenate([biases[1], biases[0]], axis=0)
    sb_ref[...] = _dot_nt(eye_ref[...], sb_t).astype(BF16)


def _nsa_cmp_select(qa3, kcmp, vcmp, ovt, n_cmp, n_slc):
    b, s, _ = qa3.shape
    ncp = kcmp.shape[1]
    eye = jnp.asarray(np.eye(ATT_Q, dtype=np.float32), BF16)
    kern = functools.partial(_nsa_cmp_kernel, n_cmp=n_cmp, n_slc=n_slc)
    return pl.pallas_call(
        kern,
        grid=(b, s // ATT_Q),
        in_specs=[
            pl.BlockSpec((None, ATT_Q, 512), lambda i, j: (i, j, 0)),
            pl.BlockSpec((None, ncp, LANES), lambda i, j: (i, 0, 0)),
            pl.BlockSpec((None, ncp, LANES), lambda i, j: (i, 0, 0)),
            pl.BlockSpec((64, ncp), lambda i, j: (0, 0)),
            pl.BlockSpec((ATT_Q, ATT_Q), lambda i, j: (0, 0)),
        ],
        out_specs=[
            pl.BlockSpec((None, ATT_Q, 512), lambda i, j: (i, j, 0)),
            pl.BlockSpec((None, ATT_Q, LANES), lambda i, j: (i, j, 0)),
        ],
        out_shape=[
            jax.ShapeDtypeStruct((b, s, 512), BF16),
            jax.ShapeDtypeStruct((b, s, LANES), BF16),
        ],
        compiler_params=_cp(("parallel", "parallel")),
        name="nsa_cmp_select",
    )(qa3, kcmp, vcmp, ovt, eye)


def _flash_update(carry, s, v, p_ref, *, key_bias=None, row_bias=None, mask_fn=None, strip=None):
    m, l, acc = carry
    rows = s.shape[0]
    strip = rows if strip is None else min(strip, rows)
    ms, ls, alphas = [], [], []
    for r in range(rows // strip):
        rs = slice(r * strip, (r + 1) * strip)
        t = s[rs]
        if key_bias is not None:
            t = t - key_bias
        if mask_fn is not None:
            t = mask_fn(t, r * strip)
        m_old = m[rs]
        row_max = jnp.max(t, axis=-1, keepdims=True)
        if row_bias is not None:
            m_new = jnp.maximum(m_old, row_max + row_bias[rs])
            shift = m_new - row_bias[rs]
        else:
            m_new = jnp.maximum(m_old, row_max)
            shift = m_new
        p = jnp.exp(t - shift)
        alpha = jnp.exp(m_old - m_new)
        ls.append(alpha * l[rs] + jnp.sum(p, axis=-1, keepdims=True))
        p_ref[rs, :] = p.astype(BF16)
        ms.append(m_new)
        alphas.append(alpha)
    alpha = jnp.concatenate(alphas, axis=0)
    acc = alpha * acc + _dot(p_ref[0:rows, :], v)
    return jnp.concatenate(ms, axis=0), jnp.concatenate(ls, axis=0), acc


def _flash_init(rows):
    return (jnp.full((rows, 1), NEG, F32), jnp.zeros((rows, 1), F32), jnp.zeros((rows, LANES), F32))


V_ROWS = 80


def _flash_step_t(carry, s_t, v_t, *, query_bias=None, mask=None):
    m, acc = carry
    if mask is not None:
        s_t = jnp.where(mask, s_t, NEG)
    col_max = jnp.max(s_t, axis=0, keepdims=True)
    if query_bias is not None:
        m_new = jnp.maximum(m, col_max + query_bias)
        shift = m_new - query_bias
    else:
        m_new = jnp.maximum(m, col_max)
        shift = m_new
    p = jnp.exp(s_t - shift).astype(BF16)
    acc = jnp.exp(m - m_new) * acc + _dot(v_t, p)
    return m_new, acc


def _flash_init_t(queries):
    return (jnp.full((1, queries), NEG, F32), jnp.zeros((V_ROWS, queries), F32))


def _values_t(v):
    b, s, w = v.shape
    heads = w // HEAD_DIM
    vt = v.reshape(b, s, heads, HEAD_DIM).transpose(0, 2, 3, 1)
    ones = jnp.ones((b, heads, 1, s), v.dtype)
    pad = jnp.zeros((b, heads, V_ROWS - HEAD_DIM - 1, s), v.dtype)
    return jnp.concatenate([vt, ones, pad], axis=2)


def _nsa_slc_kernel(q_ref, sb_ref, k0_ref, k1_ref, v_ref, o_ref, p0_ref, p1_ref):
    tq, tk = ATT_Q, SLC_K
    q0 = pl.program_id(1) * tq
    sb = sb_ref[...]
    lo_half = lax.broadcasted_iota(jnp.int32, (tq, LANES), 1) < HEAD_DIM
    rows = 4 * tq
    n_before = q0 // tk
    k_refs = (k0_ref, k1_ref)
    p_refs = (p0_ref, p1_ref)

    qq = []
    for g in range(2):
        parts = []
        for t in range(4):
            qt = q_ref[:, t * LANES:(t + 1) * LANES]
            parts.append(jnp.where(lo_half, qt, sb) if g == 0 else jnp.where(lo_half, sb, qt))
        qq.append(jnp.concatenate(parts, axis=0))

    def step(j, carry, mask_fn):
        off = pl.multiple_of(j * tk, tk)
        v = v_ref[pl.ds(off, tk), :]
        out = []
        for g in range(2):
            s = _dot_nt(qq[g], k_refs[g][pl.ds(off, tk), :])
            out.append(_flash_update(carry[g], s, v, p_refs[g], mask_fn=mask_fn))
        return tuple(out)

    carry = lax.fori_loop(0, n_before, lambda j, c: step(j, c, None), (_flash_init(rows), _flash_init(rows)))

    q_in_chunk = q0 - n_before * tk
    def causal(t, row0):
        query = (row0 + lax.broadcasted_iota(jnp.int32, t.shape, 0)) & (tq - 1)
        return jnp.where(lax.broadcasted_iota(jnp.int32, t.shape, 1) - query <= q_in_chunk, t, NEG)

    carry = step(n_before, carry, causal)
    outs = [acc / l for (_, l, acc) in carry]
    for t in range(4):
        rs = slice(t * tq, (t + 1) * tq)
        o_ref[:, t * LANES:(t + 1) * LANES] = jnp.where(lo_half, outs[0][rs], outs[1][rs]).astype(BF16)


def _nsa_selected(qa3, sb, k_aug0, k_aug1, v2):
    b, s, _ = qa3.shape
    blk = lambda i, j: (i, j, 0)
    whole = lambda i, j: (i, 0, 0)
    return pl.pallas_call(
        _nsa_slc_kernel,
        grid=(b, s // ATT_Q),
        in_specs=[
            pl.BlockSpec((None, ATT_Q, 512), blk),
            pl.BlockSpec((None, ATT_Q, LANES), blk),
            pl.BlockSpec((None, s, LANES), whole),
            pl.BlockSpec((None, s, LANES), whole),
            pl.BlockSpec((None, s, LANES), whole),
        ],
        out_specs=pl.BlockSpec((None, ATT_Q, 512), blk),
        out_shape=jax.ShapeDtypeStruct((b, s, 512), BF16),
        scratch_shapes=[pltpu.VMEM((4 * ATT_Q, SLC_K), BF16), pltpu.VMEM((4 * ATT_Q, SLC_K), BF16)],
        compiler_params=_cp(("parallel", "parallel")),
        name="nsa_selected",
    )(qa3, sb, k_aug0, k_aug1, v2)


def _nsa_slc_t_kernel(q_ref, sb_ref, k0_ref, k1_ref, v_ref, eye_ref, o_ref):
    tq, tk = ATT_Q, SLC_K
    q0 = pl.program_id(1) * tq
    sb = sb_ref[...]
    lo_half = lax.broadcasted_iota(jnp.int32, (tq, LANES), 1) < HEAD_DIM
    cols = 4 * tq
    n_before = q0 // tk
    k_refs = (k0_ref, k1_ref)

    qq = []
    for g in range(2):
        parts = []
        for t in range(4):
            qt = q_ref[:, t * LANES:(t + 1) * LANES]
            parts.append(jnp.where(lo_half, qt, sb) if g == 0 else jnp.where(lo_half, sb, qt))
        qq.append(jnp.concatenate(parts, axis=0))

    def step(j, carry, mask=None):
        off = pl.multiple_of(j * tk, tk)
        scores = [_dot_nt(k_refs[g][pl.ds(off, tk), :], qq[g]) for g in range(2)]
        return tuple(_flash_step_t(carry[g], scores[g], v_ref[g, :, pl.ds(off, tk)], mask=mask)
                     for g in range(2))

    carry = lax.fori_loop(0, n_before, step, (_flash_init_t(cols), _flash_init_t(cols)))

    q_in_chunk = q0 - n_before * tk
    query = lax.broadcasted_iota(jnp.int32, (tk, cols), 1) & (tq - 1)
    causal = lax.broadcasted_iota(jnp.int32, (tk, cols), 0) - query <= q_in_chunk
    carry = step(n_before, carry, mask=causal)

    outs = [(acc[0:HEAD_DIM] / acc[HEAD_DIM:HEAD_DIM + 1]).astype(BF16) for (_, acc) in carry]
    eye = eye_ref[...]
    for t in range(4):
        cs = slice(t * tq, (t + 1) * tq)
        both = jnp.concatenate([outs[0][:, cs], outs[1][:, cs]], axis=0)
        o_ref[:, t * LANES:(t + 1) * LANES] = _dot_nt(eye, both).astype(BF16)


def _nsa_selected_t(qa3, sb, k_aug0, k_aug1, v_t):
    b, s, _ = qa3.shape
    blk = lambda i, j: (i, j, 0)
    whole = lambda i, j: (i, 0, 0)
    eye = jnp.asarray(np.eye(ATT_Q, dtype=np.float32), BF16)
    return pl.pallas_call(
        _nsa_slc_t_kernel,
        grid=(b, s // ATT_Q),
        in_specs=[
            pl.BlockSpec((None, ATT_Q, 512), blk),
            pl.BlockSpec((None, ATT_Q, LANES), blk),
            pl.BlockSpec((None, s, LANES), whole),
            pl.BlockSpec((None, s, LANES), whole),
            pl.BlockSpec((None, 2, V_ROWS, s), lambda i, j: (i, 0, 0, 0)),
            pl.BlockSpec((ATT_Q, ATT_Q), lambda i, j: (0, 0)),
        ],
        out_specs=pl.BlockSpec((None, ATT_Q, 512), blk),
        out_shape=jax.ShapeDtypeStruct((b, s, 512), BF16),
        compiler_params=_cp(("parallel", "parallel")),
        name="nsa_selected",
    )(qa3, sb, k_aug0, k_aug1, v_t, eye)


WIN_STRIP = 32


def _nsa_win_kernel(q_ref, k_ref, v_ref, o_ref, p0_ref, p1_ref):
    tq = ATT_Q
    span = WINDOW + tq
    q0 = pl.program_id(1) * tq
    start = pl.multiple_of(jnp.maximum(q0 - WINDOW, 0), tq)
    k = k_ref[pl.ds(start, span), :]
    v = v_ref[pl.ds(start, span), :]
    rows = 4 * tq
    lo_half = lax.broadcasted_iota(jnp.int32, (tq, LANES), 1) < HEAD_DIM
    shape = (WIN_STRIP, span)
    behind = (q0 - start) + lax.broadcasted_iota(jnp.int32, shape, 0) - lax.broadcasted_iota(jnp.int32, shape, 1)
    outs = []
    for g, p_ref in enumerate((p0_ref, p1_ref)):
        keep = lo_half if g == 0 else jnp.logical_not(lo_half)
        parts = []
        for t in range(4):
            qt = q_ref[:, t * LANES:(t + 1) * LANES]
            parts.append(jnp.where(keep, qt, jnp.zeros_like(qt)))
        s = _dot_nt(jnp.concatenate(parts, axis=0), k)
        ls = []
        for r in range(rows // WIN_STRIP):
            rs = slice(r * WIN_STRIP, (r + 1) * WIN_STRIP)
            dist = behind + (r * WIN_STRIP) % tq
            t_s = jnp.where((dist >= 0) & (dist < WINDOW), s[rs], NEG)
            p = jnp.exp(t_s - jnp.max(t_s, axis=-1, keepdims=True))
            ls.append(jnp.sum(p, axis=-1, keepdims=True))
            p_ref[rs, :] = p.astype(BF16)
        outs.append(_dot(p_ref[...], v) / jnp.concatenate(ls, axis=0))
    for t in range(4):
        rs = slice(t * tq, (t + 1) * tq)
        o_ref[:, t * LANES:(t + 1) * LANES] = jnp.where(lo_half, outs[0][rs], outs[1][rs]).astype(BF16)


def _nsa_window(qa3, k2, v2):
    b, s, _ = qa3.shape
    blk = lambda i, j: (i, j, 0)
    whole = lambda i, j: (i, 0, 0)
    return pl.pallas_call(
        _nsa_win_kernel,
        grid=(b, s // ATT_Q),
        in_specs=[
            pl.BlockSpec((None, ATT_Q, 512), blk),
            pl.BlockSpec((None, s, LANES), whole),
            pl.BlockSpec((None, s, LANES), whole),
        ],
        out_specs=pl.BlockSpec((None, ATT_Q, 512), blk),
        out_shape=jax.ShapeDtypeStruct((b, s, 512), BF16),
        scratch_shapes=[pltpu.VMEM((4 * ATT_Q, WINDOW + ATT_Q), BF16),
                        pltpu.VMEM((4 * ATT_Q, WINDOW + ATT_Q), BF16)],
        compiler_params=_cp(("parallel", "parallel")),
        name="nsa_window",
    )(qa3, k2, v2)


def _fox_kernel(q_ref, k_ref, v_ref, cc_ref, cr_ref, o_ref, p0_ref, p1_ref):
    tq, tk = FOX_Q, FOX_K
    q0 = pl.program_id(1) * tq
    n_before = q0 // tk
    n_diag = tq // tk
    lo_half = lax.broadcasted_iota(jnp.int32, (tq, LANES), 1) < HEAD_DIM
    p_refs = (p0_ref, p1_ref)
    for pair in range(FOX_HEADS // 2):
        cols = slice(pair * LANES, (pair + 1) * LANES)
        qt = q_ref[:, cols]
        qm = (jnp.where(lo_half, qt, jnp.zeros_like(qt)), jnp.where(lo_half, jnp.zeros_like(qt), qt))
        cq = tuple(cc_ref[:, 2 * pair + hh:2 * pair + hh + 1] for hh in range(2))

        def step(j, carry, row_lo=0, diag=None):
            off = pl.multiple_of(j * tk, tk)
            k = k_ref[pl.ds(off, tk), cols]
            v = v_ref[pl.ds(off, tk), cols]
            mask_fn = None
            if diag is not None:
                def mask_fn(t, row0):
                    first_row = row_lo + row0 - diag * tk
                    if first_row >= tk - 1:
                        return t
                    ahead = (lax.broadcasted_iota(jnp.int32, t.shape, 1)
                             - lax.broadcasted_iota(jnp.int32, t.shape, 0))
                    return jnp.where(ahead <= first_row, t, NEG)
            out = []
            for hh in range(2):
                m, l, acc = carry[hh]
                s = _dot_nt(qm[hh][row_lo:], k)
                ck = cr_ref[2 * pair + hh:2 * pair + hh + 1, pl.ds(off, tk)]
                new = _flash_update((m[row_lo:], l[row_lo:], acc[row_lo:]), s, v, p_refs[hh],
                                    key_bias=ck, row_bias=cq[hh][row_lo:], mask_fn=mask_fn)
                if row_lo:
                    new = tuple(jnp.concatenate([old[:row_lo], part], axis=0)
                                for old, part in zip((m, l, acc), new))
                out.append(new)
            return tuple(out)

        carry = lax.fori_loop(0, n_before, step, (_flash_init(tq), _flash_init(tq)))
        for d in range(n_diag):
            carry = step(n_before + d, carry, row_lo=d * tk, diag=d)
        outs = [acc / l for (_, l, acc) in carry]
        o_ref[:, cols] = jnp.where(lo_half, outs[0], outs[1]).astype(BF16)


def _fox_attention(fox3, cum_col, cum_row):
    b, s, _ = fox3.shape
    return pl.pallas_call(
        _fox_kernel,
        grid=(b, s // FOX_Q),
        in_specs=[
            pl.BlockSpec((None, FOX_Q, 512), lambda i, j: (i, j, 0)),
            pl.BlockSpec((None, s, 512), lambda i, j: (i, 0, 1)),
            pl.BlockSpec((None, s, 512), lambda i, j: (i, 0, 2)),
            pl.BlockSpec((None, FOX_Q, FOX_HEADS), lambda i, j: (i, j, 0)),
            pl.BlockSpec((None, FOX_HEADS, s), lambda i, j: (i, 0, 0)),
        ],
        out_specs=pl.BlockSpec((None, FOX_Q, 512), lambda i, j: (i, j, 0)),
        out_shape=jax.ShapeDtypeStruct((b, s, 512), BF16),
        scratch_shapes=[pltpu.VMEM((FOX_Q, FOX_K), BF16), pltpu.VMEM((FOX_Q, FOX_K), BF16)],
        compiler_params=_cp(("parallel", "parallel")),
        name="fox_attention",
    )(fox3, fox3, fox3, cum_col, cum_row)


def _fox_t_kernel(q_ref, k_ref, v_ref, cr_ref, o_ref):
    tq, tk = FOX_Q, FOX_K
    q0 = pl.program_id(1) * tq
    n_before = q0 // tk
    n_diag = tq // tk
    for pair in range(FOX_HEADS // 2):
        heads = (2 * pair, 2 * pair + 1)
        qs = tuple(q_ref[:, h * LANES:(h + 1) * LANES] for h in heads)
        cq = tuple(cr_ref[h:h + 1, pl.ds(pl.multiple_of(q0, tq), tq)] for h in heads)

        def step(j, carry, q_lo=0, diag=None):
            off = pl.multiple_of(j * tk, tk)
            mask = None
            if diag is not None:
                shape = (tk, tq - q_lo)
                mask = (lax.broadcasted_iota(jnp.int32, shape, 0) + (diag * tk - q_lo)
                        <= lax.broadcasted_iota(jnp.int32, shape, 1))
            scores = [_dot_nt(k_ref[pl.ds(off, tk), h * LANES:(h + 1) * LANES], qs[i][q_lo:])
                      for i, h in enumerate(heads)]
            out = []
            for i, h in enumerate(heads):
                m, acc = carry[i]
                new = _flash_step_t((m[:, q_lo:], acc[:, q_lo:]), scores[i], v_ref[h, :, pl.ds(off, tk)],
                                    query_bias=cq[i][:, q_lo:], mask=mask)
                if q_lo:
                    new = tuple(jnp.concatenate([old[:, :q_lo], part], axis=1)
                                for old, part in zip((m, acc), new))
                out.append(new)
            return tuple(out)

        carry = lax.fori_loop(0, n_before, step, (_flash_init_t(tq), _flash_init_t(tq)))
        for d in range(n_diag):
            carry = step(n_before + d, carry, q_lo=d * tk, diag=d)
        for i, h in enumerate(heads):
            acc = carry[i][1]
            o_ref[h * HEAD_DIM:(h + 1) * HEAD_DIM, :] = (
                acc[0:HEAD_DIM] / acc[HEAD_DIM:HEAD_DIM + 1]).astype(BF16)


def _fox_attention_t(fox3, k_aug, v_t, cum_row):
    b, s, _ = fox3.shape
    width = FOX_HEADS * LANES
    return pl.pallas_call(
        _fox_t_kernel,
        grid=(b, s // FOX_Q),
        in_specs=[
            pl.BlockSpec((None, FOX_Q, width), lambda i, j: (i, j, 0)),
            pl.BlockSpec((None, s, width), lambda i, j: (i, 0, 0)),
            pl.BlockSpec((None, FOX_HEADS, V_ROWS, s), lambda i, j: (i, 0, 0, 0)),
            pl.BlockSpec((None, FOX_HEADS, s), lambda i, j: (i, 0, 0)),
        ],
        out_specs=pl.BlockSpec((None, FOX_HEADS * HEAD_DIM, FOX_Q), lambda i, j: (i, 0, j)),
        out_shape=jax.ShapeDtypeStruct((b, FOX_HEADS * HEAD_DIM, s), BF16),
        compiler_params=_cp(("parallel", "parallel")),
        name="fox_attention",
    )(fox3, k_aug, v_t, cum_row)


def _out_a_kernel(oc_ref, os_ref, ow_ref, ob_ref, gf_ref, gb_ref, gx_ref, x_ref, w_ref, o_ref):
    gates = _sigmoid(gf_ref[...] + gb_ref[...])
    hi, lo = _split2(gates)
    o_a = None
    for j, br in enumerate((oc_ref, os_ref, ow_ref)):
        gexp = _dot(hi, gx_ref[j]) + _dot(lo, gx_ref[j])
        term = gexp * br[...].astype(F32)
        o_a = term if o_a is None else o_a + term
    y = _dot(o_a.astype(BF16), w_ref[0:512, :]) + _dot(ob_ref[...], w_ref[512:1024, :])
    o_ref[...] = x_ref[...] + y


def _out_a(oc, os_, ow, ob, gf, gate_bias_row, gate_expand, x2d, w_out_perm):
    t = x2d.shape[0]
    tm = ROW_TILE
    row = lambda i: (i, 0)
    c2 = lambda i: (0, 0)
    return pl.pallas_call(
        _out_a_kernel,
        grid=(t // tm,),
        in_specs=[
            pl.BlockSpec((tm, 512), row),
            pl.BlockSpec((tm, 512), row),
            pl.BlockSpec((tm, 512), row),
            pl.BlockSpec((tm, 512), row),
            pl.BlockSpec((tm, LANES), row),
            pl.BlockSpec((1, LANES), c2),
            pl.BlockSpec((3, LANES, 512), lambda i: (0, 0, 0)),
            pl.BlockSpec((tm, D_MODEL), row),
            pl.BlockSpec((D_MODEL, D_MODEL), c2),
        ],
        out_specs=pl.BlockSpec((tm, D_MODEL), row),
        out_shape=jax.ShapeDtypeStruct((t, D_MODEL), F32),
        compiler_params=_cp(("parallel",)),
        name="out_a",
    )(oc, os_, ow, ob, gf, gate_bias_row, gate_expand, x2d, w_out_perm)


def _swiglu_body(xn, wg_ref, wu_ref, wd_ref, h_ref):
    for c in range(D_FF // FF_CHUNK):
        cols = slice(c * FF_CHUNK, (c + 1) * FF_CHUNK)
        g = _dot(xn, wg_ref[:, cols])
        u = _dot(xn, wu_ref[:, cols])
        h_ref[:, cols] = (g * _sigmoid(g) * u).astype(BF16)
    return _dot(h_ref[...], wd_ref[...])


def _ffn_dense_kernel(x_ref, g_ref, wg_ref, wu_ref, wd_ref, o_ref, h_ref):
    x = x_ref[...]
    xn = _rms_rows(x, g_ref[...]).astype(BF16)
    o_ref[...] = x + _swiglu_body(xn, wg_ref, wu_ref, wd_ref, h_ref)


def _ffn_dense(x2d, norm_g, wg, wu, wd):
    t = x2d.shape[0]
    tm = ROW_TILE
    c2 = lambda i: (0, 0)
    return pl.pallas_call(
        _ffn_dense_kernel,
        grid=(t // tm,),
        in_specs=[
            pl.BlockSpec((tm, D_MODEL), lambda i: (i, 0)),
            pl.BlockSpec((1, D_MODEL), c2),
            pl.BlockSpec((D_MODEL, D_FF), c2),
            pl.BlockSpec((D_MODEL, D_FF), c2),
            pl.BlockSpec((D_FF, D_MODEL), c2),
        ],
        out_specs=pl.BlockSpec((tm, D_MODEL), lambda i: (i, 0)),
        out_shape=jax.ShapeDtypeStruct((t, D_MODEL), F32),
        scratch_shapes=[pltpu.VMEM((tm, D_FF), BF16)],
        compiler_params=_cp(("parallel",)),
        name="ffn_dense",
    )(x2d, norm_g, wg, wu, wd)


def _ffn_expert_kernel(te_ref, nu_ref, x_ref, wg_ref, wu_ref, wd_ref, o_ref, h_ref):
    @pl.when(pl.program_id(0) < nu_ref[0])
    def _():
        o_ref[...] = _swiglu_body(x_ref[...], wg_ref, wu_ref, wd_ref, h_ref).astype(BF16)

    @pl.when(pl.program_id(0) >= nu_ref[0])
    def _():
        o_ref[...] = jnp.zeros_like(o_ref)


def _ffn_experts(tile_expert, n_used, xs, wg, wu, wd):
    n_slot = xs.shape[0]
    tm = EXPERT_TILE
    n_tiles = n_slot // tm
    row = lambda i, te, nu: (jnp.minimum(i, nu[0] - 1), 0)
    wsel = lambda i, te, nu: (te[i], 0, 0)
    grid_spec = pltpu.PrefetchScalarGridSpec(
        num_scalar_prefetch=2,
        grid=(n_tiles,),
        in_specs=[
            pl.BlockSpec((tm, D_MODEL), row),
            pl.BlockSpec((None, D_MODEL, D_FF), wsel),
            pl.BlockSpec((None, D_MODEL, D_FF), wsel),
            pl.BlockSpec((None, D_FF, D_MODEL), wsel),
        ],
        out_specs=pl.BlockSpec((tm, D_MODEL), lambda i, te, nu: (i, 0)),
        scratch_shapes=[pltpu.VMEM((tm, D_FF), BF16)],
    )
    return pl.pallas_call(
        _ffn_expert_kernel,
        grid_spec=grid_spec,
        out_shape=jax.ShapeDtypeStruct((n_slot, D_MODEL), BF16),
        compiler_params=_cp(("arbitrary",)),
        name="ffn_experts",
    )(tile_expert, n_used, xs, wg, wu, wd)


def _proj_c_kernel(x_ref, g_ref, w_ref, wa_ref, ba_ref, qk_ref, v_ref, r_ref, la_ref):
    xn = _rms_rows(x_ref[...], g_ref[...]).astype(BF16)
    for c in range(2):
        qk_ref[:, 512 * c:512 * (c + 1)] = _dot(xn, w_ref[:, 512 * c:512 * (c + 1)]).astype(BF16)
    for c in range(2):
        v_ref[:, 512 * c:512 * (c + 1)] = _dot(xn, w_ref[:, 1024 + 512 * c:1536 + 512 * c]).astype(BF16)
    for c in range(2):
        r_ref[:, 512 * c:512 * (c + 1)] = _dot(xn, w_ref[:, 2048 + 512 * c:2560 + 512 * c]).astype(BF16)
    a1 = _dot(xn, w_ref[:, 3072:3200])
    hi, lo = _split2(a1)
    wa = wa_ref[...]
    pre = _dot(hi, wa) + _dot(lo, wa) + ba_ref[...]
    la_ref[...] = _log_sigmoid(pre) * (1.0 / GLA_TAU)


def _proj_c(x2d, norm_g, w_pad, wa_pad, ba_row):
    t = x2d.shape[0]
    tm = ROW_TILE
    row = lambda i: (i, 0)
    c2 = lambda i: (0, 0)
    return pl.pallas_call(
        _proj_c_kernel,
        grid=(t // tm,),
        in_specs=[
            pl.BlockSpec((tm, D_MODEL), row),
            pl.BlockSpec((1, D_MODEL), c2),
            pl.BlockSpec((D_MODEL, 3200), c2),
            pl.BlockSpec((LANES, GLA_DK), c2),
            pl.BlockSpec((1, GLA_DK), c2),
        ],
        out_specs=[
            pl.BlockSpec((tm, 1024), row),
            pl.BlockSpec((tm, 1024), row),
            pl.BlockSpec((tm, 1024), row),
            pl.BlockSpec((tm, GLA_DK), row),
        ],
        out_shape=[
            jax.ShapeDtypeStruct((t, 1024), BF16),
            jax.ShapeDtypeStruct((t, 1024), BF16),
            jax.ShapeDtypeStruct((t, 1024), BF16),
            jax.ShapeDtypeStruct((t, GLA_DK), F32),
        ],
        compiler_params=_cp(("parallel",)),
        name="proj_c",
    )(x2d, norm_g, w_pad, wa_pad, ba_row)


def _gla_kernel(q_ref, k_ref, v_ref, g_ref, tri_ref, eye_ref, o_ref, st_ref):
    c_len, sub = GLA_CHUNK, GLA_SUB
    n_sub = c_len // sub
    dk = q_ref.shape[1]
    dv = v_ref.shape[1]
    scale = dk ** -0.5
    tri = tri_ref[...]
    eye = eye_ref[...]
    st_ref[...] = jnp.zeros_like(st_ref)
    i_loc = lax.broadcasted_iota(jnp.int32, (c_len, 1), 0) & (sub - 1)

    def chunk(c, _):
        r0 = pl.multiple_of(c * c_len, c_len)
        q = q_ref[pl.ds(r0, c_len), :].astype(F32) * scale
        k = k_ref[pl.ds(r0, c_len), :].astype(F32)
        vb = v_ref[pl.ds(r0, c_len), :]
        v = vb.astype(F32)
        hi, mid, lo = _split3(g_ref[pl.ds(r0, c_len), :])
        b = _dot(tri, hi) + _dot(tri, mid) + _dot(tri, lo)
        b_last = b[c_len - 1:c_len, :]

        st_t = st_ref[...]
        o = _dot_nt((q * jnp.exp(b)).astype(BF16), st_t.astype(BF16))

        for i in range(1, n_sub):
            r = b[i * sub:i * sub + 1, :]
            qi = (q[i * sub:(i + 1) * sub] * jnp.exp(b[i * sub:(i + 1) * sub] - r)).astype(BF16)
            kj = (k[:i * sub] * jnp.exp(r - b[:i * sub])).astype(BF16)
            att = _dot_nt(qi, kj).astype(BF16)
            o_i = _dot(att, vb[:i * sub])
            pad = [jnp.zeros((i * sub, dv), F32), o_i]
            if i + 1 < n_sub:
                pad.append(jnp.zeros((c_len - (i + 1) * sub, dv), F32))
            o = o + jnp.concatenate(pad, axis=0)

        for j in range(sub):
            def rows(x):
                return jnp.concatenate(
                    [jnp.broadcast_to(x[s * sub + j:s * sub + j + 1, :], (sub, x.shape[1]))
                     for s in range(n_sub)], axis=0)
            e = jnp.exp(jnp.minimum(b - rows(b), 0.0))
            a = jnp.sum(q * e * rows(k), axis=-1, keepdims=True)
            a = jnp.where(i_loc >= j, a, 0.0)
            o = o + a * rows(v)

        o_ref[pl.ds(r0, c_len), :] = o.astype(BF16)

        kd = (k * jnp.exp(b_last - b)).astype(BF16)
        v_t = _dot_nt(eye, vb).astype(BF16)
        st_ref[...] = st_t * jnp.exp(b_last) + _dot(v_t, kd)
        return 0

    lax.fori_loop(0, q_ref.shape[0] // c_len, chunk, 0)


def _gla(qk3, v3, la3):
    b, s, _ = qk3.shape
    dk = GLA_DK // GLA_HEADS
    dv = GLA_DV // GLA_HEADS
    tri = jnp.asarray(np.tril(np.ones((GLA_CHUNK, GLA_CHUNK), np.float32)), BF16)
    eye = jnp.asarray(np.eye(dv, dtype=np.float32), BF16)
    return pl.pallas_call(
        _gla_kernel,
        grid=(b, GLA_HEADS),
        in_specs=[
            pl.BlockSpec((None, s, dk), lambda i, h: (i, 0, h)),
            pl.BlockSpec((None, s, dk), lambda i, h: (i, 0, GLA_HEADS + h)),
            pl.BlockSpec((None, s, dv), lambda i, h: (i, 0, h)),
            pl.BlockSpec((None, s, dk), lambda i, h: (i, 0, h)),
            pl.BlockSpec((GLA_CHUNK, GLA_CHUNK), lambda i, h: (0, 0)),
            pl.BlockSpec((dv, dv), lambda i, h: (0, 0)),
        ],
        out_specs=pl.BlockSpec((None, s, dv), lambda i, h: (i, 0, h)),
        out_shape=jax.ShapeDtypeStruct((b, s, GLA_DV), BF16),
        scratch_shapes=[pltpu.VMEM((dv, dk), F32)],
        compiler_params=_cp(("parallel", "parallel")),
        name="gla",
    )(qk3, qk3, v3, la3, tri, eye)


def _out_c_kernel(o_ref, r_ref, x_ref, on_ref, bd_ref, w_ref, mg_ref, rt_ref,
                  x3_ref, xn_ref, e1_ref, e2_ref, w1_ref, w2_ref):
    dv = GLA_DV // GLA_HEADS
    bd = bd_ref[...]
    parts = []
    for h in range(GLA_HEADS):
        cols = slice(h * dv, (h + 1) * dv)
        oh = _seg_norm(o_ref[:, cols].astype(F32), on_ref[...], bd)
        r = r_ref[:, cols].astype(F32)
        parts.append((oh * (r * _sigmoid(r))).astype(BF16))
    x3 = x_ref[...] + _dot(jnp.concatenate(parts, axis=1), w_ref[...])
    x3_ref[...] = x3

    xn = _rms_rows(x3, mg_ref[...])
    xn_ref[...] = xn.astype(BF16)
    xh, xl = _split2(xn)
    logits = _dot(xh, rt_ref[0]) + _dot(xl, rt_ref[0]) + _dot(xh, rt_ref[1])
    lane = lax.broadcasted_iota(jnp.int32, logits.shape, 1)
    logits = jnp.where(lane < N_EXPERTS, logits, -3e38)
    v1 = jnp.max(logits, axis=-1, keepdims=True)
    i1 = jnp.min(jnp.where(logits == v1, lane, LANES), axis=-1, keepdims=True)
    rest = jnp.where(lane == i1, -3e38, logits)
    v2 = jnp.max(rest, axis=-1, keepdims=True)
    i2 = jnp.min(jnp.where(rest == v2, lane, LANES), axis=-1, keepdims=True)
    ex = jnp.exp(v2 - v1)
    den = 1.0 + ex
    e1_ref[...] = jnp.broadcast_to(i1, logits.shape)
    e2_ref[...] = jnp.broadcast_to(i2, logits.shape)
    w1_ref[...] = jnp.broadcast_to(1.0 / den, logits.shape)
    w2_ref[...] = jnp.broadcast_to(ex / den, logits.shape)


def _out_c(o2d, r2d, x2d, onorm_row, w_out, mnorm_row, router2):
    t = x2d.shape[0]
    tm = ROW_TILE
    dv = GLA_DV // GLA_HEADS
    bd = jnp.asarray(_block_diag_mean(dv, dv), BF16)
    row = lambda i: (i, 0)
    c2 = lambda i: (0, 0)
    slab = pl.BlockSpec((tm, LANES), row)
    return pl.pallas_call(
        _out_c_kernel,
        grid=(t // tm,),
        in_specs=[
            pl.BlockSpec((tm, GLA_DV), row),
            pl.BlockSpec((tm, GLA_DV), row),
            pl.BlockSpec((tm, D_MODEL), row),
            pl.BlockSpec((1, dv), c2),
            pl.BlockSpec((dv, dv), c2),
            pl.BlockSpec((D_MODEL, D_MODEL), c2),
            pl.BlockSpec((1, D_MODEL), c2),
            pl.BlockSpec((2, D_MODEL, LANES), lambda i: (0, 0, 0)),
        ],
        out_specs=[
            pl.BlockSpec((tm, D_MODEL), row),
            pl.BlockSpec((tm, D_MODEL), row),
            slab, slab, slab, slab,
        ],
        out_shape=[
            jax.ShapeDtypeStruct((t, D_MODEL), F32),
            jax.ShapeDtypeStruct((t, D_MODEL), BF16),
            jax.ShapeDtypeStruct((t, LANES), jnp.int32),
            jax.ShapeDtypeStruct((t, LANES), jnp.int32),
            jax.ShapeDtypeStruct((t, LANES), F32),
            jax.ShapeDtypeStruct((t, LANES), F32),
        ],
        compiler_params=_cp(("parallel",)),
        name="out_c_router",
    )(o2d, r2d, x2d, onorm_row, bd, w_out, mnorm_row, router2)


def _row_copy(src_ref, src_i, dst_ref, dst_i, sem):
    return pltpu.make_async_copy(src_ref.at[src_i], dst_ref.at[dst_i], sem)


def _dispatch_kernel(dest_ref, x_ref, init_ref, xs_ref, sem):
    del init_ref
    n = x_ref.shape[0]

    def issue(r, _):
        for k in range(TOP_K):
            _row_copy(x_ref, r, xs_ref, dest_ref[0, 0, TOP_K * r + k], sem).start()
        return 0

    lax.fori_loop(0, n, issue, 0)

    def drain(r, _):
        for k in range(TOP_K):
            _row_copy(x_ref, r, xs_ref, 0, sem).wait()
        return 0

    lax.fori_loop(0, n, drain, 0)


def _moe_dispatch(dest3, xn3, n_slot):
    t = xn3.shape[0]
    tm = MOVE_TILE
    zeros = jnp.zeros((n_slot, 8, LANES), BF16)
    return pl.pallas_call(
        _dispatch_kernel,
        grid=(t // tm,),
        in_specs=[
            pl.BlockSpec((1, 1, TOP_K * tm), lambda i: (i, 0, 0), memory_space=pltpu.SMEM),
            pl.BlockSpec((tm, 8, LANES), lambda i: (i, 0, 0)),
            pl.BlockSpec(memory_space=pl.ANY),
        ],
        out_specs=pl.BlockSpec(memory_space=pl.ANY),
        out_shape=jax.ShapeDtypeStruct((n_slot, 8, LANES), BF16),
        scratch_shapes=[pltpu.SemaphoreType.DMA(())],
        input_output_aliases={2: 0},
        compiler_params=_cp(("arbitrary",), has_side_effects=True),
        name="moe_dispatch",
    )(dest3, xn3, zeros)


def _combine_kernel(dest_ref, ys_ref, x_ref, w1_ref, w2_ref, o_ref, ya_ref, yb_ref, sem_a, sem_b):
    n = x_ref.shape[0]

    def issue(r, _):
        _row_copy(ys_ref, dest_ref[0, 0, TOP_K * r], ya_ref, r, sem_a).start()
        _row_copy(ys_ref, dest_ref[0, 0, TOP_K * r + 1], yb_ref, r, sem_b).start()
        return 0

    lax.fori_loop(0, n, issue, 0)

    def drain(r, _):
        _row_copy(ys_ref, 0, ya_ref, r, sem_a).wait()
        _row_copy(ys_ref, 0, yb_ref, r, sem_b).wait()
        return 0

    lax.fori_loop(0, n, drain, 0)
    o_ref[...] = (x_ref[...] + w1_ref[...] * ya_ref[...].astype(F32)
                  + w2_ref[...] * yb_ref[...].astype(F32))


def _moe_combine(dest3, ys3, x3d, w1b, w2b):
    t = x3d.shape[0]
    tm = MOVE_TILE
    tok = lambda i: (i, 0, 0)
    return pl.pallas_call(
        _combine_kernel,
        grid=(t // tm,),
        in_specs=[
            pl.BlockSpec((1, 1, TOP_K * tm), tok, memory_space=pltpu.SMEM),
            pl.BlockSpec(memory_space=pl.ANY),
            pl.BlockSpec((tm, 8, LANES), tok),
            pl.BlockSpec((tm, 1, LANES), tok),
            pl.BlockSpec((tm, 1, LANES), tok),
        ],
        out_specs=pl.BlockSpec((tm, 8, LANES), tok),
        out_shape=jax.ShapeDtypeStruct((t, 8, LANES), F32),
        scratch_shapes=[
            pltpu.VMEM((tm, 8, LANES), BF16),
            pltpu.VMEM((tm, 8, LANES), BF16),
            pltpu.SemaphoreType.DMA(()),
            pltpu.SemaphoreType.DMA(()),
        ],
        compiler_params=_cp(("arbitrary",)),
        name="moe_combine",
    )(dest3, ys3, x3d, w1b, w2b)


def _nsa_head_perm():
    order = []
    for t in range(4):
        order += [t, 4 + t]
    cols = np.concatenate([np.arange(h * HEAD_DIM, (h + 1) * HEAD_DIM) for h in order])
    return order, cols


def _layer_even(x, a_norm, a_w_in, a_q_norm, a_k_norm, a_pe_k, a_pe_v, ck1, ck2, cv1, cv2,
                gate_bias, fq_norm, fk_norm, f_bias, a_w_out, f_norm, f_wg, f_wu, f_wd):
    b, s, d = x.shape
    t = b * s
    x2d = x.reshape(t, d)
    order, qcols = _nsa_head_perm()

    def head_tiles(cols0):
        wq = a_w_in[:, cols0:cols0 + 512].reshape(d, FOX_HEADS, HEAD_DIM)
        return jnp.pad(wq, ((0, 0), (0, 0), (0, LANES - HEAD_DIM))).reshape(d, FOX_HEADS * LANES)

    w = jnp.concatenate([
        jnp.take(a_w_in, jnp.asarray(qcols), axis=1),
        a_w_in[:, 512:1280],
        head_tiles(1304),
        head_tiles(1816),
        a_w_in[:, 2328:2840],
        a_w_in[:, 1280:1304],
        a_w_in[:, 2840:2848],
        jnp.zeros((d, LANES - 32), F32),
    ], axis=1).astype(BF16)

    tile2 = lambda g: jnp.concatenate([g, g])
    lo_only = lambda g: jnp.concatenate([g, jnp.zeros((LANES - HEAD_DIM,), F32)])
    q_bias_lanes = jnp.zeros((LANES,), F32).at[FOX_BIAS_LANE:FOX_BIAS_LANE + 3].set(-1.0)
    head_gains = jnp.stack([tile2(a_q_norm), tile2(a_k_norm), lo_only(fq_norm), lo_only(fk_norm), q_bias_lanes]
                           + [jnp.zeros((LANES,), F32)] * 3)
    cos, sin = _rope_tables(np.arange(s))
    qa, kv6, fox, gf = _proj_a(x2d, a_norm.reshape(1, d), w, head_gains, cos, sin, s)

    fox3 = fox.reshape(b, s, FOX_SLAB)
    bias_row = jnp.zeros((1, LANES), F32).at[0, 24:32].set(f_bias)
    cum, k_aug = _fox_cumlog(gf.reshape(b, s, LANES), bias_row, fox3)
    cum_row = cum[:, :, 24:32].transpose(0, 2, 1)
    o_b = _fox_attention_t(fox3, k_aug, _values_t(fox3[:, :, 2048:2560]), cum_row).transpose(0, 2, 1)

    ncp = s // CMP_STRIDE
    n_cmp = (s - CMP_LEN) // CMP_STRIDE + 1
    n_slc = s // SLC_BLOCK
    kv3 = kv6.reshape(b, s, 768)
    kx = kv3[:, :, 0:128].reshape(b, ncp, CMP_STRIDE * LANES)
    vx = kv3[:, :, 128:256].reshape(b, ncp, CMP_STRIDE * LANES)

    def pe_rows(pe):
        p2 = jnp.concatenate([pe, pe], axis=1)
        return p2.reshape(2, CMP_STRIDE * LANES)

    def w1_blocks(w1):
        w4 = w1.reshape(2, CMP_STRIDE, HEAD_DIM, HEAD_DIM)
        z = jnp.zeros_like(w4)
        top = jnp.concatenate([w4, z], axis=-1)
        bot = jnp.concatenate([z, w4], axis=-1)
        return jnp.stack([top, bot], axis=2).reshape(2, CMP_STRIDE * LANES, LANES).astype(BF16)

    def w2_block(w2):
        z = jnp.zeros_like(w2)
        return jnp.concatenate([jnp.concatenate([w2, z], 1), jnp.concatenate([z, w2], 1)], 0).astype(BF16)

    ccos, csin = _rope_tables(np.arange(ncp) * CMP_STRIDE + CMP_LEN - 1)
    kcmp, vcmp = _nsa_compress(kx, vx, pe_rows(a_pe_k), pe_rows(a_pe_v), w1_blocks(ck1), w2_block(ck2),
                               w1_blocks(cv1), w2_block(cv2), tile2(a_k_norm).reshape(1, LANES), ccos, csin)

    c_start = np.arange(ncp) * CMP_STRIDE
    s_start = np.arange(64) * SLC_BLOCK
    ov = np.maximum(np.minimum(c_start[:, None] + CMP_LEN, s_start[None, :] + SLC_BLOCK)
                    - np.maximum(c_start[:, None], s_start[None, :]), 0).astype(np.float32) / CMP_LEN
    ov[n_cmp:, :] = 0.0
    ov[:, n_slc:] = 0.0
    qa3 = qa.reshape(b, s, 512)
    o_c, sb = _nsa_cmp_select(qa3, kcmp, vcmp, jnp.asarray(ov.T, BF16), n_cmp, n_slc)

    onehot = jnp.asarray((np.arange(s)[:, None] // SLC_BLOCK) == np.arange(64)[None, :], BF16)
    onehot = jnp.broadcast_to(onehot[None], (b, s, 64))
    ksl = kv3[:, :, 256:384]
    k_aug0 = jnp.concatenate([ksl[:, :, :64], onehot], axis=-1)
    k_aug1 = jnp.concatenate([onehot, ksl[:, :, 64:]], axis=-1)
    o_s = _nsa_selected_t(qa3, sb, k_aug0, k_aug1, _values_t(kv3[:, :, 384:512]))

    o_w = _nsa_window(qa3, kv3[:, :, 512:640], kv3[:, :, 640:768])

    gexp = np.zeros((3, LANES, 512), np.float32)
    for tile_i in range(4):
        for half, h in enumerate((tile_i, 4 + tile_i)):
            for j in range(3):
                gexp[j, h * 3 + j, tile_i * LANES + half * HEAD_DIM:tile_i * LANES + (half + 1) * HEAD_DIM] = 1.0
    gate_bias_row = jnp.zeros((1, LANES), F32).at[0, 0:24].set(gate_bias)
    w_out = jnp.concatenate([jnp.take(a_w_out[:512], jnp.asarray(qcols), axis=0), a_w_out[512:]], 0).astype(BF16)
    x1 = _out_a(o_c.reshape(t, 512), o_s.reshape(t, 512), o_w.reshape(t, 512), o_b.reshape(t, 512),
                gf, gate_bias_row, jnp.asarray(gexp, BF16), x2d, w_out)

    x2 = _ffn_dense(x1, f_norm.reshape(1, d), f_wg.astype(BF16), f_wu.astype(BF16), f_wd.astype(BF16))
    return x2.reshape(b, s, d)


def _layer_odd(x, c_norm, c_w_in, c_w_a2, c_b_a, c_o_norm, c_w_out, m_norm, m_router, m_wg, m_wu, m_wd):
    b, s, d = x.shape
    t = b * s
    x2d = x.reshape(t, d)
    w = jnp.pad(c_w_in, ((0, 0), (0, 3200 - c_w_in.shape[1]))).astype(BF16)
    wa = jnp.pad(c_w_a2, ((0, LANES - GLA_RANK), (0, 0))).astype(BF16)
    qk, v, r, la = _proj_c(x2d, c_norm.reshape(1, d), w, wa, c_b_a.reshape(1, GLA_DK))
    o = _gla(qk.reshape(b, s, 1024), v.reshape(b, s, 1024), la.reshape(b, s, GLA_DK))

    rt = jnp.pad(m_router, ((0, 0), (0, LANES - N_EXPERTS)))
    rt_hi = rt.astype(BF16)
    rt_lo = (rt - rt_hi.astype(F32)).astype(BF16)
    x3, xn, e1, e2, w1b, w2b = _out_c(o.reshape(t, GLA_DV), r, x2d, c_o_norm.reshape(1, -1),
                                      c_w_out.astype(BF16), m_norm.reshape(1, d), jnp.stack([rt_hi, rt_lo]))

    tm = EXPERT_TILE
    n_slot = TOP_K * t + N_EXPERTS * tm
    n_tiles = n_slot // tm
    flat_e = jnp.stack([e1[:, 0], e2[:, 0]], axis=1).reshape(-1)
    onehot = (flat_e[:, None] == jnp.arange(N_EXPERTS)[None, :]).astype(jnp.int32)
    csum = jnp.cumsum(onehot, axis=0)
    rank = jnp.sum(csum * onehot, axis=1) - 1
    counts = csum[-1]
    padded = ((counts + tm - 1) // tm) * tm
    ends = jnp.cumsum(padded)
    offs = ends - padded
    dest = (jnp.sum(onehot * offs[None, :], axis=1) + rank).astype(jnp.int32)
    n_used = (ends[-1] // tm).astype(jnp.int32)
    tile_start = jnp.arange(n_tiles, dtype=jnp.int32) * tm
    tile_e = jnp.sum((tile_start[:, None] >= ends[None, :]).astype(jnp.int32), axis=1)
    last_e = jnp.sum((((n_used - 1) * tm) >= ends).astype(jnp.int32))
    tile_e = jnp.where(jnp.arange(n_tiles) < n_used, tile_e, last_e).astype(jnp.int32)
    dest3 = dest.reshape(t // MOVE_TILE, 1, TOP_K * MOVE_TILE)

    xs = _moe_dispatch(dest3, xn.reshape(t, 8, LANES), n_slot)
    ys = _ffn_experts(tile_e, n_used.reshape(1), xs.reshape(n_slot, D_MODEL),
                      m_wg.astype(BF16), m_wu.astype(BF16), m_wd.astype(BF16))
    out = _moe_combine(dest3, ys.reshape(n_slot, 8, LANES), x3.reshape(t, 8, LANES),
                       w1b.reshape(t, 1, LANES), w2b.reshape(t, 1, LANES))
    return out.reshape(b, s, d)


def kernel(x, a_norm, a_w_in, a_q_norm, a_k_norm, a_pe_k, a_pe_v, a_cmp_k_w1, a_cmp_k_w2, a_cmp_v_w1, a_cmp_v_w2, a_gate_bias, a_fox_q_norm, a_fox_k_norm, a_fox_f_bias, a_w_out, f_norm, f_w_gate, f_w_up, f_w_down, c_norm, c_w_in, c_w_a2, c_b_a, c_o_norm, c_w_out, m_norm, m_router, m_w_gate, m_w_up, m_w_down):
    x = _layer_even(x, a_norm[0], a_w_in[0], a_q_norm[0], a_k_norm[0], a_pe_k[0], a_pe_v[0],
                    a_cmp_k_w1[0], a_cmp_k_w2[0], a_cmp_v_w1[0], a_cmp_v_w2[0], a_gate_bias[0],
                    a_fox_q_norm[0], a_fox_k_norm[0], a_fox_f_bias[0], a_w_out[0],
                    f_norm[0], f_w_gate[0], f_w_up[0], f_w_down[0])
    x = _layer_odd(x, c_norm[0], c_w_in[0], c_w_a2[0], c_b_a[0], c_o_norm[0], c_w_out[0],
                   m_norm[0], m_router[0], m_w_gate[0], m_w_up[0], m_w_down[0])
    return x
```

```python
import functools

import numpy as np
import jax
import jax.numpy as jnp
from jax import lax
from jax.experimental import pallas as pl
from jax.experimental.pallas import tpu as pltpu

F32 = jnp.float32
BF16 = jnp.bfloat16

D_MODEL = 1024
HEAD_DIM = 64
ROPE_DIM = 16
ROPE_THETA = 500000.0
NORM_EPS = 1e-6
NEG = -1e30
FORCE_SCORE = 1e6

NSA_HEADS = 8
CMP_LEN = 32
CMP_STRIDE = 16
SLC_BLOCK = 64
SLC_TOPK = 16
WINDOW = 512
FOX_HEADS = 8

GLA_HEADS = 4
GLA_DK = 512
GLA_DV = 1024
GLA_RANK = 16
GLA_TAU = 16.0
GLA_CHUNK = 64
GLA_SUB = 16

D_FF = 2816
N_EXPERTS = 8
TOP_K = 2

LANES = 128
FF_CHUNK = 256
VMEM_LIMIT = 56 * 1024 * 1024

ROW_TILE = 512
ATT_Q = 128
FOX_Q = 512
FOX_K = 256
SLC_K = 256
EXPERT_TILE = 512
MOVE_TILE = 256

NT_DIMS = (((1,), (1,)), ((), ()))


def _cp(sem, **kw):
    return pltpu.CompilerParams(dimension_semantics=sem, vmem_limit_bytes=VMEM_LIMIT, **kw)


def _dot(a, b):
    return jnp.dot(a, b, preferred_element_type=F32)


def _dot_nt(a, b):
    return lax.dot_general(a, b, NT_DIMS, preferred_element_type=F32)


def _split2(x):
    hi = x.astype(BF16)
    lo = (x - hi.astype(F32)).astype(BF16)
    return hi, lo


def _split3(x):
    hi = x.astype(BF16)
    r = x - hi.astype(F32)
    mid = r.astype(BF16)
    lo = (r - mid.astype(F32)).astype(BF16)
    return hi, mid, lo


def _rms_rows(x, gain):
    return x * lax.rsqrt(jnp.mean(x * x, axis=-1, keepdims=True) + NORM_EPS) * gain


def _log_sigmoid(x):
    return jnp.minimum(x, 0.0) - jnp.log1p(jnp.exp(-jnp.abs(x)))


def _sigmoid(x):
    return 1.0 / (1.0 + jnp.exp(-x))


def _seg_norm(y, gain, bd):
    ms = _dot((y * y).astype(BF16), bd)
    return y * lax.rsqrt(ms + NORM_EPS) * gain


def _rope(y, cos, sin, perm):
    return y * cos + _dot(y.astype(BF16), perm) * sin


def _block_diag_mean(width, seg):
    i = np.arange(width)
    return ((i[:, None] // seg) == (i[None, :] // seg)).astype(np.float32) / seg


def _rope_perm():
    p = np.zeros((LANES, LANES), np.float32)
    half = ROPE_DIM // 2
    for j in range(LANES):
        d = j % HEAD_DIM
        if d < half:
            p[j + half, j] = -1.0
        elif d < ROPE_DIM:
            p[j - half, j] = 1.0
    return p


def _rope_tables(pos):
    inv = ROPE_THETA ** (-np.arange(0, ROPE_DIM, 2, dtype=np.float64) / ROPE_DIM)
    ang = pos.astype(np.float64)[:, None] * inv[None, :]
    cos8, sin8 = np.cos(ang), np.sin(ang)
    n = pos.shape[0]
    ones = np.ones((n, HEAD_DIM - ROPE_DIM))
    zeros = np.zeros((n, HEAD_DIM - ROPE_DIM))
    c = np.concatenate([cos8, cos8, ones], axis=1)
    s = np.concatenate([sin8, sin8, zeros], axis=1)
    return (jnp.asarray(np.concatenate([c, c], axis=1), F32),
            jnp.asarray(np.concatenate([s, s], axis=1), F32))


PROJ_A_COLS = 3968
FOX_SLAB = 2560


def _proj_a_kernel(x_ref, g_ref, w_ref, hg_ref, cos_ref, sin_ref, bd_ref, bdlo_ref, perm_ref,
                   qa_ref, kv_ref, fox_ref, gf_ref):
    xn = _rms_rows(x_ref[...], g_ref[...]).astype(BF16)
    bd = bd_ref[...]
    perm = perm_ref[...]
    cos = cos_ref[...]
    sin = sin_ref[...]

    def tile(acc, i):
        return acc[:, i * LANES:(i + 1) * LANES]

    acc = _dot(xn, w_ref[:, 0:512])
    for i in range(4):
        y = _rope(_seg_norm(tile(acc, i), hg_ref[0:1, :], bd), cos, sin, perm)
        qa_ref[:, i * LANES:(i + 1) * LANES] = (y * 0.125).astype(BF16)

    acc = _dot(xn, w_ref[:, 512:1280])
    for i in range(6):
        y = tile(acc, i)
        if i in (2, 4):
            y = _rope(_seg_norm(y, hg_ref[1:2, :], bd), cos, sin, perm)
        kv_ref[:, i * LANES:(i + 1) * LANES] = y.astype(BF16)

    bd_lo = bdlo_ref[...]
    for c in range(4):
        acc = _dot(xn, w_ref[:, 1280 + 512 * c:1280 + 512 * (c + 1)])
        for i in range(4):
            y = tile(acc, i)
            if c < 2:
                y = _seg_norm(y, hg_ref[2:3, :], bd_lo) * 0.125 + hg_ref[4:5, :]
            else:
                y = _seg_norm(y, hg_ref[3:4, :], bd_lo)
            fox_ref[:, 512 * c + i * LANES:512 * c + (i + 1) * LANES] = y.astype(BF16)
    fox_ref[:, 2048:2560] = _dot(xn, w_ref[:, 3328:3840]).astype(BF16)

    gf_ref[...] = _dot(xn, w_ref[:, 3840:3968])


def _proj_a(x2d, norm_g, w_perm, head_gains, cos, sin, seq):
    t = x2d.shape[0]
    tm = ROW_TILE
    n_pos = seq // tm
    bd = jnp.asarray(_block_diag_mean(LANES, HEAD_DIM), BF16)
    lo = np.zeros((LANES, LANES), np.float32)
    lo[:HEAD_DIM, :] = 1.0 / HEAD_DIM
    perm = jnp.asarray(_rope_perm(), BF16)
    full = lambda i: (0, 0)
    return pl.pallas_call(
        _proj_a_kernel,
        grid=(t // tm,),
        in_specs=[
            pl.BlockSpec((tm, D_MODEL), lambda i: (i, 0)),
            pl.BlockSpec((1, D_MODEL), full),
            pl.BlockSpec((D_MODEL, PROJ_A_COLS), full),
            pl.BlockSpec((8, LANES), full),
            pl.BlockSpec((tm, LANES), lambda i: (i % n_pos, 0)),
            pl.BlockSpec((tm, LANES), lambda i: (i % n_pos, 0)),
            pl.BlockSpec((LANES, LANES), full),
            pl.BlockSpec((LANES, LANES), full),
            pl.BlockSpec((LANES, LANES), full),
        ],
        out_specs=[
            pl.BlockSpec((tm, 512), lambda i: (i, 0)),
            pl.BlockSpec((tm, 768), lambda i: (i, 0)),
            pl.BlockSpec((tm, FOX_SLAB), lambda i: (i, 0)),
            pl.BlockSpec((tm, LANES), lambda i: (i, 0)),
        ],
        out_shape=[
            jax.ShapeDtypeStruct((t, 512), BF16),
            jax.ShapeDtypeStruct((t, 768), BF16),
            jax.ShapeDtypeStruct((t, FOX_SLAB), BF16),
            jax.ShapeDtypeStruct((t, LANES), F32),
        ],
        compiler_params=_cp(("parallel",)),
        name="proj_a",
    )(x2d, norm_g, w_perm, head_gains, cos, sin, bd, jnp.asarray(lo, BF16), perm)


CUM_CHUNK = 256
FOX_BIAS_LANE = 64


def _cum_kernel(gf_ref, bias_ref, tri_ref, k_ref, place_ref, out_ref, ka_ref):
    tri = tri_ref[...]
    n_chunk = gf_ref.shape[0] // CUM_CHUNK
    carry = jnp.zeros((1, LANES), F32)
    for c in range(n_chunk):
        rows = slice(c * CUM_CHUNK, (c + 1) * CUM_CHUNK)
        lf = _log_sigmoid(gf_ref[rows, :] + bias_ref[...])
        hi, mid, lo = _split3(lf)
        cs = _dot(tri, hi) + _dot(tri, mid) + _dot(tri, lo) + carry
        out_ref[rows, :] = cs
        carry = cs[CUM_CHUNK - 1:CUM_CHUNK, :]
        c_hi, c_mid, c_lo = _split3(cs)
        placed = _dot(c_hi, place_ref[0]) + _dot(c_mid, place_ref[1]) + _dot(c_lo, place_ref[2])
        ka_ref[rows, :] = (k_ref[rows, :].astype(F32) + placed).astype(BF16)


def _fox_cumlog(gf3, bias_row, fox3):
    b, s, _ = gf3.shape
    tri = jnp.asarray(np.tril(np.ones((CUM_CHUNK, CUM_CHUNK), np.float32)), BF16)
    place = np.zeros((3, LANES, FOX_HEADS * LANES), np.float32)
    for h in range(FOX_HEADS):
        for piece in range(3):
            place[piece, 24 + h, h * LANES + FOX_BIAS_LANE + piece] = 1.0
    return pl.pallas_call(
        _cum_kernel,
        grid=(b,),
        in_specs=[
            pl.BlockSpec((None, s, LANES), lambda i: (i, 0, 0)),
            pl.BlockSpec((1, LANES), lambda i: (0, 0)),
            pl.BlockSpec((CUM_CHUNK, CUM_CHUNK), lambda i: (0, 0)),
            pl.BlockSpec((None, s, FOX_HEADS * LANES), lambda i: (i, 0, 1)),
            pl.BlockSpec((3, LANES, FOX_HEADS * LANES), lambda i: (0, 0, 0)),
        ],
        out_specs=[
            pl.BlockSpec((None, s, LANES), lambda i: (i, 0, 0)),
            pl.BlockSpec((None, s, FOX_HEADS * LANES), lambda i: (i, 0, 0)),
        ],
        out_shape=[
            jax.ShapeDtypeStruct((b, s, LANES), F32),
            jax.ShapeDtypeStruct((b, s, FOX_HEADS * LANES), BF16),
        ],
        compiler_params=_cp(("parallel",)),
        name="fox_cumlog",
    )(gf3, bias_row, tri, fox3, jnp.asarray(place, BF16))


def _gelu_tanh(x):
    return 0.5 * x * (1.0 + jnp.tanh(0.7978845608028654 * (x + 0.044715 * x * x * x)))


def _cmp_kernel(kx_ref, vx_ref, pek_ref, pev_ref, w1k_ref, w2k_ref, w1v_ref, w2v_ref,
                kg_ref, cos_ref, sin_ref, bd_ref, perm_ref, kc_ref, vc_ref):
    ncp = kx_ref.shape[0]

    def compress(x_ref, pe_ref, w1_ref, w2_ref):
        x = x_ref[...].astype(F32)
        first = _dot((x + pe_ref[0:1, :]).astype(BF16), w1_ref[0])
        second = _dot((x + pe_ref[1:2, :]).astype(BF16), w1_ref[1])
        pre = first + pltpu.roll(second, ncp - 1, 0)
        return _dot(_gelu_tanh(pre).astype(BF16), w2_ref[...])

    k = compress(kx_ref, pek_ref, w1k_ref, w2k_ref)
    k = _rope(_seg_norm(k, kg_ref[...], bd_ref[...]), cos_ref[...], sin_ref[...], perm_ref[...])
    kc_ref[...] = k.astype(BF16)
    vc_ref[...] = compress(vx_ref, pev_ref, w1v_ref, w2v_ref).astype(BF16)


def _nsa_compress(kx, vx, pek, pev, w1k, w2k, w1v, w2v, kgain, ccos, csin):
    b, ncp, width = kx.shape
    bd = jnp.asarray(_block_diag_mean(LANES, HEAD_DIM), BF16)
    perm = jnp.asarray(_rope_perm(), BF16)
    c2 = lambda i: (0, 0)
    c3 = lambda i: (0, 0, 0)
    return pl.pallas_call(
        _cmp_kernel,
        grid=(b,),
        in_specs=[
            pl.BlockSpec((None, ncp, width), lambda i: (i, 0, 0)),
            pl.BlockSpec((None, ncp, width), lambda i: (i, 0, 0)),
            pl.BlockSpec((2, width), c2),
            pl.BlockSpec((2, width), c2),
            pl.BlockSpec((2, width, LANES), c3),
            pl.BlockSpec((LANES, LANES), c2),
            pl.BlockSpec((2, width, LANES), c3),
            pl.BlockSpec((LANES, LANES), c2),
            pl.BlockSpec((1, LANES), c2),
            pl.BlockSpec((ncp, LANES), c2),
            pl.BlockSpec((ncp, LANES), c2),
            pl.BlockSpec((LANES, LANES), c2),
            pl.BlockSpec((LANES, LANES), c2),
        ],
        out_specs=[
            pl.BlockSpec((None, ncp, LANES), lambda i: (i, 0, 0)),
            pl.BlockSpec((None, ncp, LANES), lambda i: (i, 0, 0)),
        ],
        out_shape=[
            jax.ShapeDtypeStruct((b, ncp, LANES), BF16),
            jax.ShapeDtypeStruct((b, ncp, LANES), BF16),
        ],
        compiler_params=_cp(("parallel",)),
        name="nsa_compress",
    )(kx, vx, pek, pev, w1k, w2k, w1v, w2v, kgain, ccos, csin, bd, perm)


def _nsa_cmp_kernel(q_ref, kc_ref, vc_ref, ovt_ref, eye_ref, oc_ref, sb_ref, *, n_cmp, n_slc):
    tq = ATT_Q
    q0 = pl.program_id(1) * tq
    kc = kc_ref[...]
    vc = vc_ref[...]
    ncp = kc.shape[0]
    ovt = ovt_ref[...]
    lo_half = lax.broadcasted_iota(jnp.int32, (tq, LANES), 1) < HEAD_DIM

    n_lane = lax.broadcasted_iota(jnp.int32, (tq, ncp), 1)
    t_sub = q0 + lax.broadcasted_iota(jnp.int32, (tq, ncp), 0)
    mask = (n_lane * CMP_STRIDE + (CMP_LEN - 1) <= t_sub) & (n_lane < n_cmp)
    n_sub = lax.broadcasted_iota(jnp.int32, (ncp, tq), 0)
    t_lane = q0 + lax.broadcasted_iota(jnp.int32, (ncp, tq), 1)
    mask_t = (n_sub * CMP_STRIDE + (CMP_LEN - 1) <= t_lane) & (n_sub < n_cmp)

    imp_t = [jnp.zeros((64, tq), F32), jnp.zeros((64, tq), F32)]
    for t in range(4):
        qt = q_ref[:, t * LANES:(t + 1) * LANES]
        outs = []
        for g in range(2):
            qm = jnp.where(lo_half if g == 0 else jnp.logical_not(lo_half), qt, jnp.zeros_like(qt))
            s = jnp.where(mask, _dot_nt(qm, kc), NEG)
            p = jnp.where(mask, jnp.exp(s - jnp.max(s, axis=-1, keepdims=True)), 0.0)
            p = p / jnp.maximum(jnp.sum(p, axis=-1, keepdims=True), 1e-30)
            outs.append(_dot(p.astype(BF16), vc))
            st = jnp.where(mask_t, _dot_nt(kc, qm), NEG)
            pt = jnp.where(mask_t, jnp.exp(st - jnp.max(st, axis=0, keepdims=True)), 0.0)
            pt = pt / jnp.maximum(jnp.sum(pt, axis=0, keepdims=True), 1e-30)
            hi, lo = _split2(pt)
            imp_t[g] = imp_t[g] + _dot(ovt, hi) + _dot(ovt, lo)
        oc_ref[:, t * LANES:(t + 1) * LANES] = jnp.where(lo_half, outs[0], outs[1]).astype(BF16)

    blk = lax.broadcasted_iota(jnp.int32, (64, tq), 0)
    cur = (q0 + lax.broadcasted_iota(jnp.int32, (64, tq), 1)) >> 6
    forced = (blk == 0) | (blk == cur) | (blk == cur - 1)
    future = blk > cur
    exists = blk < n_slc
    biases = []
    for g in range(2):
        v = jnp.where(forced, FORCE_SCORE, jnp.where(future, -FORCE_SCORE, imp_t[g]))
        v = jnp.where(exists, v, -3e38)
        sel = jnp.zeros((64, tq), jnp.int32)
        for _ in range(min(SLC_TOPK, n_slc)):
            m = jnp.max(v, axis=0, keepdims=True)
            first = jnp.min(jnp.where(v == m, blk, 64), axis=0, keepdims=True)
            hit = blk == first
            sel = jnp.where(hit, 1, sel)
            v = jnp.where(hit, -3.2e38, v)
        keep = (sel == 1) & jnp.logical_not(future) & exists
        biases.append(jnp.where(keep, 0.0, NEG).astype(BF16))
    sb_t = jnp.concatenate([biases[1], biases[0]], axis=0)
    sb_ref[...] = _dot_nt(eye_ref[...], sb_t).astype(BF16)


def _nsa_cmp_select(qa3, kcmp, vcmp, ovt, n_cmp, n_slc):
    b, s, _ = qa3.shape
    ncp = kcmp.shape[1]
    eye = jnp.asarray(np.eye(ATT_Q, dtype=np.float32), BF16)
    kern = functools.partial(_nsa_cmp_kernel, n_cmp=n_cmp, n_slc=n_slc)
    return pl.pallas_call(
        kern,
        grid=(b, s // ATT_Q),
        in_specs=[
            pl.BlockSpec((None, ATT_Q, 512), lambda i, j: (i, j, 0)),
            pl.BlockSpec((None, ncp, LANES), lambda i, j: (i, 0, 0)),
            pl.BlockSpec((None, ncp, LANES), lambda i, j: (i, 0, 0)),
            pl.BlockSpec((64, ncp), lambda i, j: (0, 0)),
            pl.BlockSpec((ATT_Q, ATT_Q), lambda i, j: (0, 0)),
        ],
        out_specs=[
            pl.BlockSpec((None, ATT_Q, 512), lambda i, j: (i, j, 0)),
            pl.BlockSpec((None, ATT_Q, LANES), lambda i, j: (i, j, 0)),
        ],
        out_shape=[
            jax.ShapeDtypeStruct((b, s, 512), BF16),
            jax.ShapeDtypeStruct((b, s, LANES), BF16),
        ],
        compiler_params=_cp(("parallel", "parallel")),
        name="nsa_cmp_select",
    )(qa3, kcmp, vcmp, ovt, eye)


V_ROWS = 80


def _flash_step_t(carry, s_t, v_t, *, query_bias=None, mask=None):
    m, acc = carry
    if mask is not None:
        s_t = jnp.where(mask, s_t, NEG)
    col_max = jnp.max(s_t, axis=0, keepdims=True)
    if query_bias is not None:
        m_new = jnp.maximum(m, col_max + query_bias)
        shift = m_new - query_bias
    else:
        m_new = jnp.maximum(m, col_max)
        shift = m_new
    p = jnp.exp(s_t - shift).astype(BF16)
    acc = jnp.exp(m - m_new) * acc + _dot(v_t, p)
    return m_new, acc


def _flash_init_t(queries):
    return (jnp.full((1, queries), NEG, F32), jnp.zeros((V_ROWS, queries), F32))


def _values_t(v):
    b, s, w = v.shape
    heads = w // HEAD_DIM
    vt = v.reshape(b, s, heads, HEAD_DIM).transpose(0, 2, 3, 1)
    ones = jnp.ones((b, heads, 1, s), v.dtype)
    pad = jnp.zeros((b, heads, V_ROWS - HEAD_DIM - 1, s), v.dtype)
    return jnp.concatenate([vt, ones, pad], axis=2)


def _nsa_slc_t_kernel(q_ref, sb_ref, k0_ref, k1_ref, v_ref, eye_ref, o_ref, sa_ref, sc_ref):
    tq, tk = ATT_Q, SLC_K
    q0 = pl.program_id(1) * tq
    sb = sb_ref[...]
    lo_half = lax.broadcasted_iota(jnp.int32, (tq, LANES), 1) < HEAD_DIM
    cols = 4 * tq
    n_before = q0 // tk
    k_refs = (k0_ref, k1_ref)
    slots = (sa_ref, sc_ref)

    qq = []
    for g in range(2):
        parts = []
        for t in range(4):
            qt = q_ref[:, t * LANES:(t + 1) * LANES]
            parts.append(jnp.where(lo_half, qt, sb) if g == 0 else jnp.where(lo_half, sb, qt))
        qq.append(jnp.concatenate(parts, axis=0))

    def issue(j, slot):
        off = pl.multiple_of(j * tk, tk)
        for g in range(2):
            slots[slot][g] = _dot_nt(k_refs[g][pl.ds(off, tk), :], qq[g])

    def consume(j, carry, slot, mask=None):
        off = pl.multiple_of(j * tk, tk)
        return tuple(_flash_step_t(carry[g], slots[slot][g], v_ref[g, :, pl.ds(off, tk)], mask=mask)
                     for g in range(2))

    def step(jj, carry):
        j = 2 * jj
        issue(j + 1, 1)
        carry = consume(j, carry, 0)
        issue(j + 2, 0)
        return consume(j + 1, carry, 1)

    issue(0, 0)
    carry = lax.fori_loop(0, n_before // 2, step, (_flash_init_t(cols), _flash_init_t(cols)))

    q_in_chunk = q0 - n_before * tk
    query = lax.broadcasted_iota(jnp.int32, (tk, cols), 1) & (tq - 1)
    causal = lax.broadcasted_iota(jnp.int32, (tk, cols), 0) - query <= q_in_chunk

    def even_tail(carry):
        return consume(n_before, carry, 0, causal)

    def odd_tail(carry):
        issue(n_before, 1)
        carry = consume(n_before - 1, carry, 0)
        return consume(n_before, carry, 1, causal)

    carry = lax.cond((n_before & 1) == 0, even_tail, odd_tail, carry)

    outs = [(acc[0:HEAD_DIM] / acc[HEAD_DIM:HEAD_DIM + 1]).astype(BF16) for (_, acc) in carry]
    eye = eye_ref[...]
    for t in range(4):
        cs = slice(t * tq, (t + 1) * tq)
        both = jnp.concatenate([outs[0][:, cs], outs[1][:, cs]], axis=0)
        o_ref[:, t * LANES:(t + 1) * LANES] = _dot_nt(eye, both).astype(BF16)


def _nsa_selected_t(qa3, sb, k_aug0, k_aug1, v_t):
    b, s, _ = qa3.shape
    blk = lambda i, j: (i, j, 0)
    whole = lambda i, j: (i, 0, 0)
    eye = jnp.asarray(np.eye(ATT_Q, dtype=np.float32), BF16)
    return pl.pallas_call(
        _nsa_slc_t_kernel,
        grid=(b, s // ATT_Q),
        in_specs=[
            pl.BlockSpec((None, ATT_Q, 512), blk),
            pl.BlockSpec((None, ATT_Q, LANES), blk),
            pl.BlockSpec((None, s, LANES), whole),
            pl.BlockSpec((None, s, LANES), whole),
            pl.BlockSpec((None, 2, V_ROWS, s), lambda i, j: (i, 0, 0, 0)),
            pl.BlockSpec((ATT_Q, ATT_Q), lambda i, j: (0, 0)),
        ],
        out_specs=pl.BlockSpec((None, ATT_Q, 512), blk),
        out_shape=jax.ShapeDtypeStruct((b, s, 512), BF16),
        scratch_shapes=[pltpu.VMEM((2, SLC_K, 4 * ATT_Q), F32), pltpu.VMEM((2, SLC_K, 4 * ATT_Q), F32)],
        compiler_params=_cp(("parallel", "parallel")),
        name="nsa_selected",
    )(qa3, sb, k_aug0, k_aug1, v_t, eye)


WIN_STRIP = 32


def _nsa_win_kernel(q_ref, k_ref, v_ref, o_ref, p0_ref, p1_ref):
    tq = ATT_Q
    span = WINDOW + tq
    q0 = pl.program_id(1) * tq
    start = pl.multiple_of(jnp.maximum(q0 - WINDOW, 0), tq)
    k = k_ref[pl.ds(start, span), :]
    v = v_ref[pl.ds(start, span), :]
    rows = 4 * tq
    lo_half = lax.broadcasted_iota(jnp.int32, (tq, LANES), 1) < HEAD_DIM
    shape = (WIN_STRIP, span)
    behind = (q0 - start) + lax.broadcasted_iota(jnp.int32, shape, 0) - lax.broadcasted_iota(jnp.int32, shape, 1)
    outs = []
    for g, p_ref in enumerate((p0_ref, p1_ref)):
        keep = lo_half if g == 0 else jnp.logical_not(lo_half)
        parts = []
        for t in range(4):
            qt = q_ref[:, t * LANES:(t + 1) * LANES]
            parts.append(jnp.where(keep, qt, jnp.zeros_like(qt)))
        s = _dot_nt(jnp.concatenate(parts, axis=0), k)
        ls = []
        for r in range(rows // WIN_STRIP):
            rs = slice(r * WIN_STRIP, (r + 1) * WIN_STRIP)
            dist = behind + (r * WIN_STRIP) % tq
            t_s = jnp.where((dist >= 0) & (dist < WINDOW), s[rs], NEG)
            p = jnp.exp(t_s - jnp.max(t_s, axis=-1, keepdims=True))
            ls.append(jnp.sum(p, axis=-1, keepdims=True))
            p_ref[rs, :] = p.astype(BF16)
        outs.append(_dot(p_ref[...], v) / jnp.concatenate(ls, axis=0))
    for t in range(4):
        rs = slice(t * tq, (t + 1) * tq)
        o_ref[:, t * LANES:(t + 1) * LANES] = jnp.where(lo_half, outs[0][rs], outs[1][rs]).astype(BF16)


def _nsa_window(qa3, k2, v2):
    b, s, _ = qa3.shape
    blk = lambda i, j: (i, j, 0)
    whole = lambda i, j: (i, 0, 0)
    return pl.pallas_call(
        _nsa_win_kernel,
        grid=(b, s // ATT_Q),
        in_specs=[
            pl.BlockSpec((None, ATT_Q, 512), blk),
            pl.BlockSpec((None, s, LANES), whole),
            pl.BlockSpec((None, s, LANES), whole),
        ],
        out_specs=pl.BlockSpec((None, ATT_Q, 512), blk),
        out_shape=jax.ShapeDtypeStruct((b, s, 512), BF16),
        scratch_shapes=[pltpu.VMEM((4 * ATT_Q, WINDOW + ATT_Q), BF16),
                        pltpu.VMEM((4 * ATT_Q, WINDOW + ATT_Q), BF16)],
        compiler_params=_cp(("parallel", "parallel")),
        name="nsa_window",
    )(qa3, k2, v2)


def _fox_t_kernel(q_ref, k_ref, v_ref, cr_ref, o_ref, sa_ref, sb_ref):
    tq, tk = FOX_Q, FOX_K
    q0 = pl.program_id(1) * tq
    n_before = q0 // tk
    n_diag = tq // tk
    for pair in range(FOX_HEADS // 2):
        heads = (2 * pair, 2 * pair + 1)
        qs = tuple(q_ref[:, h * LANES:(h + 1) * LANES] for h in heads)
        cq = tuple(cr_ref[h:h + 1, pl.ds(pl.multiple_of(q0, tq), tq)] for h in heads)

        def scores(j, i, q_lo=0):
            off = pl.multiple_of(j * tk, tk)
            h = heads[i]
            return _dot_nt(k_ref[pl.ds(off, tk), h * LANES:(h + 1) * LANES], qs[i][q_lo:])

        slots = (sa_ref, sb_ref)

        def consume(j, carry, slot):
            off = pl.multiple_of(j * tk, tk)
            return tuple(_flash_step_t(carry[i], slots[slot][i], v_ref[heads[i], :, pl.ds(off, tk)],
                                       query_bias=cq[i]) for i in range(2))

        def step(jj, carry):
            j = 2 * jj
            for i in range(2):
                sb_ref[i] = scores(j + 1, i)
            carry = consume(j, carry, 0)
            for i in range(2):
                sa_ref[i] = scores(j + 2, i)
            return consume(j + 1, carry, 1)

        for i in range(2):
            sa_ref[i] = scores(0, i)
        carry = lax.fori_loop(0, n_before // 2, step, (_flash_init_t(tq), _flash_init_t(tq)))

        late = [[scores(n_before + d, i, d * tk) for i in range(2)] for d in range(1, n_diag)]
        for d in range(n_diag):
            q_lo = d * tk
            off = pl.multiple_of((n_before + d) * tk, tk)
            shape = (tk, tq - q_lo)
            mask = (lax.broadcasted_iota(jnp.int32, shape, 0) + (d * tk - q_lo)
                    <= lax.broadcasted_iota(jnp.int32, shape, 1))
            out = []
            for i in range(2):
                m, acc = carry[i]
                s_t = sa_ref[i] if d == 0 else late[d - 1][i]
                new = _flash_step_t((m[:, q_lo:], acc[:, q_lo:]), s_t, v_ref[heads[i], :, pl.ds(off, tk)],
                                    query_bias=cq[i][:, q_lo:], mask=mask)
                if q_lo:
                    new = tuple(jnp.concatenate([old[:, :q_lo], part], axis=1)
                                for old, part in zip((m, acc), new))
                out.append(new)
            carry = tuple(out)
        for i, h in enumerate(heads):
            acc = carry[i][1]
            o_ref[h * HEAD_DIM:(h + 1) * HEAD_DIM, :] = (
                acc[0:HEAD_DIM] / acc[HEAD_DIM:HEAD_DIM + 1]).astype(BF16)


def _fox_attention_t(fox3, k_aug, v_t, cum_row):
    b, s, _ = fox3.shape
    width = FOX_HEADS * LANES
    return pl.pallas_call(
        _fox_t_kernel,
        grid=(b, s // FOX_Q),
        in_specs=[
            pl.BlockSpec((None, FOX_Q, width), lambda i, j: (i, j, 0)),
            pl.BlockSpec((None, s, width), lambda i, j: (i, 0, 0)),
            pl.BlockSpec((None, FOX_HEADS, V_ROWS, s), lambda i, j: (i, 0, 0, 0)),
            pl.BlockSpec((None, FOX_HEADS, s), lambda i, j: (i, 0, 0)),
        ],
        out_specs=pl.BlockSpec((None, FOX_HEADS * HEAD_DIM, FOX_Q), lambda i, j: (i, 0, j)),
        out_shape=jax.ShapeDtypeStruct((b, FOX_HEADS * HEAD_DIM, s), BF16),
        scratch_shapes=[pltpu.VMEM((2, FOX_K, FOX_Q), F32), pltpu.VMEM((2, FOX_K, FOX_Q), F32)],
        compiler_params=_cp(("parallel", "parallel")),
        name="fox_attention",
    )(fox3, k_aug, v_t, cum_row)


def _out_a_kernel(oc_ref, os_ref, ow_ref, ob_ref, gf_ref, gb_ref, gx_ref, x_ref, w_ref, o_ref):
    gates = _sigmoid(gf_ref[...] + gb_ref[...])
    hi, lo = _split2(gates)
    o_a = None
    for j, br in enumerate((oc_ref, os_ref, ow_ref)):
        gexp = _dot(hi, gx_ref[j]) + _dot(lo, gx_ref[j])
        term = gexp * br[...].astype(F32)
        o_a = term if o_a is None else o_a + term
    y = _dot(o_a.astype(BF16), w_ref[0:512, :]) + _dot(ob_ref[...], w_ref[512:1024, :])
    o_ref[...] = x_ref[...] + y


def _out_a(oc, os_, ow, ob, gf, gate_bias_row, gate_expand, x2d, w_out_perm):
    t = x2d.shape[0]
    tm = ROW_TILE
    row = lambda i: (i, 0)
    c2 = lambda i: (0, 0)
    return pl.pallas_call(
        _out_a_kernel,
        grid=(t // tm,),
        in_specs=[
            pl.BlockSpec((tm, 512), row),
            pl.BlockSpec((tm, 512), row),
            pl.BlockSpec((tm, 512), row),
            pl.BlockSpec((tm, 512), row),
            pl.BlockSpec((tm, LANES), row),
            pl.BlockSpec((1, LANES), c2),
            pl.BlockSpec((3, LANES, 512), lambda i: (0, 0, 0)),
            pl.BlockSpec((tm, D_MODEL), row),
            pl.BlockSpec((D_MODEL, D_MODEL), c2),
        ],
        out_specs=pl.BlockSpec((tm, D_MODEL), row),
        out_shape=jax.ShapeDtypeStruct((t, D_MODEL), F32),
        compiler_params=_cp(("parallel",)),
        name="out_a",
    )(oc, os_, ow, ob, gf, gate_bias_row, gate_expand, x2d, w_out_perm)


def _swiglu_body(xn, wg_ref, wu_ref, wd_ref, h_ref):
    for c in range(D_FF // FF_CHUNK):
        cols = slice(c * FF_CHUNK, (c + 1) * FF_CHUNK)
        g = _dot(xn, wg_ref[:, cols])
        u = _dot(xn, wu_ref[:, cols])
        h_ref[:, cols] = (g * _sigmoid(g) * u).astype(BF16)
    return _dot(h_ref[...], wd_ref[...])


def _ffn_dense_kernel(x_ref, g_ref, wg_ref, wu_ref, wd_ref, o_ref, h_ref):
    x = x_ref[...]
    xn = _rms_rows(x, g_ref[...]).astype(BF16)
    o_ref[...] = x + _swiglu_body(xn, wg_ref, wu_ref, wd_ref, h_ref)


def _ffn_dense(x2d, norm_g, wg, wu, wd):
    t = x2d.shape[0]
    tm = ROW_TILE
    c2 = lambda i: (0, 0)
    return pl.pallas_call(
        _ffn_dense_kernel,
        grid=(t // tm,),
        in_specs=[
            pl.BlockSpec((tm, D_MODEL), lambda i: (i, 0)),
            pl.BlockSpec((1, D_MODEL), c2),
            pl.BlockSpec((D_MODEL, D_FF), c2),
            pl.BlockSpec((D_MODEL, D_FF), c2),
            pl.BlockSpec((D_FF, D_MODEL), c2),
        ],
        out_specs=pl.BlockSpec((tm, D_MODEL), lambda i: (i, 0)),
        out_shape=jax.ShapeDtypeStruct((t, D_MODEL), F32),
        scratch_shapes=[pltpu.VMEM((tm, D_FF), BF16)],
        compiler_params=_cp(("parallel",)),
        name="ffn_dense",
    )(x2d, norm_g, wg, wu, wd)


def _ffn_expert_kernel(te_ref, nu_ref, x_ref, wg_ref, wu_ref, wd_ref, o_ref, h_ref):
    @pl.when(pl.program_id(0) < nu_ref[0])
    def _():
        o_ref[...] = _swiglu_body(x_ref[...], wg_ref, wu_ref, wd_ref, h_ref)

    @pl.when(pl.program_id(0) >= nu_ref[0])
    def _():
        o_ref[...] = jnp.zeros_like(o_ref)


def _ffn_experts(tile_expert, n_used, xs, wg, wu, wd):
    n_slot = xs.shape[0]
    tm = EXPERT_TILE
    n_tiles = n_slot // tm
    row = lambda i, te, nu: (jnp.minimum(i, nu[0] - 1), 0)
    wsel = lambda i, te, nu: (te[i], 0, 0)
    grid_spec = pltpu.PrefetchScalarGridSpec(
        num_scalar_prefetch=2,
        grid=(n_tiles,),
        in_specs=[
            pl.BlockSpec((tm, D_MODEL), row),
            pl.BlockSpec((None, D_MODEL, D_FF), wsel),
            pl.BlockSpec((None, D_MODEL, D_FF), wsel),
            pl.BlockSpec((None, D_FF, D_MODEL), wsel),
        ],
        out_specs=pl.BlockSpec((tm, D_MODEL), lambda i, te, nu: (i, 0)),
        scratch_shapes=[pltpu.VMEM((tm, D_FF), BF16)],
    )
    return pl.pallas_call(
        _ffn_expert_kernel,
        grid_spec=grid_spec,
        out_shape=jax.ShapeDtypeStruct((n_slot, D_MODEL), F32),
        compiler_params=_cp(("arbitrary",)),
        name="ffn_experts",
    )(tile_expert, n_used, xs, wg, wu, wd)


def _proj_c_kernel(x_ref, g_ref, w_ref, wa_ref, ba_ref, tri_ref, qk_ref, v_ref, r_ref, la_ref):
    xn = _rms_rows(x_ref[...], g_ref[...]).astype(BF16)
    for c in range(2):
        qk_ref[:, 512 * c:512 * (c + 1)] = _dot(xn, w_ref[:, 512 * c:512 * (c + 1)]).astype(BF16)
    for c in range(2):
        v_ref[:, 512 * c:512 * (c + 1)] = _dot(xn, w_ref[:, 1024 + 512 * c:1536 + 512 * c]).astype(BF16)
    for c in range(2):
        r_ref[:, 512 * c:512 * (c + 1)] = _dot(xn, w_ref[:, 2048 + 512 * c:2560 + 512 * c]).astype(BF16)
    a1 = _dot(xn, w_ref[:, 3072:3200])
    hi, lo = _split2(a1)
    wa = wa_ref[...]
    pre = _dot(hi, wa) + _dot(lo, wa) + ba_ref[...]
    la = _log_sigmoid(pre) * (1.0 / GLA_TAU)
    tri = tri_ref[...]
    c_hi, c_mid, c_lo = _split3(la)
    la_ref[...] = _dot(tri, c_hi) + _dot(tri, c_mid) + _dot(tri, c_lo)


def _proj_c(x2d, norm_g, w_pad, wa_pad, ba_row):
    t = x2d.shape[0]
    tm = ROW_TILE
    idx = np.arange(tm)
    tri = ((idx[:, None] // GLA_CHUNK == idx[None, :] // GLA_CHUNK) & (idx[:, None] >= idx[None, :])).astype(np.float32)
    row = lambda i: (i, 0)
    c2 = lambda i: (0, 0)
    return pl.pallas_call(
        _proj_c_kernel,
        grid=(t // tm,),
        in_specs=[
            pl.BlockSpec((tm, D_MODEL), row),
            pl.BlockSpec((1, D_MODEL), c2),
            pl.BlockSpec((D_MODEL, 3200), c2),
            pl.BlockSpec((LANES, GLA_DK), c2),
            pl.BlockSpec((1, GLA_DK), c2),
            pl.BlockSpec((tm, tm), c2),
        ],
        out_specs=[
            pl.BlockSpec((tm, 1024), row),
            pl.BlockSpec((tm, 1024), row),
            pl.BlockSpec((tm, 1024), row),
            pl.BlockSpec((tm, GLA_DK), row),
        ],
        out_shape=[
            jax.ShapeDtypeStruct((t, 1024), BF16),
            jax.ShapeDtypeStruct((t, 1024), BF16),
            jax.ShapeDtypeStruct((t, 1024), BF16),
            jax.ShapeDtypeStruct((t, GLA_DK), F32),
        ],
        compiler_params=_cp(("parallel",)),
        name="proj_c",
    )(x2d, norm_g, w_pad, wa_pad, ba_row, jnp.asarray(tri, BF16))


GLA_STEP_HEADS = 2


def _gla_kernel(q_ref, k_ref, v_ref, b_ref, eye_ref, o_ref, st_ref):
    c_len, sub = GLA_CHUNK, GLA_SUB
    n_sub = c_len // sub
    hs = range(GLA_STEP_HEADS)
    dk = q_ref.shape[1] // GLA_STEP_HEADS
    dv = v_ref.shape[1] // GLA_STEP_HEADS
    scale = dk ** -0.5
    eye = eye_ref[...]
    st_ref[...] = jnp.zeros_like(st_ref)
    row = lax.broadcasted_iota(jnp.int32, (c_len, c_len), 0)
    col = lax.broadcasted_iota(jnp.int32, (c_len, c_len), 1)
    col_in_sub = col & (sub - 1)
    same_sub_causal = ((row >> 4) == (col >> 4)) & ((row & (sub - 1)) >= col_in_sub)
    earlier_sub = (col >> 4) < (row >> 4)

    def chunk(c, _):
        rows = pl.ds(pl.multiple_of(c * c_len, c_len), c_len)
        q = [q_ref[rows, h * dk:(h + 1) * dk].astype(F32) * scale for h in hs]
        k = [k_ref[rows, h * dk:(h + 1) * dk].astype(F32) for h in hs]
        vb = [v_ref[rows, h * dv:(h + 1) * dv] for h in hs]
        b = [b_ref[rows, h * dk:(h + 1) * dk] for h in hs]
        b_last = [b[h][c_len - 1:c_len, :] for h in hs]
        st_t = [st_ref[h] for h in hs]

        v_t = [_dot_nt(eye, vb[h]).astype(BF16) for h in hs]
        inter = [_dot_nt((q[h] * jnp.exp(b[h])).astype(BF16), st_t[h].astype(BF16)) for h in hs]
        att_rows = [[jnp.zeros((sub, c_len), F32)] for _ in hs]
        for i in range(1, n_sub):
            for h in hs:
                r = b[h][i * sub:i * sub + 1, :]
                qi = (q[h][i * sub:(i + 1) * sub] * jnp.exp(b[h][i * sub:(i + 1) * sub] - r)).astype(BF16)
                kj = (k[h] * jnp.exp(jnp.minimum(r - b[h], 0.0))).astype(BF16)
                att_rows[h].append(_dot_nt(qi, kj))
        for h in hs:
            kd = (k[h] * jnp.exp(b_last[h] - b[h])).astype(BF16)
            st_ref[h] = st_t[h] * jnp.exp(b_last[h]) + _dot(v_t[h], kd)

        att_d = [jnp.zeros((c_len, c_len), F32) for _ in hs]
        for j in range(sub):
            def rows_j(x):
                return jnp.concatenate(
                    [jnp.broadcast_to(x[s * sub + j:s * sub + j + 1, :], (sub, x.shape[1]))
                     for s in range(n_sub)], axis=0)
            for h in hs:
                e = jnp.exp(jnp.minimum(b[h] - rows_j(b[h]), 0.0))
                a = jnp.sum(q[h] * e * rows_j(k[h]), axis=-1, keepdims=True)
                att_d[h] = jnp.where(col_in_sub == j, a, att_d[h])
        for h in hs:
            att = jnp.where(earlier_sub, jnp.concatenate(att_rows[h], axis=0),
                            jnp.where(same_sub_causal, att_d[h], 0.0)).astype(BF16)
            o_ref[rows, h * dv:(h + 1) * dv] = (inter[h] + _dot(att, vb[h])).astype(BF16)
        return 0

    def two_chunks(cc, carry):
        chunk(2 * cc, carry)
        return chunk(2 * cc + 1, carry)

    lax.fori_loop(0, q_ref.shape[0] // (2 * c_len), two_chunks, 0)


def _gla(qk3, v3, b3):
    b, s, _ = qk3.shape
    n = GLA_STEP_HEADS
    dk = GLA_DK // GLA_HEADS
    dv = GLA_DV // GLA_HEADS
    steps = GLA_HEADS // n
    eye = jnp.asarray(np.eye(dv, dtype=np.float32), BF16)
    return pl.pallas_call(
        _gla_kernel,
        grid=(b, steps),
        in_specs=[
            pl.BlockSpec((None, s, n * dk), lambda i, h: (i, 0, h)),
            pl.BlockSpec((None, s, n * dk), lambda i, h: (i, 0, steps + h)),
            pl.BlockSpec((None, s, n * dv), lambda i, h: (i, 0, h)),
            pl.BlockSpec((None, s, n * dk), lambda i, h: (i, 0, h)),
            pl.BlockSpec((dv, dv), lambda i, h: (0, 0)),
        ],
        out_specs=pl.BlockSpec((None, s, n * dv), lambda i, h: (i, 0, h)),
        out_shape=jax.ShapeDtypeStruct((b, s, GLA_DV), BF16),
        scratch_shapes=[pltpu.VMEM((n, dv, dk), F32)],
        compiler_params=_cp(("parallel", "parallel")),
        name="gla",
    )(qk3, qk3, v3, b3, eye)


def _out_c_kernel(o_ref, r_ref, x_ref, on_ref, bd_ref, w_ref, mg_ref, rt_ref,
                  x3_ref, xn_ref, e1_ref, e2_ref, w1_ref, w2_ref):
    dv = GLA_DV // GLA_HEADS
    bd = bd_ref[...]
    parts = []
    for h in range(GLA_HEADS):
        cols = slice(h * dv, (h + 1) * dv)
        oh = _seg_norm(o_ref[:, cols].astype(F32), on_ref[...], bd)
        r = r_ref[:, cols].astype(F32)
        parts.append((oh * (r * _sigmoid(r))).astype(BF16))
    x3 = x_ref[...] + _dot(jnp.concatenate(parts, axis=1), w_ref[...])
    x3_ref[...] = x3

    xn = _rms_rows(x3, mg_ref[...])
    xn_ref[...] = xn.astype(BF16)
    xh, xl = _split2(xn)
    logits = _dot(xh, rt_ref[0]) + _dot(xl, rt_ref[0]) + _dot(xh, rt_ref[1])
    lane = lax.broadcasted_iota(jnp.int32, logits.shape, 1)
    logits = jnp.where(lane < N_EXPERTS, logits, -3e38)
    v1 = jnp.max(logits, axis=-1, keepdims=True)
    i1 = jnp.min(jnp.where(logits == v1, lane, LANES), axis=-1, keepdims=True)
    rest = jnp.where(lane == i1, -3e38, logits)
    v2 = jnp.max(rest, axis=-1, keepdims=True)
    i2 = jnp.min(jnp.where(rest == v2, lane, LANES), axis=-1, keepdims=True)
    ex = jnp.exp(v2 - v1)
    den = 1.0 + ex
    e1_ref[...] = jnp.broadcast_to(i1, logits.shape)
    e2_ref[...] = jnp.broadcast_to(i2, logits.shape)
    w1_ref[...] = jnp.broadcast_to(1.0 / den, logits.shape)
    w2_ref[...] = jnp.broadcast_to(ex / den, logits.shape)


def _out_c(o2d, r2d, x2d, onorm_row, w_out, mnorm_row, router2):
    t = x2d.shape[0]
    tm = ROW_TILE
    dv = GLA_DV // GLA_HEADS
    bd = jnp.asarray(_block_diag_mean(dv, dv), BF16)
    row = lambda i: (i, 0)
    c2 = lambda i: (0, 0)
    slab = pl.BlockSpec((tm, LANES), row)
    return pl.pallas_call(
        _out_c_kernel,
        grid=(t // tm,),
        in_specs=[
            pl.BlockSpec((tm, GLA_DV), row),
            pl.BlockSpec((tm, GLA_DV), row),
            pl.BlockSpec((tm, D_MODEL), row),
            pl.BlockSpec((1, dv), c2),
            pl.BlockSpec((dv, dv), c2),
            pl.BlockSpec((D_MODEL, D_MODEL), c2),
            pl.BlockSpec((1, D_MODEL), c2),
            pl.BlockSpec((2, D_MODEL, LANES), lambda i: (0, 0, 0)),
        ],
        out_specs=[
            pl.BlockSpec((tm, D_MODEL), row),
            pl.BlockSpec((tm, D_MODEL), row),
            slab, slab, slab, slab,
        ],
        out_shape=[
            jax.ShapeDtypeStruct((t, D_MODEL), F32),
            jax.ShapeDtypeStruct((t, D_MODEL), BF16),
            jax.ShapeDtypeStruct((t, LANES), jnp.int32),
            jax.ShapeDtypeStruct((t, LANES), jnp.int32),
            jax.ShapeDtypeStruct((t, LANES), F32),
            jax.ShapeDtypeStruct((t, LANES), F32),
        ],
        compiler_params=_cp(("parallel",)),
        name="out_c_router",
    )(o2d, r2d, x2d, onorm_row, bd, w_out, mnorm_row, router2)


def _row_copy(src_ref, src_i, dst_ref, dst_i, sem):
    return pltpu.make_async_copy(src_ref.at[src_i], dst_ref.at[dst_i], sem)


def _dispatch_kernel(dest_ref, x_ref, init_ref, xs_ref, sem):
    del init_ref
    n = x_ref.shape[0]

    def issue(r, _):
        for k in range(TOP_K):
            _row_copy(x_ref, r, xs_ref, dest_ref[0, 0, TOP_K * r + k], sem).start()
        return 0

    lax.fori_loop(0, n, issue, 0)

    def drain(r, _):
        for k in range(TOP_K):
            _row_copy(x_ref, r, xs_ref, 0, sem).wait()
        return 0

    lax.fori_loop(0, n, drain, 0)


def _moe_dispatch(dest3, xn3, n_slot):
    t = xn3.shape[0]
    tm = MOVE_TILE
    zeros = jnp.zeros((n_slot, 8, LANES), BF16)
    return pl.pallas_call(
        _dispatch_kernel,
        grid=(t // tm,),
        in_specs=[
            pl.BlockSpec((1, 1, TOP_K * tm), lambda i: (i, 0, 0), memory_space=pltpu.SMEM),
            pl.BlockSpec((tm, 8, LANES), lambda i: (i, 0, 0)),
            pl.BlockSpec(memory_space=pl.ANY),
        ],
        out_specs=pl.BlockSpec(memory_space=pl.ANY),
        out_shape=jax.ShapeDtypeStruct((n_slot, 8, LANES), BF16),
        scratch_shapes=[pltpu.SemaphoreType.DMA(())],
        input_output_aliases={2: 0},
        compiler_params=_cp(("arbitrary",), has_side_effects=True),
        name="moe_dispatch",
    )(dest3, xn3, zeros)


def _combine_kernel(dest_ref, ys_ref, x_ref, w1_ref, w2_ref, o_ref, ya_ref, yb_ref, sem_a, sem_b):
    n = x_ref.shape[0]

    def row(ref, i):
        return ref.at[pl.ds(i, 1), :]

    def issue(r, _):
        pltpu.make_async_copy(row(ys_ref, dest_ref[0, 0, TOP_K * r]), row(ya_ref, r), sem_a).start()
        pltpu.make_async_copy(row(ys_ref, dest_ref[0, 0, TOP_K * r + 1]), row(yb_ref, r), sem_b).start()
        return 0

    lax.fori_loop(0, n, issue, 0)

    def drain(r, _):
        pltpu.make_async_copy(row(ys_ref, 0), row(ya_ref, r), sem_a).wait()
        pltpu.make_async_copy(row(ys_ref, 0), row(yb_ref, r), sem_b).wait()
        return 0

    lax.fori_loop(0, n, drain, 0)
    reps = D_MODEL // LANES
    w1 = jnp.concatenate([w1_ref[...]] * reps, axis=1)
    w2 = jnp.concatenate([w2_ref[...]] * reps, axis=1)
    o_ref[...] = x_ref[...] + w1 * ya_ref[...] + w2 * yb_ref[...]


def _moe_combine(dest3, ys, x2d, w1b, w2b):
    t = x2d.shape[0]
    tm = MOVE_TILE
    row = lambda i: (i, 0)
    return pl.pallas_call(
        _combine_kernel,
        grid=(t // tm,),
        in_specs=[
            pl.BlockSpec((1, 1, TOP_K * tm), lambda i: (i, 0, 0), memory_space=pltpu.SMEM),
            pl.BlockSpec(memory_space=pl.ANY),
            pl.BlockSpec((tm, D_MODEL), row),
            pl.BlockSpec((tm, LANES), row),
            pl.BlockSpec((tm, LANES), row),
        ],
        out_specs=pl.BlockSpec((tm, D_MODEL), row),
        out_shape=jax.ShapeDtypeStruct((t, D_MODEL), F32),
        scratch_shapes=[
            pltpu.VMEM((tm, D_MODEL), F32),
            pltpu.VMEM((tm, D_MODEL), F32),
            pltpu.SemaphoreType.DMA(()),
            pltpu.SemaphoreType.DMA(()),
        ],
        compiler_params=_cp(("arbitrary",)),
        name="moe_combine",
    )(dest3, ys, x2d, w1b, w2b)


def _nsa_head_perm():
    order = []
    for t in range(4):
        order += [t, 4 + t]
    cols = np.concatenate([np.arange(h * HEAD_DIM, (h + 1) * HEAD_DIM) for h in order])
    return order, cols


def _layer_even(x, a_norm, a_w_in, a_q_norm, a_k_norm, a_pe_k, a_pe_v, ck1, ck2, cv1, cv2,
                gate_bias, fq_norm, fk_norm, f_bias, a_w_out, f_norm, f_wg, f_wu, f_wd):
    b, s, d = x.shape
    t = b * s
    x2d = x.reshape(t, d)
    order, qcols = _nsa_head_perm()

    def head_tiles(cols0):
        wq = a_w_in[:, cols0:cols0 + 512].reshape(d, FOX_HEADS, HEAD_DIM)
        return jnp.pad(wq, ((0, 0), (0, 0), (0, LANES - HEAD_DIM))).reshape(d, FOX_HEADS * LANES)

    w = jnp.concatenate([
        jnp.take(a_w_in, jnp.asarray(qcols), axis=1),
        a_w_in[:, 512:1280],
        head_tiles(1304),
        head_tiles(1816),
        a_w_in[:, 2328:2840],
        a_w_in[:, 1280:1304],
        a_w_in[:, 2840:2848],
        jnp.zeros((d, LANES - 32), F32),
    ], axis=1).astype(BF16)

    tile2 = lambda g: jnp.concatenate([g, g])
    lo_only = lambda g: jnp.concatenate([g, jnp.zeros((LANES - HEAD_DIM,), F32)])
    q_bias_lanes = jnp.zeros((LANES,), F32).at[FOX_BIAS_LANE:FOX_BIAS_LANE + 3].set(-1.0)
    head_gains = jnp.stack([tile2(a_q_norm), tile2(a_k_norm), lo_only(fq_norm), lo_only(fk_norm), q_bias_lanes]
                           + [jnp.zeros((LANES,), F32)] * 3)
    cos, sin = _rope_tables(np.arange(s))
    qa, kv6, fox, gf = _proj_a(x2d, a_norm.reshape(1, d), w, head_gains, cos, sin, s)

    fox3 = fox.reshape(b, s, FOX_SLAB)
    bias_row = jnp.zeros((1, LANES), F32).at[0, 24:32].set(f_bias)
    cum, k_aug = _fox_cumlog(gf.reshape(b, s, LANES), bias_row, fox3)
    cum_row = cum[:, :, 24:32].transpose(0, 2, 1)
    o_b = _fox_attention_t(fox3, k_aug, _values_t(fox3[:, :, 2048:2560]), cum_row).transpose(0, 2, 1)

    ncp = s // CMP_STRIDE
    n_cmp = (s - CMP_LEN) // CMP_STRIDE + 1
    n_slc = s // SLC_BLOCK
    kv3 = kv6.reshape(b, s, 768)
    kx = kv3[:, :, 0:128].reshape(b, ncp, CMP_STRIDE * LANES)
    vx = kv3[:, :, 128:256].reshape(b, ncp, CMP_STRIDE * LANES)

    def pe_rows(pe):
        p2 = jnp.concatenate([pe, pe], axis=1)
        return p2.reshape(2, CMP_STRIDE * LANES)

    def w1_blocks(w1):
        w4 = w1.reshape(2, CMP_STRIDE, HEAD_DIM, HEAD_DIM)
        z = jnp.zeros_like(w4)
        top = jnp.concatenate([w4, z], axis=-1)
        bot = jnp.concatenate([z, w4], axis=-1)
        return jnp.stack([top, bot], axis=2).reshape(2, CMP_STRIDE * LANES, LANES).astype(BF16)

    def w2_block(w2):
        z = jnp.zeros_like(w2)
        return jnp.concatenate([jnp.concatenate([w2, z], 1), jnp.concatenate([z, w2], 1)], 0).astype(BF16)

    ccos, csin = _rope_tables(np.arange(ncp) * CMP_STRIDE + CMP_LEN - 1)
    kcmp, vcmp = _nsa_compress(kx, vx, pe_rows(a_pe_k), pe_rows(a_pe_v), w1_blocks(ck1), w2_block(ck2),
                               w1_blocks(cv1), w2_block(cv2), tile2(a_k_norm).reshape(1, LANES), ccos, csin)

    c_start = np.arange(ncp) * CMP_STRIDE
    s_start = np.arange(64) * SLC_BLOCK
    ov = np.maximum(np.minimum(c_start[:, None] + CMP_LEN, s_start[None, :] + SLC_BLOCK)
                    - np.maximum(c_start[:, None], s_start[None, :]), 0).astype(np.float32) / CMP_LEN
    ov[n_cmp:, :] = 0.0
    ov[:, n_slc:] = 0.0
    qa3 = qa.reshape(b, s, 512)
    o_c, sb = _nsa_cmp_select(qa3, kcmp, vcmp, jnp.asarray(ov.T, BF16), n_cmp, n_slc)

    onehot = jnp.asarray((np.arange(s)[:, None] // SLC_BLOCK) == np.arange(64)[None, :], BF16)
    onehot = jnp.broadcast_to(onehot[None], (b, s, 64))
    ksl = kv3[:, :, 256:384]
    k_aug0 = jnp.concatenate([ksl[:, :, :64], onehot], axis=-1)
    k_aug1 = jnp.concatenate([onehot, ksl[:, :, 64:]], axis=-1)
    o_s = _nsa_selected_t(qa3, sb, k_aug0, k_aug1, _values_t(kv3[:, :, 384:512]))

    o_w = _nsa_window(qa3, kv3[:, :, 512:640], kv3[:, :, 640:768])

    gexp = np.zeros((3, LANES, 512), np.float32)
    for tile_i in range(4):
        for half, h in enumerate((tile_i, 4 + tile_i)):
            for j in range(3):
                gexp[j, h * 3 + j, tile_i * LANES + half * HEAD_DIM:tile_i * LANES + (half + 1) * HEAD_DIM] = 1.0
    gate_bias_row = jnp.zeros((1, LANES), F32).at[0, 0:24].set(gate_bias)
    w_out = jnp.concatenate([jnp.take(a_w_out[:512], jnp.asarray(qcols), axis=0), a_w_out[512:]], 0).astype(BF16)
    x1 = _out_a(o_c.reshape(t, 512), o_s.reshape(t, 512), o_w.reshape(t, 512), o_b.reshape(t, 512),
                gf, gate_bias_row, jnp.asarray(gexp, BF16), x2d, w_out)

    x2 = _ffn_dense(x1, f_norm.reshape(1, d), f_wg.astype(BF16), f_wu.astype(BF16), f_wd.astype(BF16))
    return x2.reshape(b, s, d)


def _layer_odd(x, c_norm, c_w_in, c_w_a2, c_b_a, c_o_norm, c_w_out, m_norm, m_router, m_wg, m_wu, m_wd):
    b, s, d = x.shape
    t = b * s
    x2d = x.reshape(t, d)
    w = jnp.pad(c_w_in, ((0, 0), (0, 3200 - c_w_in.shape[1]))).astype(BF16)
    wa = jnp.pad(c_w_a2, ((0, LANES - GLA_RANK), (0, 0))).astype(BF16)
    qk, v, r, la = _proj_c(x2d, c_norm.reshape(1, d), w, wa, c_b_a.reshape(1, GLA_DK))
    o = _gla(qk.reshape(b, s, 1024), v.reshape(b, s, 1024), la.reshape(b, s, GLA_DK))

    rt = jnp.pad(m_router, ((0, 0), (0, LANES - N_EXPERTS)))
    rt_hi = rt.astype(BF16)
    rt_lo = (rt - rt_hi.astype(F32)).astype(BF16)
    x3, xn, e1, e2, w1b, w2b = _out_c(o.reshape(t, GLA_DV), r, x2d, c_o_norm.reshape(1, -1),
                                      c_w_out.astype(BF16), m_norm.reshape(1, d), jnp.stack([rt_hi, rt_lo]))

    tm = EXPERT_TILE
    n_slot = TOP_K * t + N_EXPERTS * tm
    n_tiles = n_slot // tm
    flat_e = jnp.stack([e1[:, 0], e2[:, 0]], axis=1).reshape(-1)
    onehot = (flat_e[:, None] == jnp.arange(N_EXPERTS)[None, :]).astype(jnp.int32)
    csum = jnp.cumsum(onehot, axis=0)
    rank = jnp.sum(csum * onehot, axis=1) - 1
    counts = csum[-1]
    padded = ((counts + tm - 1) // tm) * tm
    ends = jnp.cumsum(padded)
    offs = ends - padded
    dest = (jnp.sum(onehot * offs[None, :], axis=1) + rank).astype(jnp.int32)
    n_used = (ends[-1] // tm).astype(jnp.int32)
    tile_start = jnp.arange(n_tiles, dtype=jnp.int32) * tm
    tile_e = jnp.sum((tile_start[:, None] >= ends[None, :]).astype(jnp.int32), axis=1)
    last_e = jnp.sum((((n_used - 1) * tm) >= ends).astype(jnp.int32))
    tile_e = jnp.where(jnp.arange(n_tiles) < n_used, tile_e, last_e).astype(jnp.int32)
    dest3 = dest.reshape(t // MOVE_TILE, 1, TOP_K * MOVE_TILE)

    xs = _moe_dispatch(dest3, xn.reshape(t, 8, LANES), n_slot)
    ys = _ffn_experts(tile_e, n_used.reshape(1), xs.reshape(n_slot, D_MODEL),
                      m_wg.astype(BF16), m_wu.astype(BF16), m_wd.astype(BF16))
    out = _moe_combine(dest3, ys, x3, w1b, w2b)
    return out.reshape(b, s, d)


def kernel(x, a_norm, a_w_in, a_q_norm, a_k_norm, a_pe_k, a_pe_v, a_cmp_k_w1, a_cmp_k_w2, a_cmp_v_w1, a_cmp_v_w2, a_gate_bias, a_fox_q_norm, a_fox_k_norm, a_fox_f_bias, a_w_out, f_norm, f_w_gate, f_w_up, f_w_down, c_norm, c_w_in, c_w_a2, c_b_a, c_o_norm, c_w_out, m_norm, m_router, m_w_gate, m_w_up, m_w_down):
    x = _layer_even(x, a_norm[0], a_w_in[0], a_q_norm[0], a_k_norm[0], a_pe_k[0], a_pe_v[0],
                    a_cmp_k_w1[0], a_cmp_k_w2[0], a_cmp_v_w1[0], a_cmp_v_w2[0], a_gate_bias[0],
                    a_fox_q_norm[0], a_fox_k_norm[0], a_fox_f_bias[0], a_w_out[0],
                    f_norm[0], f_w_gate[0], f_w_up[0], f_w_down[0])
    x = _layer_odd(x, c_norm[0], c_w_in[0], c_w_a2[0], c_b_a[0], c_o_norm[0], c_w_out[0],
                   m_norm[0], m_router[0], m_w_gate[0], m_w_up[0], m_w_down[0])
    return x
```

```python
import functools

import numpy as np
import jax
import jax.numpy as jnp
from jax import lax
from jax.experimental import pallas as pl
from jax.experimental.pallas import tpu as pltpu

F32 = jnp.float32
BF16 = jnp.bfloat16

D_MODEL = 1024
HEAD_DIM = 64
ROPE_DIM = 16
ROPE_THETA = 500000.0
NORM_EPS = 1e-6
NEG = -1e30
FORCE_SCORE = 1e6

NSA_HEADS = 8
CMP_LEN = 32
CMP_STRIDE = 16
SLC_BLOCK = 64
SLC_TOPK = 16
WINDOW = 512
FOX_HEADS = 8

GLA_HEADS = 4
GLA_DK = 512
GLA_DV = 1024
GLA_RANK = 16
GLA_TAU = 16.0
GLA_CHUNK = 64
GLA_SUB = 16

D_FF = 2816
N_EXPERTS = 8
TOP_K = 2

LANES = 128
FF_CHUNK = 256
VMEM_LIMIT = 56 * 1024 * 1024

ROW_TILE = 512
ATT_Q = 128
FOX_Q = 512
FOX_K = 256
SLC_K = 256
EXPERT_TILE = 512

NT_DIMS = (((1,), (1,)), ((), ()))


def _cp(sem, **kw):
    return pltpu.CompilerParams(dimension_semantics=sem, vmem_limit_bytes=VMEM_LIMIT, **kw)


def _dot(a, b):
    return jnp.dot(a, b, preferred_element_type=F32)


def _dot_nt(a, b):
    return lax.dot_general(a, b, NT_DIMS, preferred_element_type=F32)


def _split2(x):
    hi = x.astype(BF16)
    lo = (x - hi.astype(F32)).astype(BF16)
    return hi, lo


def _split3(x):
    hi = x.astype(BF16)
    r = x - hi.astype(F32)
    mid = r.astype(BF16)
    lo = (r - mid.astype(F32)).astype(BF16)
    return hi, mid, lo


def _rms_rows(x, gain):
    return x * lax.rsqrt(jnp.mean(x * x, axis=-1, keepdims=True) + NORM_EPS) * gain


def _log_sigmoid(x):
    return jnp.minimum(x, 0.0) - jnp.log1p(jnp.exp(-jnp.abs(x)))


def _sigmoid(x):
    return 1.0 / (1.0 + jnp.exp(-x))


def _seg_norm(y, gain, bd):
    ms = _dot((y * y).astype(BF16), bd)
    return y * lax.rsqrt(ms + NORM_EPS) * gain


def _rope(y, cos, sin, perm):
    return y * cos + _dot(y.astype(BF16), perm) * sin


def _block_diag_mean(width, seg):
    i = np.arange(width)
    return ((i[:, None] // seg) == (i[None, :] // seg)).astype(np.float32) / seg


def _rope_perm():
    p = np.zeros((LANES, LANES), np.float32)
    half = ROPE_DIM // 2
    for j in range(LANES):
        d = j % HEAD_DIM
        if d < half:
            p[j + half, j] = -1.0
        elif d < ROPE_DIM:
            p[j - half, j] = 1.0
    return p


def _rope_tables(pos):
    inv = ROPE_THETA ** (-np.arange(0, ROPE_DIM, 2, dtype=np.float64) / ROPE_DIM)
    ang = pos.astype(np.float64)[:, None] * inv[None, :]
    cos8, sin8 = np.cos(ang), np.sin(ang)
    n = pos.shape[0]
    ones = np.ones((n, HEAD_DIM - ROPE_DIM))
    zeros = np.zeros((n, HEAD_DIM - ROPE_DIM))
    c = np.concatenate([cos8, cos8, ones], axis=1)
    s = np.concatenate([sin8, sin8, zeros], axis=1)
    return (jnp.asarray(np.concatenate([c, c], axis=1), F32),
            jnp.asarray(np.concatenate([s, s], axis=1), F32))


PROJ_A_COLS = 3968
FOX_SLAB = 2560


def _proj_a_kernel(x_ref, g_ref, w_ref, hg_ref, cos_ref, sin_ref, bd_ref, bdlo_ref, perm_ref,
                   qa_ref, kv_ref, fox_ref, gf_ref):
    xn = _rms_rows(x_ref[...], g_ref[...]).astype(BF16)
    bd = bd_ref[...]
    perm = perm_ref[...]
    cos = cos_ref[...]
    sin = sin_ref[...]

    def tile(acc, i):
        return acc[:, i * LANES:(i + 1) * LANES]

    acc = _dot(xn, w_ref[:, 0:512])
    for i in range(4):
        y = _rope(_seg_norm(tile(acc, i), hg_ref[0:1, :], bd), cos, sin, perm)
        qa_ref[:, i * LANES:(i + 1) * LANES] = (y * 0.125).astype(BF16)

    acc = _dot(xn, w_ref[:, 512:1280])
    for i in range(6):
        y = tile(acc, i)
        if i in (2, 4):
            y = _rope(_seg_norm(y, hg_ref[1:2, :], bd), cos, sin, perm)
        kv_ref[:, i * LANES:(i + 1) * LANES] = y.astype(BF16)

    bd_lo = bdlo_ref[...]
    for c in range(4):
        acc = _dot(xn, w_ref[:, 1280 + 512 * c:1280 + 512 * (c + 1)])
        for i in range(4):
            y = tile(acc, i)
            if c < 2:
                y = _seg_norm(y, hg_ref[2:3, :], bd_lo) * 0.125 + hg_ref[4:5, :]
            else:
                y = _seg_norm(y, hg_ref[3:4, :], bd_lo)
            fox_ref[:, 512 * c + i * LANES:512 * c + (i + 1) * LANES] = y.astype(BF16)
    fox_ref[:, 2048:2560] = _dot(xn, w_ref[:, 3328:3840]).astype(BF16)

    gf_ref[...] = _dot(xn, w_ref[:, 3840:3968])


def _proj_a(x2d, norm_g, w_perm, head_gains, cos, sin, seq):
    t = x2d.shape[0]
    tm = ROW_TILE
    n_pos = seq // tm
    bd = jnp.asarray(_block_diag_mean(LANES, HEAD_DIM), BF16)
    lo = np.zeros((LANES, LANES), np.float32)
    lo[:HEAD_DIM, :] = 1.0 / HEAD_DIM
    perm = jnp.asarray(_rope_perm(), BF16)
    full = lambda i: (0, 0)
    return pl.pallas_call(
        _proj_a_kernel,
        grid=(t // tm,),
        in_specs=[
            pl.BlockSpec((tm, D_MODEL), lambda i: (i, 0)),
            pl.BlockSpec((1, D_MODEL), full),
            pl.BlockSpec((D_MODEL, PROJ_A_COLS), full),
            pl.BlockSpec((8, LANES), full),
            pl.BlockSpec((tm, LANES), lambda i: (i % n_pos, 0)),
            pl.BlockSpec((tm, LANES), lambda i: (i % n_pos, 0)),
            pl.BlockSpec((LANES, LANES), full),
            pl.BlockSpec((LANES, LANES), full),
            pl.BlockSpec((LANES, LANES), full),
        ],
        out_specs=[
            pl.BlockSpec((tm, 512), lambda i: (i, 0)),
            pl.BlockSpec((tm, 768), lambda i: (i, 0)),
            pl.BlockSpec((tm, FOX_SLAB), lambda i: (i, 0)),
            pl.BlockSpec((tm, LANES), lambda i: (i, 0)),
        ],
        out_shape=[
            jax.ShapeDtypeStruct((t, 512), BF16),
            jax.ShapeDtypeStruct((t, 768), BF16),
            jax.ShapeDtypeStruct((t, FOX_SLAB), BF16),
            jax.ShapeDtypeStruct((t, LANES), F32),
        ],
        compiler_params=_cp(("parallel",)),
        name="proj_a",
    )(x2d, norm_g, w_perm, head_gains, cos, sin, bd, jnp.asarray(lo, BF16), perm)


CUM_CHUNK = 256
FOX_BIAS_LANE = 64


def _cum_kernel(gf_ref, bias_ref, tri_ref, k_ref, place_ref, out_ref, ka_ref):
    tri = tri_ref[...]
    n_chunk = gf_ref.shape[0] // CUM_CHUNK
    carry = jnp.zeros((1, LANES), F32)
    for c in range(n_chunk):
        rows = slice(c * CUM_CHUNK, (c + 1) * CUM_CHUNK)
        lf = _log_sigmoid(gf_ref[rows, :] + bias_ref[...])
        hi, mid, lo = _split3(lf)
        cs = _dot(tri, hi) + _dot(tri, mid) + _dot(tri, lo) + carry
        out_ref[rows, :] = cs
        carry = cs[CUM_CHUNK - 1:CUM_CHUNK, :]
        c_hi, c_mid, c_lo = _split3(cs)
        placed = _dot(c_hi, place_ref[0]) + _dot(c_mid, place_ref[1]) + _dot(c_lo, place_ref[2])
        ka_ref[rows, :] = (k_ref[rows, :].astype(F32) + placed).astype(BF16)


def _fox_cumlog(gf3, bias_row, fox3):
    b, s, _ = gf3.shape
    tri = jnp.asarray(np.tril(np.ones((CUM_CHUNK, CUM_CHUNK), np.float32)), BF16)
    place = np.zeros((3, LANES, FOX_HEADS * LANES), np.float32)
    for h in range(FOX_HEADS):
        for piece in range(3):
            place[piece, 24 + h, h * LANES + FOX_BIAS_LANE + piece] = 1.0
    return pl.pallas_call(
        _cum_kernel,
        grid=(b,),
        in_specs=[
            pl.BlockSpec((None, s, LANES), lambda i: (i, 0, 0)),
            pl.BlockSpec((1, LANES), lambda i: (0, 0)),
            pl.BlockSpec((CUM_CHUNK, CUM_CHUNK), lambda i: (0, 0)),
            pl.BlockSpec((None, s, FOX_HEADS * LANES), lambda i: (i, 0, 1)),
            pl.BlockSpec((3, LANES, FOX_HEADS * LANES), lambda i: (0, 0, 0)),
        ],
        out_specs=[
            pl.BlockSpec((None, s, LANES), lambda i: (i, 0, 0)),
            pl.BlockSpec((None, s, FOX_HEADS * LANES), lambda i: (i, 0, 0)),
        ],
        out_shape=[
            jax.ShapeDtypeStruct((b, s, LANES), F32),
            jax.ShapeDtypeStruct((b, s, FOX_HEADS * LANES), BF16),
        ],
        compiler_params=_cp(("parallel",)),
        name="fox_cumlog",
    )(gf3, bias_row, tri, fox3, jnp.asarray(place, BF16))


def _gelu_tanh(x):
    return 0.5 * x * (1.0 + jnp.tanh(0.7978845608028654 * (x + 0.044715 * x * x * x)))


def _cmp_kernel(kx_ref, vx_ref, pek_ref, pev_ref, w1k_ref, w2k_ref, w1v_ref, w2v_ref,
                kg_ref, cos_ref, sin_ref, bd_ref, perm_ref, kc_ref, vc_ref):
    ncp = kx_ref.shape[0]

    def compress(x_ref, pe_ref, w1_ref, w2_ref):
        x = x_ref[...].astype(F32)
        first = _dot((x + pe_ref[0:1, :]).astype(BF16), w1_ref[0])
        second = _dot((x + pe_ref[1:2, :]).astype(BF16), w1_ref[1])
        pre = first + pltpu.roll(second, ncp - 1, 0)
        return _dot(_gelu_tanh(pre).astype(BF16), w2_ref[...])

    k = compress(kx_ref, pek_ref, w1k_ref, w2k_ref)
    k = _rope(_seg_norm(k, kg_ref[...], bd_ref[...]), cos_ref[...], sin_ref[...], perm_ref[...])
    kc_ref[...] = k.astype(BF16)
    vc_ref[...] = compress(vx_ref, pev_ref, w1v_ref, w2v_ref).astype(BF16)


def _nsa_compress(kx, vx, pek, pev, w1k, w2k, w1v, w2v, kgain, ccos, csin):
    b, ncp, width = kx.shape
    bd = jnp.asarray(_block_diag_mean(LANES, HEAD_DIM), BF16)
    perm = jnp.asarray(_rope_perm(), BF16)
    c2 = lambda i: (0, 0)
    c3 = lambda i: (0, 0, 0)
    return pl.pallas_call(
        _cmp_kernel,
        grid=(b,),
        in_specs=[
            pl.BlockSpec((None, ncp, width), lambda i: (i, 0, 0)),
            pl.BlockSpec((None, ncp, width), lambda i: (i, 0, 0)),
            pl.BlockSpec((2, width), c2),
            pl.BlockSpec((2, width), c2),
            pl.BlockSpec((2, width, LANES), c3),
            pl.BlockSpec((LANES, LANES), c2),
            pl.BlockSpec((2, width, LANES), c3),
            pl.BlockSpec((LANES, LANES), c2),
            pl.BlockSpec((1, LANES), c2),
            pl.BlockSpec((ncp, LANES), c2),
            pl.BlockSpec((ncp, LANES), c2),
            pl.BlockSpec((LANES, LANES), c2),
            pl.BlockSpec((LANES, LANES), c2),
        ],
        out_specs=[
            pl.BlockSpec((None, ncp, LANES), lambda i: (i, 0, 0)),
            pl.BlockSpec((None, ncp, LANES), lambda i: (i, 0, 0)),
        ],
        out_shape=[
            jax.ShapeDtypeStruct((b, ncp, LANES), BF16),
            jax.ShapeDtypeStruct((b, ncp, LANES), BF16),
        ],
        compiler_params=_cp(("parallel",)),
        name="nsa_compress",
    )(kx, vx, pek, pev, w1k, w2k, w1v, w2v, kgain, ccos, csin, bd, perm)


def _nsa_cmp_kernel(q_ref, kc_ref, vc_ref, ovt_ref, eye_ref, oc_ref, sb_ref, *, n_cmp, n_slc):
    tq = ATT_Q
    q0 = pl.program_id(1) * tq
    kc = kc_ref[...]
    vc = vc_ref[...]
    ncp = kc.shape[0]
    ovt = ovt_ref[...]
    lo_half = lax.broadcasted_iota(jnp.int32, (tq, LANES), 1) < HEAD_DIM

    n_lane = lax.broadcasted_iota(jnp.int32, (tq, ncp), 1)
    t_sub = q0 + lax.broadcasted_iota(jnp.int32, (tq, ncp), 0)
    mask = (n_lane * CMP_STRIDE + (CMP_LEN - 1) <= t_sub) & (n_lane < n_cmp)
    n_sub = lax.broadcasted_iota(jnp.int32, (ncp, tq), 0)
    t_lane = q0 + lax.broadcasted_iota(jnp.int32, (ncp, tq), 1)
    mask_t = (n_sub * CMP_STRIDE + (CMP_LEN - 1) <= t_lane) & (n_sub < n_cmp)

    imp_t = [jnp.zeros((64, tq), F32), jnp.zeros((64, tq), F32)]
    for t in range(4):
        qt = q_ref[:, t * LANES:(t + 1) * LANES]
        outs = []
        for g in range(2):
            qm = jnp.where(lo_half if g == 0 else jnp.logical_not(lo_half), qt, jnp.zeros_like(qt))
            s = jnp.where(mask, _dot_nt(qm, kc), NEG)
            p = jnp.where(mask, jnp.exp(s - jnp.max(s, axis=-1, keepdims=True)), 0.0)
            p = p / jnp.maximum(jnp.sum(p, axis=-1, keepdims=True), 1e-30)
            outs.append(_dot(p.astype(BF16), vc))
            st = jnp.where(mask_t, _dot_nt(kc, qm), NEG)
            pt = jnp.where(mask_t, jnp.exp(st - jnp.max(st, axis=0, keepdims=True)), 0.0)
            pt = pt / jnp.maximum(jnp.sum(pt, axis=0, keepdims=True), 1e-30)
            hi, lo = _split2(pt)
            imp_t[g] = imp_t[g] + _dot(ovt, hi) + _dot(ovt, lo)
        oc_ref[:, t * LANES:(t + 1) * LANES] = jnp.where(lo_half, outs[0], outs[1]).astype(BF16)

    blk = lax.broadcasted_iota(jnp.int32, (64, tq), 0)
    cur = (q0 + lax.broadcasted_iota(jnp.int32, (64, tq), 1)) >> 6
    forced = (blk == 0) | (blk == cur) | (blk == cur - 1)
    future = blk > cur
    exists = blk < n_slc
    biases = []
    for g in range(2):
        v = jnp.where(forced, FORCE_SCORE, jnp.where(future, -FORCE_SCORE, imp_t[g]))
        v = jnp.where(exists, v, -3e38)
        sel = jnp.zeros((64, tq), jnp.int32)
        for _ in range(min(SLC_TOPK, n_slc)):
            m = jnp.max(v, axis=0, keepdims=True)
            first = jnp.min(jnp.where(v == m, blk, 64), axis=0, keepdims=True)
            hit = blk == first
            sel = jnp.where(hit, 1, sel)
            v = jnp.where(hit, -3.2e38, v)
        keep = (sel == 1) & jnp.logical_not(future) & exists
        biases.append(jnp.where(keep, 0.0, NEG).astype(BF16))
    sb_t = jnp.concatenate([biases[1], biases[0]], axis=0)
    sb_ref[...] = _dot_nt(eye_ref[...], sb_t).astype(BF16)


def _nsa_cmp_select(qa3, kcmp, vcmp, ovt, n_cmp, n_slc):
    b, s, _ = qa3.shape
    ncp = kcmp.shape[1]
    eye = jnp.asarray(np.eye(ATT_Q, dtype=np.float32), BF16)
    kern = functools.partial(_nsa_cmp_kernel, n_cmp=n_cmp, n_slc=n_slc)
    return pl.pallas_call(
        kern,
        grid=(b, s // ATT_Q),
        in_specs=[
            pl.BlockSpec((None, ATT_Q, 512), lambda i, j: (i, j, 0)),
            pl.BlockSpec((None, ncp, LANES), lambda i, j: (i, 0, 0)),
            pl.BlockSpec((None, ncp, LANES), lambda i, j: (i, 0, 0)),
            pl.BlockSpec((64, ncp), lambda i, j: (0, 0)),
            pl.BlockSpec((ATT_Q, ATT_Q), lambda i, j: (0, 0)),
        ],
        out_specs=[
            pl.BlockSpec((None, ATT_Q, 512), lambda i, j: (i, j, 0)),
            pl.BlockSpec((None, ATT_Q, LANES), lambda i, j: (i, j, 0)),
        ],
        out_shape=[
            jax.ShapeDtypeStruct((b, s, 512), BF16),
            jax.ShapeDtypeStruct((b, s, LANES), BF16),
        ],
        compiler_params=_cp(("parallel", "parallel")),
        name="nsa_cmp_select",
    )(qa3, kcmp, vcmp, ovt, eye)


V_ROWS = 80


def _flash_step_t(carry, s_t, v_t, *, query_bias=None, mask=None):
    m, acc = carry
    if mask is not None:
        s_t = jnp.where(mask, s_t, NEG)
    col_max = jnp.max(s_t, axis=0, keepdims=True)
    if query_bias is not None:
        m_new = jnp.maximum(m, col_max + query_bias)
        shift = m_new - query_bias
    else:
        m_new = jnp.maximum(m, col_max)
        shift = m_new
    p = jnp.exp(s_t - shift).astype(BF16)
    acc = jnp.exp(m - m_new) * acc + _dot(v_t, p)
    return m_new, acc


def _flash_init_t(queries):
    return (jnp.full((1, queries), NEG, F32), jnp.zeros((V_ROWS, queries), F32))


def _values_t(v):
    b, s, w = v.shape
    heads = w // HEAD_DIM
    vt = v.reshape(b, s, heads, HEAD_DIM).transpose(0, 2, 3, 1)
    ones = jnp.ones((b, heads, 1, s), v.dtype)
    pad = jnp.zeros((b, heads, V_ROWS - HEAD_DIM - 1, s), v.dtype)
    return jnp.concatenate([vt, ones, pad], axis=2)


def _nsa_slc_t_kernel(q_ref, sb_ref, k0_ref, k1_ref, v_ref, eye_ref, o_ref, sa_ref, sc_ref):
    tq, tk = ATT_Q, SLC_K
    q0 = pl.program_id(1) * tq
    sb = sb_ref[...]
    lo_half = lax.broadcasted_iota(jnp.int32, (tq, LANES), 1) < HEAD_DIM
    cols = 4 * tq
    n_before = q0 // tk
    k_refs = (k0_ref, k1_ref)
    slots = (sa_ref, sc_ref)

    qq = []
    for g in range(2):
        parts = []
        for t in range(4):
            qt = q_ref[:, t * LANES:(t + 1) * LANES]
            parts.append(jnp.where(lo_half, qt, sb) if g == 0 else jnp.where(lo_half, sb, qt))
        qq.append(jnp.concatenate(parts, axis=0))

    def issue(j, slot):
        off = pl.multiple_of(j * tk, tk)
        for g in range(2):
            slots[slot][g] = _dot_nt(k_refs[g][pl.ds(off, tk), :], qq[g])

    def consume(j, carry, slot, mask=None):
        off = pl.multiple_of(j * tk, tk)
        return tuple(_flash_step_t(carry[g], slots[slot][g], v_ref[g, :, pl.ds(off, tk)], mask=mask)
                     for g in range(2))

    def step(jj, carry):
        j = 2 * jj
        issue(j + 1, 1)
        carry = consume(j, carry, 0)
        issue(j + 2, 0)
        return consume(j + 1, carry, 1)

    issue(0, 0)
    carry = lax.fori_loop(0, n_before // 2, step, (_flash_init_t(cols), _flash_init_t(cols)))

    q_in_chunk = q0 - n_before * tk
    query = lax.broadcasted_iota(jnp.int32, (tk, cols), 1) & (tq - 1)
    causal = lax.broadcasted_iota(jnp.int32, (tk, cols), 0) - query <= q_in_chunk

    def even_tail(carry):
        return consume(n_before, carry, 0, causal)

    def odd_tail(carry):
        issue(n_before, 1)
        carry = consume(n_before - 1, carry, 0)
        return consume(n_before, carry, 1, causal)

    carry = lax.cond((n_before & 1) == 0, even_tail, odd_tail, carry)

    outs = [(acc[0:HEAD_DIM] / acc[HEAD_DIM:HEAD_DIM + 1]).astype(BF16) for (_, acc) in carry]
    eye = eye_ref[...]
    for t in range(4):
        cs = slice(t * tq, (t + 1) * tq)
        both = jnp.concatenate([outs[0][:, cs], outs[1][:, cs]], axis=0)
        o_ref[:, t * LANES:(t + 1) * LANES] = _dot_nt(eye, both).astype(BF16)


def _nsa_selected_t(qa3, sb, k_aug0, k_aug1, v_t):
    b, s, _ = qa3.shape
    blk = lambda i, j: (i, j, 0)
    whole = lambda i, j: (i, 0, 0)
    eye = jnp.asarray(np.eye(ATT_Q, dtype=np.float32), BF16)
    return pl.pallas_call(
        _nsa_slc_t_kernel,
        grid=(b, s // ATT_Q),
        in_specs=[
            pl.BlockSpec((None, ATT_Q, 512), blk),
            pl.BlockSpec((None, ATT_Q, LANES), blk),
            pl.BlockSpec((None, s, LANES), whole),
            pl.BlockSpec((None, s, LANES), whole),
            pl.BlockSpec((None, 2, V_ROWS, s), lambda i, j: (i, 0, 0, 0)),
            pl.BlockSpec((ATT_Q, ATT_Q), lambda i, j: (0, 0)),
        ],
        out_specs=pl.BlockSpec((None, ATT_Q, 512), blk),
        out_shape=jax.ShapeDtypeStruct((b, s, 512), BF16),
        scratch_shapes=[pltpu.VMEM((2, SLC_K, 4 * ATT_Q), F32), pltpu.VMEM((2, SLC_K, 4 * ATT_Q), F32)],
        compiler_params=_cp(("parallel", "parallel")),
        name="nsa_selected",
    )(qa3, sb, k_aug0, k_aug1, v_t, eye)


WIN_STRIP = 32


def _nsa_win_kernel(q_ref, k_ref, v_ref, o_ref, p0_ref, p1_ref):
    tq = ATT_Q
    span = WINDOW + tq
    q0 = pl.program_id(1) * tq
    start = pl.multiple_of(jnp.maximum(q0 - WINDOW, 0), tq)
    k = k_ref[pl.ds(start, span), :]
    v = v_ref[pl.ds(start, span), :]
    rows = 4 * tq
    lo_half = lax.broadcasted_iota(jnp.int32, (tq, LANES), 1) < HEAD_DIM
    shape = (WIN_STRIP, span)
    behind = (q0 - start) + lax.broadcasted_iota(jnp.int32, shape, 0) - lax.broadcasted_iota(jnp.int32, shape, 1)
    outs = []
    for g, p_ref in enumerate((p0_ref, p1_ref)):
        keep = lo_half if g == 0 else jnp.logical_not(lo_half)
        parts = []
        for t in range(4):
            qt = q_ref[:, t * LANES:(t + 1) * LANES]
            parts.append(jnp.where(keep, qt, jnp.zeros_like(qt)))
        s = _dot_nt(jnp.concatenate(parts, axis=0), k)
        ls = []
        for r in range(rows // WIN_STRIP):
            rs = slice(r * WIN_STRIP, (r + 1) * WIN_STRIP)
            dist = behind + (r * WIN_STRIP) % tq
            t_s = jnp.where((dist >= 0) & (dist < WINDOW), s[rs], NEG)
            p = jnp.exp(t_s - jnp.max(t_s, axis=-1, keepdims=True))
            ls.append(jnp.sum(p, axis=-1, keepdims=True))
            p_ref[rs, :] = p.astype(BF16)
        outs.append(_dot(p_ref[...], v) / jnp.concatenate(ls, axis=0))
    for t in range(4):
        rs = slice(t * tq, (t + 1) * tq)
        o_ref[:, t * LANES:(t + 1) * LANES] = jnp.where(lo_half, outs[0][rs], outs[1][rs]).astype(BF16)


def _nsa_window(qa3, k2, v2):
    b, s, _ = qa3.shape
    blk = lambda i, j: (i, j, 0)
    whole = lambda i, j: (i, 0, 0)
    return pl.pallas_call(
        _nsa_win_kernel,
        grid=(b, s // ATT_Q),
        in_specs=[
            pl.BlockSpec((None, ATT_Q, 512), blk),
            pl.BlockSpec((None, s, LANES), whole),
            pl.BlockSpec((None, s, LANES), whole),
        ],
        out_specs=pl.BlockSpec((None, ATT_Q, 512), blk),
        out_shape=jax.ShapeDtypeStruct((b, s, 512), BF16),
        scratch_shapes=[pltpu.VMEM((4 * ATT_Q, WINDOW + ATT_Q), BF16),
                        pltpu.VMEM((4 * ATT_Q, WINDOW + ATT_Q), BF16)],
        compiler_params=_cp(("parallel", "parallel")),
        name="nsa_window",
    )(qa3, k2, v2)


def _fox_t_kernel(q_ref, k_ref, v_ref, cr_ref, o_ref, sa_ref, sb_ref):
    tq, tk = FOX_Q, FOX_K
    q0 = pl.program_id(1) * tq
    n_before = q0 // tk
    n_diag = tq // tk
    for pair in range(FOX_HEADS // 2):
        heads = (2 * pair, 2 * pair + 1)
        qs = tuple(q_ref[:, h * LANES:(h + 1) * LANES] for h in heads)
        cq = tuple(cr_ref[h:h + 1, pl.ds(pl.multiple_of(q0, tq), tq)] for h in heads)

        def scores(j, i, q_lo=0):
            off = pl.multiple_of(j * tk, tk)
            h = heads[i]
            return _dot_nt(k_ref[pl.ds(off, tk), h * LANES:(h + 1) * LANES], qs[i][q_lo:])

        slots = (sa_ref, sb_ref)

        def consume(j, carry, slot):
            off = pl.multiple_of(j * tk, tk)
            return tuple(_flash_step_t(carry[i], slots[slot][i], v_ref[heads[i], :, pl.ds(off, tk)],
                                       query_bias=cq[i]) for i in range(2))

        def step(jj, carry):
            j = 2 * jj
            for i in range(2):
                sb_ref[i] = scores(j + 1, i)
            carry = consume(j, carry, 0)
            for i in range(2):
                sa_ref[i] = scores(j + 2, i)
            return consume(j + 1, carry, 1)

        for i in range(2):
            sa_ref[i] = scores(0, i)
        carry = lax.fori_loop(0, n_before // 2, step, (_flash_init_t(tq), _flash_init_t(tq)))

        late = [[scores(n_before + d, i, d * tk) for i in range(2)] for d in range(1, n_diag)]
        for d in range(n_diag):
            q_lo = d * tk
            off = pl.multiple_of((n_before + d) * tk, tk)
            shape = (tk, tq - q_lo)
            mask = (lax.broadcasted_iota(jnp.int32, shape, 0) + (d * tk - q_lo)
                    <= lax.broadcasted_iota(jnp.int32, shape, 1))
            out = []
            for i in range(2):
                m, acc = carry[i]
                s_t = sa_ref[i] if d == 0 else late[d - 1][i]
                new = _flash_step_t((m[:, q_lo:], acc[:, q_lo:]), s_t, v_ref[heads[i], :, pl.ds(off, tk)],
                                    query_bias=cq[i][:, q_lo:], mask=mask)
                if q_lo:
                    new = tuple(jnp.concatenate([old[:, :q_lo], part], axis=1)
                                for old, part in zip((m, acc), new))
                out.append(new)
            carry = tuple(out)
        for i, h in enumerate(heads):
            acc = carry[i][1]
            o_ref[h * HEAD_DIM:(h + 1) * HEAD_DIM, :] = (
                acc[0:HEAD_DIM] / acc[HEAD_DIM:HEAD_DIM + 1]).astype(BF16)


def _fox_attention_t(fox3, k_aug, v_t, cum_row):
    b, s, _ = fox3.shape
    width = FOX_HEADS * LANES
    return pl.pallas_call(
        _fox_t_kernel,
        grid=(b, s // FOX_Q),
        in_specs=[
            pl.BlockSpec((None, FOX_Q, width), lambda i, j: (i, j, 0)),
            pl.BlockSpec((None, s, width), lambda i, j: (i, 0, 0)),
            pl.BlockSpec((None, FOX_HEADS, V_ROWS, s), lambda i, j: (i, 0, 0, 0)),
            pl.BlockSpec((None, FOX_HEADS, s), lambda i, j: (i, 0, 0)),
        ],
        out_specs=pl.BlockSpec((None, FOX_HEADS * HEAD_DIM, FOX_Q), lambda i, j: (i, 0, j)),
        out_shape=jax.ShapeDtypeStruct((b, FOX_HEADS * HEAD_DIM, s), BF16),
        scratch_shapes=[pltpu.VMEM((2, FOX_K, FOX_Q), F32), pltpu.VMEM((2, FOX_K, FOX_Q), F32)],
        compiler_params=_cp(("parallel", "parallel")),
        name="fox_attention",
    )(fox3, k_aug, v_t, cum_row)


def _out_a_kernel(oc_ref, os_ref, ow_ref, ob_ref, gf_ref, gb_ref, gx_ref, x_ref, w_ref, o_ref):
    gates = _sigmoid(gf_ref[...] + gb_ref[...])
    hi, lo = _split2(gates)
    o_a = None
    for j, br in enumerate((oc_ref, os_ref, ow_ref)):
        gexp = _dot(hi, gx_ref[j]) + _dot(lo, gx_ref[j])
        term = gexp * br[...].astype(F32)
        o_a = term if o_a is None else o_a + term
    y = _dot(o_a.astype(BF16), w_ref[0:512, :]) + _dot(ob_ref[...], w_ref[512:1024, :])
    o_ref[...] = x_ref[...] + y


def _out_a(oc, os_, ow, ob, gf, gate_bias_row, gate_expand, x2d, w_out_perm):
    t = x2d.shape[0]
    tm = ROW_TILE
    row = lambda i: (i, 0)
    c2 = lambda i: (0, 0)
    return pl.pallas_call(
        _out_a_kernel,
        grid=(t // tm,),
        in_specs=[
            pl.BlockSpec((tm, 512), row),
            pl.BlockSpec((tm, 512), row),
            pl.BlockSpec((tm, 512), row),
            pl.BlockSpec((tm, 512), row),
            pl.BlockSpec((tm, LANES), row),
            pl.BlockSpec((1, LANES), c2),
            pl.BlockSpec((3, LANES, 512), lambda i: (0, 0, 0)),
            pl.BlockSpec((tm, D_MODEL), row),
            pl.BlockSpec((D_MODEL, D_MODEL), c2),
        ],
        out_specs=pl.BlockSpec((tm, D_MODEL), row),
        out_shape=jax.ShapeDtypeStruct((t, D_MODEL), F32),
        compiler_params=_cp(("parallel",)),
        name="out_a",
    )(oc, os_, ow, ob, gf, gate_bias_row, gate_expand, x2d, w_out_perm)


def _swiglu_body(xn, wg_ref, wu_ref, wd_ref, h_ref):
    for c in range(D_FF // FF_CHUNK):
        cols = slice(c * FF_CHUNK, (c + 1) * FF_CHUNK)
        g = _dot(xn, wg_ref[:, cols])
        u = _dot(xn, wu_ref[:, cols])
        h_ref[:, cols] = (g * _sigmoid(g) * u).astype(BF16)
    return _dot(h_ref[...], wd_ref[...])


def _ffn_dense_kernel(x_ref, g_ref, wg_ref, wu_ref, wd_ref, o_ref, h_ref):
    x = x_ref[...]
    xn = _rms_rows(x, g_ref[...]).astype(BF16)
    o_ref[...] = x + _swiglu_body(xn, wg_ref, wu_ref, wd_ref, h_ref)


def _ffn_dense(x2d, norm_g, wg, wu, wd):
    t = x2d.shape[0]
    tm = ROW_TILE
    c2 = lambda i: (0, 0)
    return pl.pallas_call(
        _ffn_dense_kernel,
        grid=(t // tm,),
        in_specs=[
            pl.BlockSpec((tm, D_MODEL), lambda i: (i, 0)),
            pl.BlockSpec((1, D_MODEL), c2),
            pl.BlockSpec((D_MODEL, D_FF), c2),
            pl.BlockSpec((D_MODEL, D_FF), c2),
            pl.BlockSpec((D_FF, D_MODEL), c2),
        ],
        out_specs=pl.BlockSpec((tm, D_MODEL), lambda i: (i, 0)),
        out_shape=jax.ShapeDtypeStruct((t, D_MODEL), F32),
        scratch_shapes=[pltpu.VMEM((tm, D_FF), BF16)],
        compiler_params=_cp(("parallel",)),
        name="ffn_dense",
    )(x2d, norm_g, wg, wu, wd)


def _ffn_expert_kernel(te_ref, nu_ref, cnt_ref, src_now_ref, src_next_ref, dst_ref,
                       xn_hbm, wg_ref, wu_ref, wd_ref, y_hbm, xbuf, ybuf, h_ref, gsem, ssem):
    tm = EXPERT_TILE
    i = pl.program_id(0)
    n_used = nu_ref[0]
    slot = i & 1

    def gather(idx_ref, s):
        def body(r, _):
            pltpu.make_async_copy(xn_hbm.at[idx_ref[0, 0, r]], xbuf.at[s, r], gsem.at[s]).start()
            return 0
        lax.fori_loop(0, tm, body, 0, unroll=8)

    def wait_rows(copy_of_row, count):
        def body(r, _):
            copy_of_row(r).wait()
            return 0
        lax.fori_loop(0, count, body, 0)

    def scatter_copy(s, r, row):
        return pltpu.make_async_copy(ybuf.at[s, r], y_hbm.at[row], ssem.at[s])

    def wait_scatter(tile, s):
        @pl.when(cnt_ref[tile] == tm)
        def _():
            pltpu.make_async_copy(ybuf.at[s], y_hbm.at[pl.ds(0, tm)], ssem.at[s]).wait()

        @pl.when(cnt_ref[tile] < tm)
        def _():
            wait_rows(lambda r: scatter_copy(s, r, 0), cnt_ref[tile])

    @pl.when(i == 0)
    def _():
        gather(src_now_ref, 0)

    @pl.when(i + 1 < n_used)
    def _():
        gather(src_next_ref, 1 - slot)

    @pl.when(i < n_used)
    def _():
        pltpu.make_async_copy(xn_hbm.at[pl.ds(0, tm)], xbuf.at[slot], gsem.at[slot]).wait()

        @pl.when(i >= 2)
        def _():
            wait_scatter(i - 2, slot)

        x = xbuf[slot].reshape(tm, D_MODEL)
        y = _swiglu_body(x, wg_ref, wu_ref, wd_ref, h_ref)
        ybuf[slot] = y.astype(BF16).reshape(tm, D_MODEL // LANES, LANES)

        def send(r, _):
            scatter_copy(slot, r, dst_ref[0, 0, r]).start()
            return 0

        @pl.when(cnt_ref[i] == tm)
        def _():
            lax.fori_loop(0, tm, send, 0, unroll=8)

        @pl.when(cnt_ref[i] < tm)
        def _():
            lax.fori_loop(0, cnt_ref[i], send, 0)

    @pl.when(i == pl.num_programs(0) - 1)
    def _():
        @pl.when(n_used >= 2)
        def _():
            wait_scatter(n_used - 2, (n_used - 2) & 1)
        wait_scatter(n_used - 1, (n_used - 1) & 1)


def _ffn_experts(tile_expert, n_used, tile_count, src_tok, dst_row, xn3, wg, wu, wd):
    tm = EXPERT_TILE
    n_tiles = src_tok.shape[0]
    t = xn3.shape[0]
    wsel = lambda i, te, nu, cnt: (te[i], 0, 0)
    idx = lambda i, te, nu, cnt: (i, 0, 0)
    idx_next = lambda i, te, nu, cnt: (jnp.minimum(i + 1, n_tiles - 1), 0, 0)
    smem_idx = lambda m: pl.BlockSpec((1, 1, tm), m, memory_space=pltpu.SMEM)
    rows = D_MODEL // LANES
    grid_spec = pltpu.PrefetchScalarGridSpec(
        num_scalar_prefetch=3,
        grid=(n_tiles,),
        in_specs=[
            smem_idx(idx),
            smem_idx(idx_next),
            smem_idx(idx),
            pl.BlockSpec(memory_space=pl.ANY),
            pl.BlockSpec((None, D_MODEL, D_FF), wsel),
            pl.BlockSpec((None, D_MODEL, D_FF), wsel),
            pl.BlockSpec((None, D_FF, D_MODEL), wsel),
        ],
        out_specs=pl.BlockSpec(memory_space=pl.ANY),
        scratch_shapes=[
            pltpu.VMEM((2, tm, rows, LANES), BF16),
            pltpu.VMEM((2, tm, rows, LANES), BF16),
            pltpu.VMEM((tm, D_FF), BF16),
            pltpu.SemaphoreType.DMA((2,)),
            pltpu.SemaphoreType.DMA((2,)),
        ],
    )
    return pl.pallas_call(
        _ffn_expert_kernel,
        grid_spec=grid_spec,
        out_shape=jax.ShapeDtypeStruct((TOP_K * t, rows, LANES), BF16),
        compiler_params=_cp(("arbitrary",), has_side_effects=True),
        name="ffn_experts",
    )(tile_expert, n_used, tile_count, src_tok, src_tok, dst_row, xn3, wg, wu, wd)


def _proj_c_kernel(x_ref, g_ref, w_ref, wa_ref, ba_ref, tri_ref, qk_ref, v_ref, r_ref, la_ref):
    xn = _rms_rows(x_ref[...], g_ref[...]).astype(BF16)
    for c in range(2):
        qk_ref[:, 512 * c:512 * (c + 1)] = _dot(xn, w_ref[:, 512 * c:512 * (c + 1)]).astype(BF16)
    for c in range(2):
        v_ref[:, 512 * c:512 * (c + 1)] = _dot(xn, w_ref[:, 1024 + 512 * c:1536 + 512 * c]).astype(BF16)
    for c in range(2):
        r_ref[:, 512 * c:512 * (c + 1)] = _dot(xn, w_ref[:, 2048 + 512 * c:2560 + 512 * c]).astype(BF16)
    a1 = _dot(xn, w_ref[:, 3072:3200])
    hi, lo = _split2(a1)
    wa = wa_ref[...]
    pre = _dot(hi, wa) + _dot(lo, wa) + ba_ref[...]
    la = _log_sigmoid(pre) * (1.0 / GLA_TAU)
    tri = tri_ref[...]
    c_hi, c_mid, c_lo = _split3(la)
    la_ref[...] = _dot(tri, c_hi) + _dot(tri, c_mid) + _dot(tri, c_lo)


def _proj_c(x2d, norm_g, w_pad, wa_pad, ba_row):
    t = x2d.shape[0]
    tm = ROW_TILE
    idx = np.arange(tm)
    tri = ((idx[:, None] // GLA_CHUNK == idx[None, :] // GLA_CHUNK) & (idx[:, None] >= idx[None, :])).astype(np.float32)
    row = lambda i: (i, 0)
    c2 = lambda i: (0, 0)
    return pl.pallas_call(
        _proj_c_kernel,
        grid=(t // tm,),
        in_specs=[
            pl.BlockSpec((tm, D_MODEL), row),
            pl.BlockSpec((1, D_MODEL), c2),
            pl.BlockSpec((D_MODEL, 3200), c2),
            pl.BlockSpec((LANES, GLA_DK), c2),
            pl.BlockSpec((1, GLA_DK), c2),
            pl.BlockSpec((tm, tm), c2),
        ],
        out_specs=[
            pl.BlockSpec((tm, 1024), row),
            pl.BlockSpec((tm, 1024), row),
            pl.BlockSpec((tm, 1024), row),
            pl.BlockSpec((tm, GLA_DK), row),
        ],
        out_shape=[
            jax.ShapeDtypeStruct((t, 1024), BF16),
            jax.ShapeDtypeStruct((t, 1024), BF16),
            jax.ShapeDtypeStruct((t, 1024), BF16),
            jax.ShapeDtypeStruct((t, GLA_DK), F32),
        ],
        compiler_params=_cp(("parallel",)),
        name="proj_c",
    )(x2d, norm_g, w_pad, wa_pad, ba_row, jnp.asarray(tri, BF16))


GLA_STEP_HEADS = 2


def _gla_kernel(q_ref, k_ref, v_ref, b_ref, eye_ref, o_ref, st_ref):
    c_len, sub = GLA_CHUNK, GLA_SUB
    n_sub = c_len // sub
    hs = range(GLA_STEP_HEADS)
    dk = q_ref.shape[1] // GLA_STEP_HEADS
    dv = v_ref.shape[1] // GLA_STEP_HEADS
    scale = dk ** -0.5
    eye = eye_ref[...]
    st_ref[...] = jnp.zeros_like(st_ref)
    row = lax.broadcasted_iota(jnp.int32, (c_len, c_len), 0)
    col = lax.broadcasted_iota(jnp.int32, (c_len, c_len), 1)
    col_in_sub = col & (sub - 1)
    same_sub_causal = ((row >> 4) == (col >> 4)) & ((row & (sub - 1)) >= col_in_sub)
    earlier_sub = (col >> 4) < (row >> 4)

    def chunk(c, _):
        rows = pl.ds(pl.multiple_of(c * c_len, c_len), c_len)
        q = [q_ref[rows, h * dk:(h + 1) * dk].astype(F32) * scale for h in hs]
        k = [k_ref[rows, h * dk:(h + 1) * dk].astype(F32) for h in hs]
        vb = [v_ref[rows, h * dv:(h + 1) * dv] for h in hs]
        b = [b_ref[rows, h * dk:(h + 1) * dk] for h in hs]
        b_last = [b[h][c_len - 1:c_len, :] for h in hs]
        st_t = [st_ref[h] for h in hs]

        v_t = [_dot_nt(eye, vb[h]).astype(BF16) for h in hs]
        inter = [_dot_nt((q[h] * jnp.exp(b[h])).astype(BF16), st_t[h].astype(BF16)) for h in hs]
        att_rows = [[jnp.zeros((sub, c_len), F32)] for _ in hs]
        for i in range(1, n_sub):
            for h in hs:
                r = b[h][i * sub:i * sub + 1, :]
                qi = (q[h][i * sub:(i + 1) * sub] * jnp.exp(b[h][i * sub:(i + 1) * sub] - r)).astype(BF16)
                kj = (k[h] * jnp.exp(jnp.minimum(r - b[h], 0.0))).astype(BF16)
                att_rows[h].append(_dot_nt(qi, kj))
        for h in hs:
            kd = (k[h] * jnp.exp(b_last[h] - b[h])).astype(BF16)
            st_ref[h] = st_t[h] * jnp.exp(b_last[h]) + _dot(v_t[h], kd)

        att_d = [jnp.zeros((c_len, c_len), F32) for _ in hs]
        for j in range(sub):
            def rows_j(x):
                return jnp.concatenate(
                    [jnp.broadcast_to(x[s * sub + j:s * sub + j + 1, :], (sub, x.shape[1]))
                     for s in range(n_sub)], axis=0)
            for h in hs:
                e = jnp.exp(jnp.minimum(b[h] - rows_j(b[h]), 0.0))
                a = jnp.sum(q[h] * e * rows_j(k[h]), axis=-1, keepdims=True)
                att_d[h] = jnp.where(col_in_sub == j, a, att_d[h])
        for h in hs:
            att = jnp.where(earlier_sub, jnp.concatenate(att_rows[h], axis=0),
                            jnp.where(same_sub_causal, att_d[h], 0.0)).astype(BF16)
            o_ref[rows, h * dv:(h + 1) * dv] = (inter[h] + _dot(att, vb[h])).astype(BF16)
        return 0

    def two_chunks(cc, carry):
        chunk(2 * cc, carry)
        return chunk(2 * cc + 1, carry)

    lax.fori_loop(0, q_ref.shape[0] // (2 * c_len), two_chunks, 0)


def _gla(qk3, v3, b3):
    b, s, _ = qk3.shape
    n = GLA_STEP_HEADS
    dk = GLA_DK // GLA_HEADS
    dv = GLA_DV // GLA_HEADS
    steps = GLA_HEADS // n
    eye = jnp.asarray(np.eye(dv, dtype=np.float32), BF16)
    return pl.pallas_call(
        _gla_kernel,
        grid=(b, steps),
        in_specs=[
            pl.BlockSpec((None, s, n * dk), lambda i, h: (i, 0, h)),
            pl.BlockSpec((None, s, n * dk), lambda i, h: (i, 0, steps + h)),
            pl.BlockSpec((None, s, n * dv), lambda i, h: (i, 0, h)),
            pl.BlockSpec((None, s, n * dk), lambda i, h: (i, 0, h)),
            pl.BlockSpec((dv, dv), lambda i, h: (0, 0)),
        ],
        out_specs=pl.BlockSpec((None, s, n * dv), lambda i, h: (i, 0, h)),
        out_shape=jax.ShapeDtypeStruct((b, s, GLA_DV), BF16),
        scratch_shapes=[pltpu.VMEM((n, dv, dk), F32)],
        compiler_params=_cp(("parallel", "parallel")),
        name="gla",
    )(qk3, qk3, v3, b3, eye)


def _out_c_kernel(o_ref, r_ref, x_ref, on_ref, bd_ref, w_ref, mg_ref, rt_ref,
                  x3_ref, xn_ref, e1_ref, e2_ref, w1_ref, w2_ref):
    dv = GLA_DV // GLA_HEADS
    bd = bd_ref[...]
    parts = []
    for h in range(GLA_HEADS):
        cols = slice(h * dv, (h + 1) * dv)
        oh = _seg_norm(o_ref[:, cols].astype(F32), on_ref[...], bd)
        r = r_ref[:, cols].astype(F32)
        parts.append((oh * (r * _sigmoid(r))).astype(BF16))
    x3 = x_ref[...] + _dot(jnp.concatenate(parts, axis=1), w_ref[...])
    x3_ref[...] = x3

    xn = _rms_rows(x3, mg_ref[...])
    xn_ref[...] = xn.astype(BF16).reshape(xn_ref.shape)
    xh, xl = _split2(xn)
    logits = _dot(xh, rt_ref[0]) + _dot(xl, rt_ref[0]) + _dot(xh, rt_ref[1])
    lane = lax.broadcasted_iota(jnp.int32, logits.shape, 1)
    logits = jnp.where(lane < N_EXPERTS, logits, -3e38)
    v1 = jnp.max(logits, axis=-1, keepdims=True)
    i1 = jnp.min(jnp.where(logits == v1, lane, LANES), axis=-1, keepdims=True)
    rest = jnp.where(lane == i1, -3e38, logits)
    v2 = jnp.max(rest, axis=-1, keepdims=True)
    i2 = jnp.min(jnp.where(rest == v2, lane, LANES), axis=-1, keepdims=True)
    ex = jnp.exp(v2 - v1)
    den = 1.0 + ex
    e1_ref[...] = jnp.broadcast_to(i1, logits.shape)
    e2_ref[...] = jnp.broadcast_to(i2, logits.shape)
    w1_ref[...] = jnp.broadcast_to(1.0 / den, logits.shape)
    w2_ref[...] = jnp.broadcast_to(ex / den, logits.shape)


def _out_c(o2d, r2d, x2d, onorm_row, w_out, mnorm_row, router2):
    t = x2d.shape[0]
    tm = ROW_TILE
    dv = GLA_DV // GLA_HEADS
    bd = jnp.asarray(_block_diag_mean(dv, dv), BF16)
    row = lambda i: (i, 0)
    c2 = lambda i: (0, 0)
    slab = pl.BlockSpec((tm, LANES), row)
    return pl.pallas_call(
        _out_c_kernel,
        grid=(t // tm,),
        in_specs=[
            pl.BlockSpec((tm, GLA_DV), row),
            pl.BlockSpec((tm, GLA_DV), row),
            pl.BlockSpec((tm, D_MODEL), row),
            pl.BlockSpec((1, dv), c2),
            pl.BlockSpec((dv, dv), c2),
            pl.BlockSpec((D_MODEL, D_MODEL), c2),
            pl.BlockSpec((1, D_MODEL), c2),
            pl.BlockSpec((2, D_MODEL, LANES), lambda i: (0, 0, 0)),
        ],
        out_specs=[
            pl.BlockSpec((tm, D_MODEL), row),
            pl.BlockSpec((tm, D_MODEL // LANES, LANES), lambda i: (i, 0, 0)),
            slab, slab, slab, slab,
        ],
        out_shape=[
            jax.ShapeDtypeStruct((t, D_MODEL), F32),
            jax.ShapeDtypeStruct((t, D_MODEL // LANES, LANES), BF16),
            jax.ShapeDtypeStruct((t, LANES), jnp.int32),
            jax.ShapeDtypeStruct((t, LANES), jnp.int32),
            jax.ShapeDtypeStruct((t, LANES), F32),
            jax.ShapeDtypeStruct((t, LANES), F32),
        ],
        compiler_params=_cp(("parallel",)),
        name="out_c_router",
    )(o2d, r2d, x2d, onorm_row, bd, w_out, mnorm_row, router2)


def _combine_kernel(ya_ref, yb_ref, x_ref, w1_ref, w2_ref, o_ref):
    tm = x_ref.shape[0]
    reps = D_MODEL // LANES
    w1 = jnp.concatenate([w1_ref[...]] * reps, axis=1)
    w2 = jnp.concatenate([w2_ref[...]] * reps, axis=1)
    ya = ya_ref[...].reshape(tm, D_MODEL).astype(F32)
    yb = yb_ref[...].reshape(tm, D_MODEL).astype(F32)
    o_ref[...] = x_ref[...] + w1 * ya + w2 * yb


def _moe_combine(ys3, x2d, w1b, w2b):
    t = x2d.shape[0]
    tm = ROW_TILE
    row = lambda i: (i, 0)
    rows = D_MODEL // LANES
    second = t // tm
    return pl.pallas_call(
        _combine_kernel,
        grid=(t // tm,),
        in_specs=[
            pl.BlockSpec((tm, rows, LANES), lambda i: (i, 0, 0)),
            pl.BlockSpec((tm, rows, LANES), lambda i: (second + i, 0, 0)),
            pl.BlockSpec((tm, D_MODEL), row),
            pl.BlockSpec((tm, LANES), row),
            pl.BlockSpec((tm, LANES), row),
        ],
        out_specs=pl.BlockSpec((tm, D_MODEL), row),
        out_shape=jax.ShapeDtypeStruct((t, D_MODEL), F32),
        compiler_params=_cp(("parallel",)),
        name="moe_combine",
    )(ys3, ys3, x2d, w1b, w2b)


def _nsa_head_perm():
    order = []
    for t in range(4):
        order += [t, 4 + t]
    cols = np.concatenate([np.arange(h * HEAD_DIM, (h + 1) * HEAD_DIM) for h in order])
    return order, cols


def _layer_even(x, a_norm, a_w_in, a_q_norm, a_k_norm, a_pe_k, a_pe_v, ck1, ck2, cv1, cv2,
                gate_bias, fq_norm, fk_norm, f_bias, a_w_out, f_norm, f_wg, f_wu, f_wd):
    b, s, d = x.shape
    t = b * s
    x2d = x.reshape(t, d)
    order, qcols = _nsa_head_perm()

    def head_tiles(cols0):
        wq = a_w_in[:, cols0:cols0 + 512].reshape(d, FOX_HEADS, HEAD_DIM)
        return jnp.pad(wq, ((0, 0), (0, 0), (0, LANES - HEAD_DIM))).reshape(d, FOX_HEADS * LANES)

    w = jnp.concatenate([
        jnp.take(a_w_in, jnp.asarray(qcols), axis=1),
        a_w_in[:, 512:1280],
        head_tiles(1304),
        head_tiles(1816),
        a_w_in[:, 2328:2840],
        a_w_in[:, 1280:1304],
        a_w_in[:, 2840:2848],
        jnp.zeros((d, LANES - 32), F32),
    ], axis=1).astype(BF16)

    tile2 = lambda g: jnp.concatenate([g, g])
    lo_only = lambda g: jnp.concatenate([g, jnp.zeros((LANES - HEAD_DIM,), F32)])
    q_bias_lanes = jnp.zeros((LANES,), F32).at[FOX_BIAS_LANE:FOX_BIAS_LANE + 3].set(-1.0)
    head_gains = jnp.stack([tile2(a_q_norm), tile2(a_k_norm), lo_only(fq_norm), lo_only(fk_norm), q_bias_lanes]
                           + [jnp.zeros((LANES,), F32)] * 3)
    cos, sin = _rope_tables(np.arange(s))
    qa, kv6, fox, gf = _proj_a(x2d, a_norm.reshape(1, d), w, head_gains, cos, sin, s)

    fox3 = fox.reshape(b, s, FOX_SLAB)
    bias_row = jnp.zeros((1, LANES), F32).at[0, 24:32].set(f_bias)
    cum, k_aug = _fox_cumlog(gf.reshape(b, s, LANES), bias_row, fox3)
    cum_row = cum[:, :, 24:32].transpose(0, 2, 1)
    o_b = _fox_attention_t(fox3, k_aug, _values_t(fox3[:, :, 2048:2560]), cum_row).transpose(0, 2, 1)

    ncp = s // CMP_STRIDE
    n_cmp = (s - CMP_LEN) // CMP_STRIDE + 1
    n_slc = s // SLC_BLOCK
    kv3 = kv6.reshape(b, s, 768)
    kx = kv3[:, :, 0:128].reshape(b, ncp, CMP_STRIDE * LANES)
    vx = kv3[:, :, 128:256].reshape(b, ncp, CMP_STRIDE * LANES)

    def pe_rows(pe):
        p2 = jnp.concatenate([pe, pe], axis=1)
        return p2.reshape(2, CMP_STRIDE * LANES)

    def w1_blocks(w1):
        w4 = w1.reshape(2, CMP_STRIDE, HEAD_DIM, HEAD_DIM)
        z = jnp.zeros_like(w4)
        top = jnp.concatenate([w4, z], axis=-1)
        bot = jnp.concatenate([z, w4], axis=-1)
        return jnp.stack([top, bot], axis=2).reshape(2, CMP_STRIDE * LANES, LANES).astype(BF16)

    def w2_block(w2):
        z = jnp.zeros_like(w2)
        return jnp.concatenate([jnp.concatenate([w2, z], 1), jnp.concatenate([z, w2], 1)], 0).astype(BF16)

    ccos, csin = _rope_tables(np.arange(ncp) * CMP_STRIDE + CMP_LEN - 1)
    kcmp, vcmp = _nsa_compress(kx, vx, pe_rows(a_pe_k), pe_rows(a_pe_v), w1_blocks(ck1), w2_block(ck2),
                               w1_blocks(cv1), w2_block(cv2), tile2(a_k_norm).reshape(1, LANES), ccos, csin)

    c_start = np.arange(ncp) * CMP_STRIDE
    s_start = np.arange(64) * SLC_BLOCK
    ov = np.maximum(np.minimum(c_start[:, None] + CMP_LEN, s_start[None, :] + SLC_BLOCK)
                    - np.maximum(c_start[:, None], s_start[None, :]), 0).astype(np.float32) / CMP_LEN
    ov[n_cmp:, :] = 0.0
    ov[:, n_slc:] = 0.0
    qa3 = qa.reshape(b, s, 512)
    o_c, sb = _nsa_cmp_select(qa3, kcmp, vcmp, jnp.asarray(ov.T, BF16), n_cmp, n_slc)

    onehot = jnp.asarray((np.arange(s)[:, None] // SLC_BLOCK) == np.arange(64)[None, :], BF16)
    onehot = jnp.broadcast_to(onehot[None], (b, s, 64))
    ksl = kv3[:, :, 256:384]
    k_aug0 = jnp.concatenate([ksl[:, :, :64], onehot], axis=-1)
    k_aug1 = jnp.concatenate([onehot, ksl[:, :, 64:]], axis=-1)
    o_s = _nsa_selected_t(qa3, sb, k_aug0, k_aug1, _values_t(kv3[:, :, 384:512]))

    o_w = _nsa_window(qa3, kv3[:, :, 512:640], kv3[:, :, 640:768])

    gexp = np.zeros((3, LANES, 512), np.float32)
    for tile_i in range(4):
        for half, h in enumerate((tile_i, 4 + tile_i)):
            for j in range(3):
                gexp[j, h * 3 + j, tile_i * LANES + half * HEAD_DIM:tile_i * LANES + (half + 1) * HEAD_DIM] = 1.0
    gate_bias_row = jnp.zeros((1, LANES), F32).at[0, 0:24].set(gate_bias)
    w_out = jnp.concatenate([jnp.take(a_w_out[:512], jnp.asarray(qcols), axis=0), a_w_out[512:]], 0).astype(BF16)
    x1 = _out_a(o_c.reshape(t, 512), o_s.reshape(t, 512), o_w.reshape(t, 512), o_b.reshape(t, 512),
                gf, gate_bias_row, jnp.asarray(gexp, BF16), x2d, w_out)

    x2 = _ffn_dense(x1, f_norm.reshape(1, d), f_wg.astype(BF16), f_wu.astype(BF16), f_wd.astype(BF16))
    return x2.reshape(b, s, d)


def _layer_odd(x, c_norm, c_w_in, c_w_a2, c_b_a, c_o_norm, c_w_out, m_norm, m_router, m_wg, m_wu, m_wd):
    b, s, d = x.shape
    t = b * s
    x2d = x.reshape(t, d)
    w = jnp.pad(c_w_in, ((0, 0), (0, 3200 - c_w_in.shape[1]))).astype(BF16)
    wa = jnp.pad(c_w_a2, ((0, LANES - GLA_RANK), (0, 0))).astype(BF16)
    qk, v, r, la = _proj_c(x2d, c_norm.reshape(1, d), w, wa, c_b_a.reshape(1, GLA_DK))
    o = _gla(qk.reshape(b, s, 1024), v.reshape(b, s, 1024), la.reshape(b, s, GLA_DK))

    rt = jnp.pad(m_router, ((0, 0), (0, LANES - N_EXPERTS)))
    rt_hi = rt.astype(BF16)
    rt_lo = (rt - rt_hi.astype(F32)).astype(BF16)
    x3, xn, e1, e2, w1b, w2b = _out_c(o.reshape(t, GLA_DV), r, x2d, c_o_norm.reshape(1, -1),
                                      c_w_out.astype(BF16), m_norm.reshape(1, d), jnp.stack([rt_hi, rt_lo]))

    tm = EXPERT_TILE
    n_slot = TOP_K * t + N_EXPERTS * tm
    n_tiles = n_slot // tm
    flat_e = jnp.concatenate([e1[:, 0], e2[:, 0]])
    onehot = (flat_e[:, None] == jnp.arange(N_EXPERTS)[None, :]).astype(jnp.int32)
    csum = jnp.cumsum(onehot, axis=0)
    rank = jnp.sum(csum * onehot, axis=1) - 1
    counts = csum[-1]
    padded = ((counts + tm - 1) // tm) * tm
    ends = jnp.cumsum(padded)
    offs = ends - padded
    slot_of = (jnp.sum(onehot * offs[None, :], axis=1) + rank).astype(jnp.int32)
    n_used = (ends[-1] // tm).astype(jnp.int32)
    tile_start = jnp.arange(n_tiles, dtype=jnp.int32) * tm
    tile_e = jnp.sum((tile_start[:, None] >= ends[None, :]).astype(jnp.int32), axis=1)
    last_e = jnp.sum((((n_used - 1) * tm) >= ends).astype(jnp.int32))
    tile_e = jnp.where(jnp.arange(n_tiles) < n_used, tile_e, last_e).astype(jnp.int32)
    row_of = jnp.full((n_slot,), -1, jnp.int32).at[slot_of].set(jnp.arange(TOP_K * t, dtype=jnp.int32))
    valid = row_of >= 0
    src_tok = jnp.where(valid, row_of % t, 0).reshape(n_tiles, 1, tm)
    dst_row = jnp.where(valid, row_of, 0).reshape(n_tiles, 1, tm)
    tile_count = jnp.sum(valid.reshape(n_tiles, tm).astype(jnp.int32), axis=1)

    ys = _ffn_experts(tile_e, n_used.reshape(1), tile_count, src_tok, dst_row, xn,
                      m_wg.astype(BF16), m_wu.astype(BF16), m_wd.astype(BF16))
    out = _moe_combine(ys, x3, w1b, w2b)
    return out.reshape(b, s, d)


def kernel(x, a_norm, a_w_in, a_q_norm, a_k_norm, a_pe_k, a_pe_v, a_cmp_k_w1, a_cmp_k_w2, a_cmp_v_w1, a_cmp_v_w2, a_gate_bias, a_fox_q_norm, a_fox_k_norm, a_fox_f_bias, a_w_out, f_norm, f_w_gate, f_w_up, f_w_down, c_norm, c_w_in, c_w_a2, c_b_a, c_o_norm, c_w_out, m_norm, m_router, m_w_gate, m_w_up, m_w_down):
    x = _layer_even(x, a_norm[0], a_w_in[0], a_q_norm[0], a_k_norm[0], a_pe_k[0], a_pe_v[0],
                    a_cmp_k_w1[0], a_cmp_k_w2[0], a_cmp_v_w1[0], a_cmp_v_w2[0], a_gate_bias[0],
                    a_fox_q_norm[0], a_fox_k_norm[0], a_fox_f_bias[0], a_w_out[0],
                    f_norm[0], f_w_gate[0], f_w_up[0], f_w_down[0])
    x = _layer_odd(x, c_norm[0], c_w_in[0], c_w_a2[0], c_b_a[0], c_o_norm[0], c_w_out[0],
                   m_norm[0], m_router[0], m_w_gate[0], m_w_up[0], m_w_down[0])
    return x
```

```python
import functools

import numpy as np
import jax
import jax.numpy as jnp
from jax import lax
from jax.experimental import pallas as pl
from jax.experimental.pallas import tpu as pltpu

F32 = jnp.float32
BF16 = jnp.bfloat16

D_MODEL = 1024
HEAD_DIM = 64
ROPE_DIM = 16
ROPE_THETA = 500000.0
NORM_EPS = 1e-6
NEG = -1e30
FORCE_SCORE = 1e6

NSA_HEADS = 8
CMP_LEN = 32
CMP_STRIDE = 16
SLC_BLOCK = 64
SLC_TOPK = 16
WINDOW = 512
FOX_HEADS = 8

GLA_HEADS = 4
GLA_DK = 512
GLA_DV = 1024
GLA_RANK = 16
GLA_TAU = 16.0
GLA_CHUNK = 64
GLA_SUB = 16

D_FF = 2816
N_EXPERTS = 8
TOP_K = 2

LANES = 128
FF_CHUNK = 256
VMEM_LIMIT = 56 * 1024 * 1024

ROW_TILE = 512
ATT_Q = 128
FOX_Q = 512
FOX_K = 256
SLC_K = 256
EXPERT_TILE = 512

NT_DIMS = (((1,), (1,)), ((), ()))


def _cp(sem, **kw):
    return pltpu.CompilerParams(dimension_semantics=sem, vmem_limit_bytes=VMEM_LIMIT, **kw)


def _dot(a, b):
    return jnp.dot(a, b, preferred_element_type=F32)


def _dot_nt(a, b):
    return lax.dot_general(a, b, NT_DIMS, preferred_element_type=F32)


def _split2(x):
    hi = x.astype(BF16)
    lo = (x - hi.astype(F32)).astype(BF16)
    return hi, lo


def _split3(x):
    hi = x.astype(BF16)
    r = x - hi.astype(F32)
    mid = r.astype(BF16)
    lo = (r - mid.astype(F32)).astype(BF16)
    return hi, mid, lo


def _rms_rows(x, gain):
    return x * lax.rsqrt(jnp.mean(x * x, axis=-1, keepdims=True) + NORM_EPS) * gain


def _log_sigmoid(x):
    return jnp.minimum(x, 0.0) - jnp.log1p(jnp.exp(-jnp.abs(x)))


def _sigmoid(x):
    return 1.0 / (1.0 + jnp.exp(-x))


def _seg_norm(y, gain, bd):
    ms = _dot((y * y).astype(BF16), bd)
    return y * lax.rsqrt(ms + NORM_EPS) * gain


def _rope(y, cos, sin, perm):
    return y * cos + _dot(y.astype(BF16), perm) * sin


def _block_diag_mean(width, seg):
    i = np.arange(width)
    return ((i[:, None] // seg) == (i[None, :] // seg)).astype(np.float32) / seg


def _rope_perm():
    p = np.zeros((LANES, LANES), np.float32)
    half = ROPE_DIM // 2
    for j in range(LANES):
        d = j % HEAD_DIM
        if d < half:
            p[j + half, j] = -1.0
        elif d < ROPE_DIM:
            p[j - half, j] = 1.0
    return p


def _rope_tables(pos):
    inv = ROPE_THETA ** (-np.arange(0, ROPE_DIM, 2, dtype=np.float64) / ROPE_DIM)
    ang = pos.astype(np.float64)[:, None] * inv[None, :]
    cos8, sin8 = np.cos(ang), np.sin(ang)
    n = pos.shape[0]
    ones = np.ones((n, HEAD_DIM - ROPE_DIM))
    zeros = np.zeros((n, HEAD_DIM - ROPE_DIM))
    c = np.concatenate([cos8, cos8, ones], axis=1)
    s = np.concatenate([sin8, sin8, zeros], axis=1)
    return (jnp.asarray(np.concatenate([c, c], axis=1), F32),
            jnp.asarray(np.concatenate([s, s], axis=1), F32))


PROJ_A_COLS = 3968
FOX_SLAB = 2560


def _proj_a_kernel(x_ref, g_ref, w_ref, hg_ref, cos_ref, sin_ref, bd_ref, bdlo_ref, perm_ref,
                   qa_ref, kv_ref, fox_ref, gf_ref):
    xn = _rms_rows(x_ref[...], g_ref[...]).astype(BF16)
    bd = bd_ref[...]
    perm = perm_ref[...]
    cos = cos_ref[...]
    sin = sin_ref[...]

    def tile(acc, i):
        return acc[:, i * LANES:(i + 1) * LANES]

    acc = _dot(xn, w_ref[:, 0:512])
    for i in range(4):
        y = _rope(_seg_norm(tile(acc, i), hg_ref[0:1, :], bd), cos, sin, perm)
        qa_ref[:, i * LANES:(i + 1) * LANES] = (y * 0.125).astype(BF16)

    acc = _dot(xn, w_ref[:, 512:1280])
    for i in range(6):
        y = tile(acc, i)
        if i in (2, 4):
            y = _rope(_seg_norm(y, hg_ref[1:2, :], bd), cos, sin, perm)
        kv_ref[:, i * LANES:(i + 1) * LANES] = y.astype(BF16)

    bd_lo = bdlo_ref[...]
    for c in range(4):
        acc = _dot(xn, w_ref[:, 1280 + 512 * c:1280 + 512 * (c + 1)])
        for i in range(4):
            y = tile(acc, i)
            if c < 2:
                y = _seg_norm(y, hg_ref[2:3, :], bd_lo) * 0.125 + hg_ref[4:5, :]
            else:
                y = _seg_norm(y, hg_ref[3:4, :], bd_lo)
            fox_ref[:, 512 * c + i * LANES:512 * c + (i + 1) * LANES] = y.astype(BF16)
    fox_ref[:, 2048:2560] = _dot(xn, w_ref[:, 3328:3840]).astype(BF16)

    gf_ref[...] = _dot(xn, w_ref[:, 3840:3968])


def _proj_a(x2d, norm_g, w_perm, head_gains, cos, sin, seq):
    t = x2d.shape[0]
    tm = ROW_TILE
    n_pos = seq // tm
    bd = jnp.asarray(_block_diag_mean(LANES, HEAD_DIM), BF16)
    lo = np.zeros((LANES, LANES), np.float32)
    lo[:HEAD_DIM, :] = 1.0 / HEAD_DIM
    perm = jnp.asarray(_rope_perm(), BF16)
    full = lambda i: (0, 0)
    return pl.pallas_call(
        _proj_a_kernel,
        grid=(t // tm,),
        in_specs=[
            pl.BlockSpec((tm, D_MODEL), lambda i: (i, 0)),
            pl.BlockSpec((1, D_MODEL), full),
            pl.BlockSpec((D_MODEL, PROJ_A_COLS), full),
            pl.BlockSpec((8, LANES), full),
            pl.BlockSpec((tm, LANES), lambda i: (i % n_pos, 0)),
            pl.BlockSpec((tm, LANES), lambda i: (i % n_pos, 0)),
            pl.BlockSpec((LANES, LANES), full),
            pl.BlockSpec((LANES, LANES), full),
            pl.BlockSpec((LANES, LANES), full),
        ],
        out_specs=[
            pl.BlockSpec((tm, 512), lambda i: (i, 0)),
            pl.BlockSpec((tm, 768), lambda i: (i, 0)),
            pl.BlockSpec((tm, FOX_SLAB), lambda i: (i, 0)),
            pl.BlockSpec((tm, LANES), lambda i: (i, 0)),
        ],
        out_shape=[
            jax.ShapeDtypeStruct((t, 512), BF16),
            jax.ShapeDtypeStruct((t, 768), BF16),
            jax.ShapeDtypeStruct((t, FOX_SLAB), BF16),
            jax.ShapeDtypeStruct((t, LANES), F32),
        ],
        compiler_params=_cp(("parallel",)),
        name="proj_a",
    )(x2d, norm_g, w_perm, head_gains, cos, sin, bd, jnp.asarray(lo, BF16), perm)


CUM_CHUNK = 256
FOX_BIAS_LANE = 64


def _cum_kernel(gf_ref, bias_ref, tri_ref, k_ref, place_ref, out_ref, ka_ref):
    tri = tri_ref[...]
    n_chunk = gf_ref.shape[0] // CUM_CHUNK
    carry = jnp.zeros((1, LANES), F32)
    for c in range(n_chunk):
        rows = slice(c * CUM_CHUNK, (c + 1) * CUM_CHUNK)
        lf = _log_sigmoid(gf_ref[rows, :] + bias_ref[...])
        hi, mid, lo = _split3(lf)
        cs = _dot(tri, hi) + _dot(tri, mid) + _dot(tri, lo) + carry
        out_ref[rows, :] = cs
        carry = cs[CUM_CHUNK - 1:CUM_CHUNK, :]
        c_hi, c_mid, c_lo = _split3(cs)
        placed = _dot(c_hi, place_ref[0]) + _dot(c_mid, place_ref[1]) + _dot(c_lo, place_ref[2])
        ka_ref[rows, :] = (k_ref[rows, :].astype(F32) + placed).astype(BF16)


def _fox_cumlog(gf3, bias_row, fox3):
    b, s, _ = gf3.shape
    tri = jnp.asarray(np.tril(np.ones((CUM_CHUNK, CUM_CHUNK), np.float32)), BF16)
    place = np.zeros((3, LANES, FOX_HEADS * LANES), np.float32)
    for h in range(FOX_HEADS):
        for piece in range(3):
            place[piece, 24 + h, h * LANES + FOX_BIAS_LANE + piece] = 1.0
    return pl.pallas_call(
        _cum_kernel,
        grid=(b,),
        in_specs=[
            pl.BlockSpec((None, s, LANES), lambda i: (i, 0, 0)),
            pl.BlockSpec((1, LANES), lambda i: (0, 0)),
            pl.BlockSpec((CUM_CHUNK, CUM_CHUNK), lambda i: (0, 0)),
            pl.BlockSpec((None, s, FOX_HEADS * LANES), lambda i: (i, 0, 1)),
            pl.BlockSpec((3, LANES, FOX_HEADS * LANES), lambda i: (0, 0, 0)),
        ],
        out_specs=[
            pl.BlockSpec((None, s, LANES), lambda i: (i, 0, 0)),
            pl.BlockSpec((None, s, FOX_HEADS * LANES), lambda i: (i, 0, 0)),
        ],
        out_shape=[
            jax.ShapeDtypeStruct((b, s, LANES), F32),
            jax.ShapeDtypeStruct((b, s, FOX_HEADS * LANES), BF16),
        ],
        compiler_params=_cp(("parallel",)),
        name="fox_cumlog",
    )(gf3, bias_row, tri, fox3, jnp.asarray(place, BF16))


def _gelu_tanh(x):
    return 0.5 * x * (1.0 + jnp.tanh(0.7978845608028654 * (x + 0.044715 * x * x * x)))


def _cmp_kernel(kx_ref, vx_ref, pek_ref, pev_ref, w1k_ref, w2k_ref, w1v_ref, w2v_ref,
                kg_ref, cos_ref, sin_ref, bd_ref, perm_ref, kc_ref, vc_ref):
    ncp = kx_ref.shape[0]

    def compress(x_ref, pe_ref, w1_ref, w2_ref):
        x = x_ref[...].astype(F32)
        first = _dot((x + pe_ref[0:1, :]).astype(BF16), w1_ref[0])
        second = _dot((x + pe_ref[1:2, :]).astype(BF16), w1_ref[1])
        pre = first + pltpu.roll(second, ncp - 1, 0)
        return _dot(_gelu_tanh(pre).astype(BF16), w2_ref[...])

    k = compress(kx_ref, pek_ref, w1k_ref, w2k_ref)
    k = _rope(_seg_norm(k, kg_ref[...], bd_ref[...]), cos_ref[...], sin_ref[...], perm_ref[...])
    kc_ref[...] = k.astype(BF16)
    vc_ref[...] = compress(vx_ref, pev_ref, w1v_ref, w2v_ref).astype(BF16)


def _nsa_compress(kx, vx, pek, pev, w1k, w2k, w1v, w2v, kgain, ccos, csin):
    b, ncp, width = kx.shape
    bd = jnp.asarray(_block_diag_mean(LANES, HEAD_DIM), BF16)
    perm = jnp.asarray(_rope_perm(), BF16)
    c2 = lambda i: (0, 0)
    c3 = lambda i: (0, 0, 0)
    return pl.pallas_call(
        _cmp_kernel,
        grid=(b,),
        in_specs=[
            pl.BlockSpec((None, ncp, width), lambda i: (i, 0, 0)),
            pl.BlockSpec((None, ncp, width), lambda i: (i, 0, 0)),
            pl.BlockSpec((2, width), c2),
            pl.BlockSpec((2, width), c2),
            pl.BlockSpec((2, width, LANES), c3),
            pl.BlockSpec((LANES, LANES), c2),
            pl.BlockSpec((2, width, LANES), c3),
            pl.BlockSpec((LANES, LANES), c2),
            pl.BlockSpec((1, LANES), c2),
            pl.BlockSpec((ncp, LANES), c2),
            pl.BlockSpec((ncp, LANES), c2),
            pl.BlockSpec((LANES, LANES), c2),
            pl.BlockSpec((LANES, LANES), c2),
        ],
        out_specs=[
            pl.BlockSpec((None, ncp, LANES), lambda i: (i, 0, 0)),
            pl.BlockSpec((None, ncp, LANES), lambda i: (i, 0, 0)),
        ],
        out_shape=[
            jax.ShapeDtypeStruct((b, ncp, LANES), BF16),
            jax.ShapeDtypeStruct((b, ncp, LANES), BF16),
        ],
        compiler_params=_cp(("parallel",)),
        name="nsa_compress",
    )(kx, vx, pek, pev, w1k, w2k, w1v, w2v, kgain, ccos, csin, bd, perm)


def _nsa_cmp_kernel(q_ref, kc_ref, vc_ref, ovt_ref, eye_ref, oc_ref, sb_ref, *, n_cmp, n_slc):
    tq = ATT_Q
    q0 = pl.program_id(1) * tq
    kc = kc_ref[...]
    vc = vc_ref[...]
    ncp = kc.shape[0]
    ovt = ovt_ref[...]
    lo_half = lax.broadcasted_iota(jnp.int32, (tq, LANES), 1) < HEAD_DIM

    n_lane = lax.broadcasted_iota(jnp.int32, (tq, ncp), 1)
    t_sub = q0 + lax.broadcasted_iota(jnp.int32, (tq, ncp), 0)
    mask = (n_lane * CMP_STRIDE + (CMP_LEN - 1) <= t_sub) & (n_lane < n_cmp)
    n_sub = lax.broadcasted_iota(jnp.int32, (ncp, tq), 0)
    t_lane = q0 + lax.broadcasted_iota(jnp.int32, (ncp, tq), 1)
    mask_t = (n_sub * CMP_STRIDE + (CMP_LEN - 1) <= t_lane) & (n_sub < n_cmp)

    imp_t = [jnp.zeros((64, tq), F32), jnp.zeros((64, tq), F32)]
    for t in range(4):
        qt = q_ref[:, t * LANES:(t + 1) * LANES]
        outs = []
        for g in range(2):
            qm = jnp.where(lo_half if g == 0 else jnp.logical_not(lo_half), qt, jnp.zeros_like(qt))
            s = jnp.where(mask, _dot_nt(qm, kc), NEG)
            p = jnp.where(mask, jnp.exp(s - jnp.max(s, axis=-1, keepdims=True)), 0.0)
            p = p / jnp.maximum(jnp.sum(p, axis=-1, keepdims=True), 1e-30)
            outs.append(_dot(p.astype(BF16), vc))
            st = jnp.where(mask_t, _dot_nt(kc, qm), NEG)
            pt = jnp.where(mask_t, jnp.exp(st - jnp.max(st, axis=0, keepdims=True)), 0.0)
            pt = pt / jnp.maximum(jnp.sum(pt, axis=0, keepdims=True), 1e-30)
            hi, lo = _split2(pt)
            imp_t[g] = imp_t[g] + _dot(ovt, hi) + _dot(ovt, lo)
        oc_ref[:, t * LANES:(t + 1) * LANES] = jnp.where(lo_half, outs[0], outs[1]).astype(BF16)

    blk = lax.broadcasted_iota(jnp.int32, (64, tq), 0)
    cur = (q0 + lax.broadcasted_iota(jnp.int32, (64, tq), 1)) >> 6
    forced = (blk == 0) | (blk == cur) | (blk == cur - 1)
    future = blk > cur
    exists = blk < n_slc
    biases = []
    for g in range(2):
        v = jnp.where(forced, FORCE_SCORE, jnp.where(future, -FORCE_SCORE, imp_t[g]))
        v = jnp.where(exists, v, -3e38)
        sel = jnp.zeros((64, tq), jnp.int32)
        for _ in range(min(SLC_TOPK, n_slc)):
            m = jnp.max(v, axis=0, keepdims=True)
            first = jnp.min(jnp.where(v == m, blk, 64), axis=0, keepdims=True)
            hit = blk == first
            sel = jnp.where(hit, 1, sel)
            v = jnp.where(hit, -3.2e38, v)
        keep = (sel == 1) & jnp.logical_not(future) & exists
        biases.append(jnp.where(keep, 0.0, NEG).astype(BF16))
    sb_t = jnp.concatenate([biases[1], biases[0]], axis=0)
    sb_ref[...] = _dot_nt(eye_ref[...], sb_t).astype(BF16)


def _nsa_cmp_select(qa3, kcmp, vcmp, ovt, n_cmp, n_slc):
    b, s, _ = qa3.shape
    ncp = kcmp.shape[1]
    eye = jnp.asarray(np.eye(ATT_Q, dtype=np.float32), BF16)
    kern = functools.partial(_nsa_cmp_kernel, n_cmp=n_cmp, n_slc=n_slc)
    return pl.pallas_call(
        kern,
        grid=(b, s // ATT_Q),
        in_specs=[
            pl.BlockSpec((None, ATT_Q, 512), lambda i, j: (i, j, 0)),
            pl.BlockSpec((None, ncp, LANES), lambda i, j: (i, 0, 0)),
            pl.BlockSpec((None, ncp, LANES), lambda i, j: (i, 0, 0)),
            pl.BlockSpec((64, ncp), lambda i, j: (0, 0)),
            pl.BlockSpec((ATT_Q, ATT_Q), lambda i, j: (0, 0)),
        ],
        out_specs=[
            pl.BlockSpec((None, ATT_Q, 512), lambda i, j: (i, j, 0)),
            pl.BlockSpec((None, ATT_Q, LANES), lambda i, j: (i, j, 0)),
        ],
        out_shape=[
            jax.ShapeDtypeStruct((b, s, 512), BF16),
            jax.ShapeDtypeStruct((b, s, LANES), BF16),
        ],
        compiler_params=_cp(("parallel", "parallel")),
        name="nsa_cmp_select",
    )(qa3, kcmp, vcmp, ovt, eye)


V_ROWS = 80


def _flash_step_t(carry, s_t, v_t, *, query_bias=None, mask=None):
    m, acc = carry
    if mask is not None:
        s_t = jnp.where(mask, s_t, NEG)
    col_max = jnp.max(s_t, axis=0, keepdims=True)
    if query_bias is not None:
        m_new = jnp.maximum(m, col_max + query_bias)
        shift = m_new - query_bias
    else:
        m_new = jnp.maximum(m, col_max)
        shift = m_new
    p = jnp.exp(s_t - shift).astype(BF16)
    acc = jnp.exp(m - m_new) * acc + _dot(v_t, p)
    return m_new, acc


def _flash_init_t(queries):
    return (jnp.full((1, queries), NEG, F32), jnp.zeros((V_ROWS, queries), F32))


def _values_t(v):
    b, s, w = v.shape
    heads = w // HEAD_DIM
    vt = v.reshape(b, s, heads, HEAD_DIM).transpose(0, 2, 3, 1)
    ones = jnp.ones((b, heads, 1, s), v.dtype)
    pad = jnp.zeros((b, heads, V_ROWS - HEAD_DIM - 1, s), v.dtype)
    return jnp.concatenate([vt, ones, pad], axis=2)


def _nsa_slc_t_kernel(q_ref, sb_ref, k0_ref, k1_ref, v_ref, eye_ref, o_ref, sa_ref, sc_ref):
    tq, tk = ATT_Q, SLC_K
    q0 = pl.program_id(1) * tq
    sb = sb_ref[...]
    lo_half = lax.broadcasted_iota(jnp.int32, (tq, LANES), 1) < HEAD_DIM
    cols = 4 * tq
    n_before = q0 // tk
    k_refs = (k0_ref, k1_ref)
    slots = (sa_ref, sc_ref)

    qq = []
    for g in range(2):
        parts = []
        for t in range(4):
            qt = q_ref[:, t * LANES:(t + 1) * LANES]
            parts.append(jnp.where(lo_half, qt, sb) if g == 0 else jnp.where(lo_half, sb, qt))
        qq.append(jnp.concatenate(parts, axis=0))

    def issue(j, slot):
        off = pl.multiple_of(j * tk, tk)
        for g in range(2):
            slots[slot][g] = _dot_nt(k_refs[g][pl.ds(off, tk), :], qq[g])

    def consume(j, carry, slot, mask=None):
        off = pl.multiple_of(j * tk, tk)
        return tuple(_flash_step_t(carry[g], slots[slot][g], v_ref[g, :, pl.ds(off, tk)], mask=mask)
                     for g in range(2))

    def step(jj, carry):
        j = 2 * jj
        issue(j + 1, 1)
        carry = consume(j, carry, 0)
        issue(j + 2, 0)
        return consume(j + 1, carry, 1)

    issue(0, 0)
    carry = lax.fori_loop(0, n_before // 2, step, (_flash_init_t(cols), _flash_init_t(cols)))

    q_in_chunk = q0 - n_before * tk
    query = lax.broadcasted_iota(jnp.int32, (tk, cols), 1) & (tq - 1)
    causal = lax.broadcasted_iota(jnp.int32, (tk, cols), 0) - query <= q_in_chunk

    def even_tail(carry):
        return consume(n_before, carry, 0, causal)

    def odd_tail(carry):
        issue(n_before, 1)
        carry = consume(n_before - 1, carry, 0)
        return consume(n_before, carry, 1, causal)

    carry = lax.cond((n_before & 1) == 0, even_tail, odd_tail, carry)

    outs = [(acc[0:HEAD_DIM] / acc[HEAD_DIM:HEAD_DIM + 1]).astype(BF16) for (_, acc) in carry]
    eye = eye_ref[...]
    for t in range(4):
        cs = slice(t * tq, (t + 1) * tq)
        both = jnp.concatenate([outs[0][:, cs], outs[1][:, cs]], axis=0)
        o_ref[:, t * LANES:(t + 1) * LANES] = _dot_nt(eye, both).astype(BF16)


def _nsa_selected_t(qa3, sb, k_aug0, k_aug1, v_t):
    b, s, _ = qa3.shape
    blk = lambda i, j: (i, j, 0)
    whole = lambda i, j: (i, 0, 0)
    eye = jnp.asarray(np.eye(ATT_Q, dtype=np.float32), BF16)
    return pl.pallas_call(
        _nsa_slc_t_kernel,
        grid=(b, s // ATT_Q),
        in_specs=[
            pl.BlockSpec((None, ATT_Q, 512), blk),
            pl.BlockSpec((None, ATT_Q, LANES), blk),
            pl.BlockSpec((None, s, LANES), whole),
            pl.BlockSpec((None, s, LANES), whole),
            pl.BlockSpec((None, 2, V_ROWS, s), lambda i, j: (i, 0, 0, 0)),
            pl.BlockSpec((ATT_Q, ATT_Q), lambda i, j: (0, 0)),
        ],
        out_specs=pl.BlockSpec((None, ATT_Q, 512), blk),
        out_shape=jax.ShapeDtypeStruct((b, s, 512), BF16),
        scratch_shapes=[pltpu.VMEM((2, SLC_K, 4 * ATT_Q), F32), pltpu.VMEM((2, SLC_K, 4 * ATT_Q), F32)],
        compiler_params=_cp(("parallel", "parallel")),
        name="nsa_selected",
    )(qa3, sb, k_aug0, k_aug1, v_t, eye)


def _nsa_win_kernel(q_ref, k_ref, v_ref, eye_ref, o_ref):
    tq = ATT_Q
    span = WINDOW + tq
    q0 = pl.program_id(1) * tq
    start = pl.multiple_of(jnp.maximum(q0 - WINDOW, 0), tq)
    k = k_ref[pl.ds(start, span), :]
    cols = 4 * tq
    lo_half = lax.broadcasted_iota(jnp.int32, (tq, LANES), 1) < HEAD_DIM
    query = lax.broadcasted_iota(jnp.int32, (span, cols), 1) & (tq - 1)
    dist = (q0 - start) + query - lax.broadcasted_iota(jnp.int32, (span, cols), 0)
    mask = (dist >= 0) & (dist < WINDOW)
    scores = []
    for g in range(2):
        keep = lo_half if g == 0 else jnp.logical_not(lo_half)
        parts = []
        for t in range(4):
            qt = q_ref[:, t * LANES:(t + 1) * LANES]
            parts.append(jnp.where(keep, qt, jnp.zeros_like(qt)))
        scores.append(_dot_nt(k, jnp.concatenate(parts, axis=0)))
    outs = []
    for g in range(2):
        s_t = jnp.where(mask, scores[g], NEG)
        p = jnp.exp(s_t - jnp.max(s_t, axis=0, keepdims=True)).astype(BF16)
        acc = _dot(v_ref[g, :, pl.ds(start, span)], p)
        outs.append((acc[0:HEAD_DIM] / acc[HEAD_DIM:HEAD_DIM + 1]).astype(BF16))
    eye = eye_ref[...]
    for t in range(4):
        cs = slice(t * tq, (t + 1) * tq)
        both = jnp.concatenate([outs[0][:, cs], outs[1][:, cs]], axis=0)
        o_ref[:, t * LANES:(t + 1) * LANES] = _dot_nt(eye, both).astype(BF16)


def _nsa_window(qa3, k2, v_t):
    b, s, _ = qa3.shape
    blk = lambda i, j: (i, j, 0)
    whole = lambda i, j: (i, 0, 0)
    eye = jnp.asarray(np.eye(ATT_Q, dtype=np.float32), BF16)
    return pl.pallas_call(
        _nsa_win_kernel,
        grid=(b, s // ATT_Q),
        in_specs=[
            pl.BlockSpec((None, ATT_Q, 512), blk),
            pl.BlockSpec((None, s, LANES), whole),
            pl.BlockSpec((None, 2, V_ROWS, s), lambda i, j: (i, 0, 0, 0)),
            pl.BlockSpec((ATT_Q, ATT_Q), lambda i, j: (0, 0)),
        ],
        out_specs=pl.BlockSpec((None, ATT_Q, 512), blk),
        out_shape=jax.ShapeDtypeStruct((b, s, 512), BF16),
        compiler_params=_cp(("parallel", "parallel")),
        name="nsa_window",
    )(qa3, k2, v_t, eye)


def _fox_t_kernel(q_ref, k_ref, v_ref, cr_ref, o_ref, sa_ref, sb_ref):
    tq, tk = FOX_Q, FOX_K
    q0 = pl.program_id(1) * tq
    n_before = q0 // tk
    n_diag = tq // tk
    for pair in range(FOX_HEADS // 2):
        heads = (2 * pair, 2 * pair + 1)
        qs = tuple(q_ref[:, h * LANES:(h + 1) * LANES] for h in heads)
        cq = tuple(cr_ref[h:h + 1, pl.ds(pl.multiple_of(q0, tq), tq)] for h in heads)

        def scores(j, i, q_lo=0):
            off = pl.multiple_of(j * tk, tk)
            h = heads[i]
            return _dot_nt(k_ref[pl.ds(off, tk), h * LANES:(h + 1) * LANES], qs[i][q_lo:])

        slots = (sa_ref, sb_ref)

        def consume(j, carry, slot):
            off = pl.multiple_of(j * tk, tk)
            return tuple(_flash_step_t(carry[i], slots[slot][i], v_ref[heads[i], :, pl.ds(off, tk)],
                                       query_bias=cq[i]) for i in range(2))

        def step(jj, carry):
            j = 2 * jj
            for i in range(2):
                sb_ref[i] = scores(j + 1, i)
            carry = consume(j, carry, 0)
            for i in range(2):
                sa_ref[i] = scores(j + 2, i)
            return consume(j + 1, carry, 1)

        for i in range(2):
            sa_ref[i] = scores(0, i)
        carry = lax.fori_loop(0, n_before // 2, step, (_flash_init_t(tq), _flash_init_t(tq)))

        late = [[scores(n_before + d, i, d * tk) for i in range(2)] for d in range(1, n_diag)]
        for d in range(n_diag):
            q_lo = d * tk
            off = pl.multiple_of((n_before + d) * tk, tk)
            shape = (tk, tq - q_lo)
            mask = (lax.broadcasted_iota(jnp.int32, shape, 0) + (d * tk - q_lo)
                    <= lax.broadcasted_iota(jnp.int32, shape, 1))
            out = []
            for i in range(2):
                m, acc = carry[i]
                s_t = sa_ref[i] if d == 0 else late[d - 1][i]
                new = _flash_step_t((m[:, q_lo:], acc[:, q_lo:]), s_t, v_ref[heads[i], :, pl.ds(off, tk)],
                                    query_bias=cq[i][:, q_lo:], mask=mask)
                if q_lo:
                    new = tuple(jnp.concatenate([old[:, :q_lo], part], axis=1)
                                for old, part in zip((m, acc), new))
                out.append(new)
            carry = tuple(out)
        for i, h in enumerate(heads):
            acc = carry[i][1]
            o_ref[h * HEAD_DIM:(h + 1) * HEAD_DIM, :] = (
                acc[0:HEAD_DIM] / acc[HEAD_DIM:HEAD_DIM + 1]).astype(BF16)


def _fox_attention_t(fox3, k_aug, v_t, cum_row):
    b, s, _ = fox3.shape
    width = FOX_HEADS * LANES
    return pl.pallas_call(
        _fox_t_kernel,
        grid=(b, s // FOX_Q),
        in_specs=[
            pl.BlockSpec((None, FOX_Q, width), lambda i, j: (i, j, 0)),
            pl.BlockSpec((None, s, width), lambda i, j: (i, 0, 0)),
            pl.BlockSpec((None, FOX_HEADS, V_ROWS, s), lambda i, j: (i, 0, 0, 0)),
            pl.BlockSpec((None, FOX_HEADS, s), lambda i, j: (i, 0, 0)),
        ],
        out_specs=pl.BlockSpec((None, FOX_HEADS * HEAD_DIM, FOX_Q), lambda i, j: (i, 0, j)),
        out_shape=jax.ShapeDtypeStruct((b, FOX_HEADS * HEAD_DIM, s), BF16),
        scratch_shapes=[pltpu.VMEM((2, FOX_K, FOX_Q), F32), pltpu.VMEM((2, FOX_K, FOX_Q), F32)],
        compiler_params=_cp(("parallel", "parallel")),
        name="fox_attention",
    )(fox3, k_aug, v_t, cum_row)


def _out_a_kernel(oc_ref, os_ref, ow_ref, ob_ref, gf_ref, gb_ref, gx_ref, x_ref, w_ref, o_ref):
    gates = _sigmoid(gf_ref[...] + gb_ref[...])
    hi, lo = _split2(gates)
    o_a = None
    for j, br in enumerate((oc_ref, os_ref, ow_ref)):
        gexp = _dot(hi, gx_ref[j]) + _dot(lo, gx_ref[j])
        term = gexp * br[...].astype(F32)
        o_a = term if o_a is None else o_a + term
    y = _dot(o_a.astype(BF16), w_ref[0:512, :]) + _dot(ob_ref[...], w_ref[512:1024, :])
    o_ref[...] = x_ref[...] + y


def _out_a(oc, os_, ow, ob, gf, gate_bias_row, gate_expand, x2d, w_out_perm):
    t = x2d.shape[0]
    tm = ROW_TILE
    row = lambda i: (i, 0)
    c2 = lambda i: (0, 0)
    return pl.pallas_call(
        _out_a_kernel,
        grid=(t // tm,),
        in_specs=[
            pl.BlockSpec((tm, 512), row),
            pl.BlockSpec((tm, 512), row),
            pl.BlockSpec((tm, 512), row),
            pl.BlockSpec((tm, 512), row),
            pl.BlockSpec((tm, LANES), row),
            pl.BlockSpec((1, LANES), c2),
            pl.BlockSpec((3, LANES, 512), lambda i: (0, 0, 0)),
            pl.BlockSpec((tm, D_MODEL), row),
            pl.BlockSpec((D_MODEL, D_MODEL), c2),
        ],
        out_specs=pl.BlockSpec((tm, D_MODEL), row),
        out_shape=jax.ShapeDtypeStruct((t, D_MODEL), F32),
        compiler_params=_cp(("parallel",)),
        name="out_a",
    )(oc, os_, ow, ob, gf, gate_bias_row, gate_expand, x2d, w_out_perm)


def _swiglu_body(xn, wg_ref, wu_ref, wd_ref, h_ref):
    for c in range(D_FF // FF_CHUNK):
        cols = slice(c * FF_CHUNK, (c + 1) * FF_CHUNK)
        g = _dot(xn, wg_ref[:, cols])
        u = _dot(xn, wu_ref[:, cols])
        h_ref[:, cols] = (g * _sigmoid(g) * u).astype(BF16)
    return _dot(h_ref[...], wd_ref[...])


def _ffn_dense_kernel(x_ref, g_ref, wg_ref, wu_ref, wd_ref, o_ref, h_ref):
    x = x_ref[...]
    xn = _rms_rows(x, g_ref[...]).astype(BF16)
    o_ref[...] = x + _swiglu_body(xn, wg_ref, wu_ref, wd_ref, h_ref)


def _ffn_dense(x2d, norm_g, wg, wu, wd):
    t = x2d.shape[0]
    tm = ROW_TILE
    c2 = lambda i: (0, 0)
    return pl.pallas_call(
        _ffn_dense_kernel,
        grid=(t // tm,),
        in_specs=[
            pl.BlockSpec((tm, D_MODEL), lambda i: (i, 0)),
            pl.BlockSpec((1, D_MODEL), c2),
            pl.BlockSpec((D_MODEL, D_FF), c2),
            pl.BlockSpec((D_MODEL, D_FF), c2),
            pl.BlockSpec((D_FF, D_MODEL), c2),
        ],
        out_specs=pl.BlockSpec((tm, D_MODEL), lambda i: (i, 0)),
        out_shape=jax.ShapeDtypeStruct((t, D_MODEL), F32),
        scratch_shapes=[pltpu.VMEM((tm, D_FF), BF16)],
        compiler_params=_cp(("parallel",)),
        name="ffn_dense",
    )(x2d, norm_g, wg, wu, wd)


def _ffn_expert_kernel(te_ref, nu_ref, cnt_ref, src_now_ref, src_next_ref, dst_ref,
                       xn_hbm, wg_ref, wu_ref, wd_ref, y_hbm, xbuf, ybuf, h_ref, gsem, ssem):
    tm = EXPERT_TILE
    i = pl.program_id(0)
    n_used = nu_ref[0]
    slot = i & 1

    def gather(idx_ref, s):
        def body(r, _):
            pltpu.make_async_copy(xn_hbm.at[idx_ref[0, 0, r]], xbuf.at[s, r], gsem.at[s]).start()
            return 0
        lax.fori_loop(0, tm, body, 0, unroll=8)

    def wait_rows(copy_of_row, count):
        def body(r, _):
            copy_of_row(r).wait()
            return 0
        lax.fori_loop(0, count, body, 0)

    def scatter_copy(s, r, row):
        return pltpu.make_async_copy(ybuf.at[s, r], y_hbm.at[row], ssem.at[s])

    def wait_scatter(tile, s):
        @pl.when(cnt_ref[tile] == tm)
        def _():
            pltpu.make_async_copy(ybuf.at[s], y_hbm.at[pl.ds(0, tm)], ssem.at[s]).wait()

        @pl.when(cnt_ref[tile] < tm)
        def _():
            wait_rows(lambda r: scatter_copy(s, r, 0), cnt_ref[tile])

    @pl.when(i == 0)
    def _():
        gather(src_now_ref, 0)

    @pl.when(i + 1 < n_used)
    def _():
        gather(src_next_ref, 1 - slot)

    @pl.when(i < n_used)
    def _():
        pltpu.make_async_copy(xn_hbm.at[pl.ds(0, tm)], xbuf.at[slot], gsem.at[slot]).wait()

        @pl.when(i >= 2)
        def _():
            wait_scatter(i - 2, slot)

        x = xbuf[slot].reshape(tm, D_MODEL)
        y = _swiglu_body(x, wg_ref, wu_ref, wd_ref, h_ref)
        ybuf[slot] = y.astype(BF16).reshape(tm, D_MODEL // LANES, LANES)

        def send(r, _):
            scatter_copy(slot, r, dst_ref[0, 0, r]).start()
            return 0

        @pl.when(cnt_ref[i] == tm)
        def _():
            lax.fori_loop(0, tm, send, 0, unroll=8)

        @pl.when(cnt_ref[i] < tm)
        def _():
            lax.fori_loop(0, cnt_ref[i], send, 0)

    @pl.when(i == pl.num_programs(0) - 1)
    def _():
        @pl.when(n_used >= 2)
        def _():
            wait_scatter(n_used - 2, (n_used - 2) & 1)
        wait_scatter(n_used - 1, (n_used - 1) & 1)


def _ffn_experts(tile_expert, n_used, tile_count, src_tok, dst_row, xn3, wg, wu, wd):
    tm = EXPERT_TILE
    n_tiles = src_tok.shape[0]
    t = xn3.shape[0]
    wsel = lambda i, te, nu, cnt: (te[i], 0, 0)
    idx = lambda i, te, nu, cnt: (i, 0, 0)
    idx_next = lambda i, te, nu, cnt: (jnp.minimum(i + 1, n_tiles - 1), 0, 0)
    smem_idx = lambda m: pl.BlockSpec((1, 1, tm), m, memory_space=pltpu.SMEM)
    rows = D_MODEL // LANES
    grid_spec = pltpu.PrefetchScalarGridSpec(
        num_scalar_prefetch=3,
        grid=(n_tiles,),
        in_specs=[
            smem_idx(idx),
            smem_idx(idx_next),
            smem_idx(idx),
            pl.BlockSpec(memory_space=pl.ANY),
            pl.BlockSpec((None, D_MODEL, D_FF), wsel),
            pl.BlockSpec((None, D_MODEL, D_FF), wsel),
            pl.BlockSpec((None, D_FF, D_MODEL), wsel),
        ],
        out_specs=pl.BlockSpec(memory_space=pl.ANY),
        scratch_shapes=[
            pltpu.VMEM((2, tm, rows, LANES), BF16),
            pltpu.VMEM((2, tm, rows, LANES), BF16),
            pltpu.VMEM((tm, D_FF), BF16),
            pltpu.SemaphoreType.DMA((2,)),
            pltpu.SemaphoreType.DMA((2,)),
        ],
    )
    return pl.pallas_call(
        _ffn_expert_kernel,
        grid_spec=grid_spec,
        out_shape=jax.ShapeDtypeStruct((TOP_K * t, rows, LANES), BF16),
        compiler_params=_cp(("arbitrary",), has_side_effects=True),
        name="ffn_experts",
    )(tile_expert, n_used, tile_count, src_tok, src_tok, dst_row, xn3, wg, wu, wd)


def _proj_c_kernel(x_ref, g_ref, w_ref, wa_ref, ba_ref, tri_ref, qk_ref, v_ref, r_ref, la_ref):
    xn = _rms_rows(x_ref[...], g_ref[...]).astype(BF16)
    for c in range(2):
        qk_ref[:, 512 * c:512 * (c + 1)] = _dot(xn, w_ref[:, 512 * c:512 * (c + 1)]).astype(BF16)
    for c in range(2):
        v_ref[:, 512 * c:512 * (c + 1)] = _dot(xn, w_ref[:, 1024 + 512 * c:1536 + 512 * c]).astype(BF16)
    for c in range(2):
        r_ref[:, 512 * c:512 * (c + 1)] = _dot(xn, w_ref[:, 2048 + 512 * c:2560 + 512 * c]).astype(BF16)
    a1 = _dot(xn, w_ref[:, 3072:3200])
    hi, lo = _split2(a1)
    wa = wa_ref[...]
    pre = _dot(hi, wa) + _dot(lo, wa) + ba_ref[...]
    la = _log_sigmoid(pre) * (1.0 / GLA_TAU)
    tri = tri_ref[...]
    c_hi, c_mid, c_lo = _split3(la)
    la_ref[...] = _dot(tri, c_hi) + _dot(tri, c_mid) + _dot(tri, c_lo)


def _proj_c(x2d, norm_g, w_pad, wa_pad, ba_row):
    t = x2d.shape[0]
    tm = ROW_TILE
    idx = np.arange(tm)
    tri = ((idx[:, None] // GLA_CHUNK == idx[None, :] // GLA_CHUNK) & (idx[:, None] >= idx[None, :])).astype(np.float32)
    row = lambda i: (i, 0)
    c2 = lambda i: (0, 0)
    return pl.pallas_call(
        _proj_c_kernel,
        grid=(t // tm,),
        in_specs=[
            pl.BlockSpec((tm, D_MODEL), row),
            pl.BlockSpec((1, D_MODEL), c2),
            pl.BlockSpec((D_MODEL, 3200), c2),
            pl.BlockSpec((LANES, GLA_DK), c2),
            pl.BlockSpec((1, GLA_DK), c2),
            pl.BlockSpec((tm, tm), c2),
        ],
        out_specs=[
            pl.BlockSpec((tm, 1024), row),
            pl.BlockSpec((tm, 1024), row),
            pl.BlockSpec((tm, 1024), row),
            pl.BlockSpec((tm, GLA_DK), row),
        ],
        out_shape=[
            jax.ShapeDtypeStruct((t, 1024), BF16),
            jax.ShapeDtypeStruct((t, 1024), BF16),
            jax.ShapeDtypeStruct((t, 1024), BF16),
            jax.ShapeDtypeStruct((t, GLA_DK), F32),
        ],
        compiler_params=_cp(("parallel",)),
        name="proj_c",
    )(x2d, norm_g, w_pad, wa_pad, ba_row, jnp.asarray(tri, BF16))


GLA_STEP_HEADS = 2


def _gla_kernel(q_ref, k_ref, v_ref, b_ref, eye_ref, o_ref, st_ref):
    c_len, sub = GLA_CHUNK, GLA_SUB
    n_sub = c_len // sub
    hs = range(GLA_STEP_HEADS)
    dk = q_ref.shape[1] // GLA_STEP_HEADS
    dv = v_ref.shape[1] // GLA_STEP_HEADS
    scale = dk ** -0.5
    eye = eye_ref[...]
    st_ref[...] = jnp.zeros_like(st_ref)
    row = lax.broadcasted_iota(jnp.int32, (c_len, c_len), 0)
    col = lax.broadcasted_iota(jnp.int32, (c_len, c_len), 1)
    col_in_sub = col & (sub - 1)
    same_sub_causal = ((row >> 4) == (col >> 4)) & ((row & (sub - 1)) >= col_in_sub)
    earlier_sub = (col >> 4) < (row >> 4)

    def chunk(c, _):
        rows = pl.ds(pl.multiple_of(c * c_len, c_len), c_len)
        q = [q_ref[rows, h * dk:(h + 1) * dk].astype(F32) * scale for h in hs]
        k = [k_ref[rows, h * dk:(h + 1) * dk].astype(F32) for h in hs]
        vb = [v_ref[rows, h * dv:(h + 1) * dv] for h in hs]
        b = [b_ref[rows, h * dk:(h + 1) * dk] for h in hs]
        b_last = [b[h][c_len - 1:c_len, :] for h in hs]
        st_t = [st_ref[h] for h in hs]

        v_t = [_dot_nt(eye, vb[h]).astype(BF16) for h in hs]
        inter = [_dot_nt((q[h] * jnp.exp(b[h])).astype(BF16), st_t[h].astype(BF16)) for h in hs]
        att_rows = [[jnp.zeros((sub, c_len), F32)] for _ in hs]
        for i in range(1, n_sub):
            for h in hs:
                r = b[h][i * sub:i * sub + 1, :]
                qi = (q[h][i * sub:(i + 1) * sub] * jnp.exp(b[h][i * sub:(i + 1) * sub] - r)).astype(BF16)
                kj = (k[h] * jnp.exp(jnp.minimum(r - b[h], 0.0))).astype(BF16)
                att_rows[h].append(_dot_nt(qi, kj))
        for h in hs:
            kd = (k[h] * jnp.exp(b_last[h] - b[h])).astype(BF16)
            st_ref[h] = st_t[h] * jnp.exp(b_last[h]) + _dot(v_t[h], kd)

        att_d = [jnp.zeros((c_len, c_len), F32) for _ in hs]
        for j in range(sub):
            def rows_j(x):
                return jnp.concatenate(
                    [jnp.broadcast_to(x[s * sub + j:s * sub + j + 1, :], (sub, x.shape[1]))
                     for s in range(n_sub)], axis=0)
            for h in hs:
                e = jnp.exp(jnp.minimum(b[h] - rows_j(b[h]), 0.0))
                a = jnp.sum(q[h] * e * rows_j(k[h]), axis=-1, keepdims=True)
                att_d[h] = jnp.where(col_in_sub == j, a, att_d[h])
        for h in hs:
            att = jnp.where(earlier_sub, jnp.concatenate(att_rows[h], axis=0),
                            jnp.where(same_sub_causal, att_d[h], 0.0)).astype(BF16)
            o_ref[rows, h * dv:(h + 1) * dv] = (inter[h] + _dot(att, vb[h])).astype(BF16)
        return 0

    def two_chunks(cc, carry):
        chunk(2 * cc, carry)
        return chunk(2 * cc + 1, carry)

    lax.fori_loop(0, q_ref.shape[0] // (2 * c_len), two_chunks, 0)


def _gla(qk3, v3, b3):
    b, s, _ = qk3.shape
    n = GLA_STEP_HEADS
    dk = GLA_DK // GLA_HEADS
    dv = GLA_DV // GLA_HEADS
    steps = GLA_HEADS // n
    eye = jnp.asarray(np.eye(dv, dtype=np.float32), BF16)
    return pl.pallas_call(
        _gla_kernel,
        grid=(b, steps),
        in_specs=[
            pl.BlockSpec((None, s, n * dk), lambda i, h: (i, 0, h)),
            pl.BlockSpec((None, s, n * dk), lambda i, h: (i, 0, steps + h)),
            pl.BlockSpec((None, s, n * dv), lambda i, h: (i, 0, h)),
            pl.BlockSpec((None, s, n * dk), lambda i, h: (i, 0, h)),
            pl.BlockSpec((dv, dv), lambda i, h: (0, 0)),
        ],
        out_specs=pl.BlockSpec((None, s, n * dv), lambda i, h: (i, 0, h)),
        out_shape=jax.ShapeDtypeStruct((b, s, GLA_DV), BF16),
        scratch_shapes=[pltpu.VMEM((n, dv, dk), F32)],
        compiler_params=_cp(("parallel", "parallel")),
        name="gla",
    )(qk3, qk3, v3, b3, eye)


def _out_c_kernel(o_ref, r_ref, x_ref, on_ref, bd_ref, w_ref, mg_ref, rt_ref,
                  x3_ref, xn_ref, e1_ref, e2_ref, w1_ref, w2_ref):
    dv = GLA_DV // GLA_HEADS
    bd = bd_ref[...]
    parts = []
    for h in range(GLA_HEADS):
        cols = slice(h * dv, (h + 1) * dv)
        oh = _seg_norm(o_ref[:, cols].astype(F32), on_ref[...], bd)
        r = r_ref[:, cols].astype(F32)
        parts.append((oh * (r * _sigmoid(r))).astype(BF16))
    x3 = x_ref[...] + _dot(jnp.concatenate(parts, axis=1), w_ref[...])
    x3_ref[...] = x3

    xn = _rms_rows(x3, mg_ref[...])
    xn_ref[...] = xn.astype(BF16).reshape(xn_ref.shape)
    xh, xl = _split2(xn)
    logits = _dot(xh, rt_ref[0]) + _dot(xl, rt_ref[0]) + _dot(xh, rt_ref[1])
    lane = lax.broadcasted_iota(jnp.int32, logits.shape, 1)
    logits = jnp.where(lane < N_EXPERTS, logits, -3e38)
    v1 = jnp.max(logits, axis=-1, keepdims=True)
    i1 = jnp.min(jnp.where(logits == v1, lane, LANES), axis=-1, keepdims=True)
    rest = jnp.where(lane == i1, -3e38, logits)
    v2 = jnp.max(rest, axis=-1, keepdims=True)
    i2 = jnp.min(jnp.where(rest == v2, lane, LANES), axis=-1, keepdims=True)
    ex = jnp.exp(v2 - v1)
    den = 1.0 + ex
    e1_ref[...] = jnp.broadcast_to(i1, logits.shape)
    e2_ref[...] = jnp.broadcast_to(i2, logits.shape)
    w1_ref[...] = jnp.broadcast_to(1.0 / den, logits.shape)
    w2_ref[...] = jnp.broadcast_to(ex / den, logits.shape)


def _out_c(o2d, r2d, x2d, onorm_row, w_out, mnorm_row, router2):
    t = x2d.shape[0]
    tm = ROW_TILE
    dv = GLA_DV // GLA_HEADS
    bd = jnp.asarray(_block_diag_mean(dv, dv), BF16)
    row = lambda i: (i, 0)
    c2 = lambda i: (0, 0)
    slab = pl.BlockSpec((tm, LANES), row)
    return pl.pallas_call(
        _out_c_kernel,
        grid=(t // tm,),
        in_specs=[
            pl.BlockSpec((tm, GLA_DV), row),
            pl.BlockSpec((tm, GLA_DV), row),
            pl.BlockSpec((tm, D_MODEL), row),
            pl.BlockSpec((1, dv), c2),
            pl.BlockSpec((dv, dv), c2),
            pl.BlockSpec((D_MODEL, D_MODEL), c2),
            pl.BlockSpec((1, D_MODEL), c2),
            pl.BlockSpec((2, D_MODEL, LANES), lambda i: (0, 0, 0)),
        ],
        out_specs=[
            pl.BlockSpec((tm, D_MODEL), row),
            pl.BlockSpec((tm, D_MODEL // LANES, LANES), lambda i: (i, 0, 0)),
            slab, slab, slab, slab,
        ],
        out_shape=[
            jax.ShapeDtypeStruct((t, D_MODEL), F32),
            jax.ShapeDtypeStruct((t, D_MODEL // LANES, LANES), BF16),
            jax.ShapeDtypeStruct((t, LANES), jnp.int32),
            jax.ShapeDtypeStruct((t, LANES), jnp.int32),
            jax.ShapeDtypeStruct((t, LANES), F32),
            jax.ShapeDtypeStruct((t, LANES), F32),
        ],
        compiler_params=_cp(("parallel",)),
        name="out_c_router",
    )(o2d, r2d, x2d, onorm_row, bd, w_out, mnorm_row, router2)


def _combine_kernel(ya_ref, yb_ref, x_ref, w1_ref, w2_ref, o_ref):
    tm = x_ref.shape[0]
    reps = D_MODEL // LANES
    w1 = jnp.concatenate([w1_ref[...]] * reps, axis=1)
    w2 = jnp.concatenate([w2_ref[...]] * reps, axis=1)
    ya = ya_ref[...].reshape(tm, D_MODEL).astype(F32)
    yb = yb_ref[...].reshape(tm, D_MODEL).astype(F32)
    o_ref[...] = x_ref[...] + w1 * ya + w2 * yb


def _moe_combine(ys3, x2d, w1b, w2b):
    t = x2d.shape[0]
    tm = ROW_TILE
    row = lambda i: (i, 0)
    rows = D_MODEL // LANES
    second = t // tm
    return pl.pallas_call(
        _combine_kernel,
        grid=(t // tm,),
        in_specs=[
            pl.BlockSpec((tm, rows, LANES), lambda i: (i, 0, 0)),
            pl.BlockSpec((tm, rows, LANES), lambda i: (second + i, 0, 0)),
            pl.BlockSpec((tm, D_MODEL), row),
            pl.BlockSpec((tm, LANES), row),
            pl.BlockSpec((tm, LANES), row),
        ],
        out_specs=pl.BlockSpec((tm, D_MODEL), row),
        out_shape=jax.ShapeDtypeStruct((t, D_MODEL), F32),
        compiler_params=_cp(("parallel",)),
        name="moe_combine",
    )(ys3, ys3, x2d, w1b, w2b)


def _nsa_head_perm():
    order = []
    for t in range(4):
        order += [t, 4 + t]
    cols = np.concatenate([np.arange(h * HEAD_DIM, (h + 1) * HEAD_DIM) for h in order])
    return order, cols


def _layer_even(x, a_norm, a_w_in, a_q_norm, a_k_norm, a_pe_k, a_pe_v, ck1, ck2, cv1, cv2,
                gate_bias, fq_norm, fk_norm, f_bias, a_w_out, f_norm, f_wg, f_wu, f_wd):
    b, s, d = x.shape
    t = b * s
    x2d = x.reshape(t, d)
    order, qcols = _nsa_head_perm()

    def head_tiles(cols0):
        wq = a_w_in[:, cols0:cols0 + 512].reshape(d, FOX_HEADS, HEAD_DIM)
        return jnp.pad(wq, ((0, 0), (0, 0), (0, LANES - HEAD_DIM))).reshape(d, FOX_HEADS * LANES)

    w = jnp.concatenate([
        jnp.take(a_w_in, jnp.asarray(qcols), axis=1),
        a_w_in[:, 512:1280],
        head_tiles(1304),
        head_tiles(1816),
        a_w_in[:, 2328:2840],
        a_w_in[:, 1280:1304],
        a_w_in[:, 2840:2848],
        jnp.zeros((d, LANES - 32), F32),
    ], axis=1).astype(BF16)

    tile2 = lambda g: jnp.concatenate([g, g])
    lo_only = lambda g: jnp.concatenate([g, jnp.zeros((LANES - HEAD_DIM,), F32)])
    q_bias_lanes = jnp.zeros((LANES,), F32).at[FOX_BIAS_LANE:FOX_BIAS_LANE + 3].set(-1.0)
    head_gains = jnp.stack([tile2(a_q_norm), tile2(a_k_norm), lo_only(fq_norm), lo_only(fk_norm), q_bias_lanes]
                           + [jnp.zeros((LANES,), F32)] * 3)
    cos, sin = _rope_tables(np.arange(s))
    qa, kv6, fox, gf = _proj_a(x2d, a_norm.reshape(1, d), w, head_gains, cos, sin, s)

    fox3 = fox.reshape(b, s, FOX_SLAB)
    bias_row = jnp.zeros((1, LANES), F32).at[0, 24:32].set(f_bias)
    cum, k_aug = _fox_cumlog(gf.reshape(b, s, LANES), bias_row, fox3)
    cum_row = cum[:, :, 24:32].transpose(0, 2, 1)
    o_b = _fox_attention_t(fox3, k_aug, _values_t(fox3[:, :, 2048:2560]), cum_row).transpose(0, 2, 1)

    ncp = s // CMP_STRIDE
    n_cmp = (s - CMP_LEN) // CMP_STRIDE + 1
    n_slc = s // SLC_BLOCK
    kv3 = kv6.reshape(b, s, 768)
    kx = kv3[:, :, 0:128].reshape(b, ncp, CMP_STRIDE * LANES)
    vx = kv3[:, :, 128:256].reshape(b, ncp, CMP_STRIDE * LANES)

    def pe_rows(pe):
        p2 = jnp.concatenate([pe, pe], axis=1)
        return p2.reshape(2, CMP_STRIDE * LANES)

    def w1_blocks(w1):
        w4 = w1.reshape(2, CMP_STRIDE, HEAD_DIM, HEAD_DIM)
        z = jnp.zeros_like(w4)
        top = jnp.concatenate([w4, z], axis=-1)
        bot = jnp.concatenate([z, w4], axis=-1)
        return jnp.stack([top, bot], axis=2).reshape(2, CMP_STRIDE * LANES, LANES).astype(BF16)

    def w2_block(w2):
        z = jnp.zeros_like(w2)
        return jnp.concatenate([jnp.concatenate([w2, z], 1), jnp.concatenate([z, w2], 1)], 0).astype(BF16)

    ccos, csin = _rope_tables(np.arange(ncp) * CMP_STRIDE + CMP_LEN - 1)
    kcmp, vcmp = _nsa_compress(kx, vx, pe_rows(a_pe_k), pe_rows(a_pe_v), w1_blocks(ck1), w2_block(ck2),
                               w1_blocks(cv1), w2_block(cv2), tile2(a_k_norm).reshape(1, LANES), ccos, csin)

    c_start = np.arange(ncp) * CMP_STRIDE
    s_start = np.arange(64) * SLC_BLOCK
    ov = np.maximum(np.minimum(c_start[:, None] + CMP_LEN, s_start[None, :] + SLC_BLOCK)
                    - np.maximum(c_start[:, None], s_start[None, :]), 0).astype(np.float32) / CMP_LEN
    ov[n_cmp:, :] = 0.0
    ov[:, n_slc:] = 0.0
    qa3 = qa.reshape(b, s, 512)
    o_c, sb = _nsa_cmp_select(qa3, kcmp, vcmp, jnp.asarray(ov.T, BF16), n_cmp, n_slc)

    onehot = jnp.asarray((np.arange(s)[:, None] // SLC_BLOCK) == np.arange(64)[None, :], BF16)
    onehot = jnp.broadcast_to(onehot[None], (b, s, 64))
    ksl = kv3[:, :, 256:384]
    k_aug0 = jnp.concatenate([ksl[:, :, :64], onehot], axis=-1)
    k_aug1 = jnp.concatenate([onehot, ksl[:, :, 64:]], axis=-1)
    o_s = _nsa_selected_t(qa3, sb, k_aug0, k_aug1, _values_t(kv3[:, :, 384:512]))

    o_w = _nsa_window(qa3, kv3[:, :, 512:640], _values_t(kv3[:, :, 640:768]))

    gexp = np.zeros((3, LANES, 512), np.float32)
    for tile_i in range(4):
        for half, h in enumerate((tile_i, 4 + tile_i)):
            for j in range(3):
                gexp[j, h * 3 + j, tile_i * LANES + half * HEAD_DIM:tile_i * LANES + (half + 1) * HEAD_DIM] = 1.0
    gate_bias_row = jnp.zeros((1, LANES), F32).at[0, 0:24].set(gate_bias)
    w_out = jnp.concatenate([jnp.take(a_w_out[:512], jnp.asarray(qcols), axis=0), a_w_out[512:]], 0).astype(BF16)
    x1 = _out_a(o_c.reshape(t, 512), o_s.reshape(t, 512), o_w.reshape(t, 512), o_b.reshape(t, 512),
                gf, gate_bias_row, jnp.asarray(gexp, BF16), x2d, w_out)

    x2 = _ffn_dense(x1, f_norm.reshape(1, d), f_wg.astype(BF16), f_wu.astype(BF16), f_wd.astype(BF16))
    return x2.reshape(b, s, d)


def _layer_odd(x, c_norm, c_w_in, c_w_a2, c_b_a, c_o_norm, c_w_out, m_norm, m_router, m_wg, m_wu, m_wd):
    b, s, d = x.shape
    t = b * s
    x2d = x.reshape(t, d)
    w = jnp.pad(c_w_in, ((0, 0), (0, 3200 - c_w_in.shape[1]))).astype(BF16)
    wa = jnp.pad(c_w_a2, ((0, LANES - GLA_RANK), (0, 0))).astype(BF16)
    qk, v, r, la = _proj_c(x2d, c_norm.reshape(1, d), w, wa, c_b_a.reshape(1, GLA_DK))
    o = _gla(qk.reshape(b, s, 1024), v.reshape(b, s, 1024), la.reshape(b, s, GLA_DK))

    rt = jnp.pad(m_router, ((0, 0), (0, LANES - N_EXPERTS)))
    rt_hi = rt.astype(BF16)
    rt_lo = (rt - rt_hi.astype(F32)).astype(BF16)
    x3, xn, e1, e2, w1b, w2b = _out_c(o.reshape(t, GLA_DV), r, x2d, c_o_norm.reshape(1, -1),
                                      c_w_out.astype(BF16), m_norm.reshape(1, d), jnp.stack([rt_hi, rt_lo]))

    tm = EXPERT_TILE
    n_slot = TOP_K * t + N_EXPERTS * tm
    n_tiles = n_slot // tm
    flat_e = jnp.concatenate([e1[:, 0], e2[:, 0]])
    onehot = (flat_e[:, None] == jnp.arange(N_EXPERTS)[None, :]).astype(jnp.int32)
    counts = jnp.sum(onehot, axis=0)
    padded = ((counts + tm - 1) // tm) * tm
    ends = jnp.cumsum(padded)
    offs = ends - padded
    n_used = (ends[-1] // tm).astype(jnp.int32)
    tile_start = jnp.arange(n_tiles, dtype=jnp.int32) * tm
    tile_e = jnp.sum((tile_start[:, None] >= ends[None, :]).astype(jnp.int32), axis=1)
    last_e = jnp.sum((((n_used - 1) * tm) >= ends).astype(jnp.int32))
    tile_e = jnp.where(jnp.arange(n_tiles) < n_used, tile_e, last_e).astype(jnp.int32)
    order = jnp.argsort(flat_e, stable=True).astype(jnp.int32)
    order = jnp.concatenate([order, jnp.zeros((n_slot,), jnp.int32)])
    starts = jnp.cumsum(counts) - counts
    row_of = jnp.zeros((n_slot + TOP_K * t,), jnp.int32)
    for e in range(N_EXPERTS):
        piece = lax.dynamic_slice(order, (starts[e],), (TOP_K * t,))
        row_of = lax.dynamic_update_slice(row_of, piece, (offs[e],))
    row_of = row_of[:n_slot]
    src_tok = (row_of % t).reshape(n_tiles, 1, tm)
    dst_row = row_of.reshape(n_tiles, 1, tm)
    tile_off = tile_start - jnp.sum(jnp.where(tile_start[:, None] >= ends[None, :], padded[None, :], 0), axis=1)
    tile_count = jnp.clip(counts[jnp.minimum(tile_e, N_EXPERTS - 1)] - tile_off, 0, tm).astype(jnp.int32)
    tile_count = jnp.where(jnp.arange(n_tiles) < n_used, tile_count, 0)

    ys = _ffn_experts(tile_e, n_used.reshape(1), tile_count, src_tok, dst_row, xn,
                      m_wg.astype(BF16), m_wu.astype(BF16), m_wd.astype(BF16))
    out = _moe_combine(ys, x3, w1b, w2b)
    return out.reshape(b, s, d)


def kernel(x, a_norm, a_w_in, a_q_norm, a_k_norm, a_pe_k, a_pe_v, a_cmp_k_w1, a_cmp_k_w2, a_cmp_v_w1, a_cmp_v_w2, a_gate_bias, a_fox_q_norm, a_fox_k_norm, a_fox_f_bias, a_w_out, f_norm, f_w_gate, f_w_up, f_w_down, c_norm, c_w_in, c_w_a2, c_b_a, c_o_norm, c_w_out, m_norm, m_router, m_w_gate, m_w_up, m_w_down):
    x = _layer_even(x, a_norm[0], a_w_in[0], a_q_norm[0], a_k_norm[0], a_pe_k[0], a_pe_v[0],
                    a_cmp_k_w1[0], a_cmp_k_w2[0], a_cmp_v_w1[0], a_cmp_v_w2[0], a_gate_bias[0],
                    a_fox_q_norm[0], a_fox_k_norm[0], a_fox_f_bias[0], a_w_out[0],
                    f_norm[0], f_w_gate[0], f_w_up[0], f_w_down[0])
    x = _layer_odd(x, c_norm[0], c_w_in[0], c_w_a2[0], c_b_a[0], c_o_norm[0], c_w_out[0],
                   m_norm[0], m_router[0], m_w_gate[0], m_w_up[0], m_w_down[0])
    return x
```

```python
import functools

import numpy as np
import jax
import jax.numpy as jnp
from jax import lax
from jax.experimental import pallas as pl
from jax.experimental.pallas import tpu as pltpu

F32 = jnp.float32
BF16 = jnp.bfloat16

D_MODEL = 1024
HEAD_DIM = 64
ROPE_DIM = 16
ROPE_THETA = 500000.0
NORM_EPS = 1e-6
NEG = -1e30
FORCE_SCORE = 1e6

NSA_HEADS = 8
CMP_LEN = 32
CMP_STRIDE = 16
SLC_BLOCK = 64
SLC_TOPK = 16
WINDOW = 512
FOX_HEADS = 8

GLA_HEADS = 4
GLA_DK = 512
GLA_DV = 1024
GLA_RANK = 16
GLA_TAU = 16.0
GLA_CHUNK = 64
GLA_SUB = 16

D_FF = 2816
N_EXPERTS = 8
TOP_K = 2

LANES = 128
FF_CHUNK = 256
VMEM_LIMIT = 56 * 1024 * 1024

ROW_TILE = 512
ATT_Q = 128
FOX_Q = 512
FOX_K = 256
SLC_K = 256
EXPERT_TILE = 512

NT_DIMS = (((1,), (1,)), ((), ()))


def _cp(sem, **kw):
    return pltpu.CompilerParams(dimension_semantics=sem, vmem_limit_bytes=VMEM_LIMIT, **kw)


def _dot(a, b):
    return jnp.dot(a, b, preferred_element_type=F32)


def _dot_nt(a, b):
    return lax.dot_general(a, b, NT_DIMS, preferred_element_type=F32)


def _split2(x):
    hi = x.astype(BF16)
    lo = (x - hi.astype(F32)).astype(BF16)
    return hi, lo


def _split3(x):
    hi = x.astype(BF16)
    r = x - hi.astype(F32)
    mid = r.astype(BF16)
    lo = (r - mid.astype(F32)).astype(BF16)
    return hi, mid, lo


def _rms_rows(x, gain):
    return x * lax.rsqrt(jnp.mean(x * x, axis=-1, keepdims=True) + NORM_EPS) * gain


def _log_sigmoid(x):
    return jnp.minimum(x, 0.0) - jnp.log1p(jnp.exp(-jnp.abs(x)))


def _sigmoid(x):
    return 1.0 / (1.0 + jnp.exp(-x))


def _seg_norm(y, gain, bd):
    ms = _dot((y * y).astype(BF16), bd)
    return y * lax.rsqrt(ms + NORM_EPS) * gain


def _rope(y, cos, sin, perm):
    return y * cos + _dot(y.astype(BF16), perm) * sin


def _block_diag_mean(width, seg):
    i = np.arange(width)
    return ((i[:, None] // seg) == (i[None, :] // seg)).astype(np.float32) / seg


def _rope_perm():
    p = np.zeros((LANES, LANES), np.float32)
    half = ROPE_DIM // 2
    for j in range(LANES):
        d = j % HEAD_DIM
        if d < half:
            p[j + half, j] = -1.0
        elif d < ROPE_DIM:
            p[j - half, j] = 1.0
    return p


def _rope_tables(pos):
    inv = ROPE_THETA ** (-np.arange(0, ROPE_DIM, 2, dtype=np.float64) / ROPE_DIM)
    ang = pos.astype(np.float64)[:, None] * inv[None, :]
    cos8, sin8 = np.cos(ang), np.sin(ang)
    n = pos.shape[0]
    ones = np.ones((n, HEAD_DIM - ROPE_DIM))
    zeros = np.zeros((n, HEAD_DIM - ROPE_DIM))
    c = np.concatenate([cos8, cos8, ones], axis=1)
    s = np.concatenate([sin8, sin8, zeros], axis=1)
    return (jnp.asarray(np.concatenate([c, c], axis=1), F32),
            jnp.asarray(np.concatenate([s, s], axis=1), F32))


PROJ_A_COLS = 3968
FOX_SLAB = 2560


def _proj_a_kernel(x_ref, g_ref, w_ref, hg_ref, cos_ref, sin_ref, bd_ref, bdlo_ref, perm_ref,
                   qa_ref, kv_ref, fox_ref, gf_ref):
    xn = _rms_rows(x_ref[...], g_ref[...]).astype(BF16)
    bd = bd_ref[...]
    perm = perm_ref[...]
    cos = cos_ref[...]
    sin = sin_ref[...]

    def tile(acc, i):
        return acc[:, i * LANES:(i + 1) * LANES]

    acc = _dot(xn, w_ref[:, 0:512])
    for i in range(4):
        y = _rope(_seg_norm(tile(acc, i), hg_ref[0:1, :], bd), cos, sin, perm)
        qa_ref[:, i * LANES:(i + 1) * LANES] = (y * 0.125).astype(BF16)

    acc = _dot(xn, w_ref[:, 512:1280])
    for i in range(6):
        y = tile(acc, i)
        if i in (2, 4):
            y = _rope(_seg_norm(y, hg_ref[1:2, :], bd), cos, sin, perm)
        kv_ref[:, i * LANES:(i + 1) * LANES] = y.astype(BF16)

    bd_lo = bdlo_ref[...]
    for c in range(4):
        acc = _dot(xn, w_ref[:, 1280 + 512 * c:1280 + 512 * (c + 1)])
        for i in range(4):
            y = tile(acc, i)
            if c < 2:
                y = _seg_norm(y, hg_ref[2:3, :], bd_lo) * 0.125 + hg_ref[4:5, :]
            else:
                y = _seg_norm(y, hg_ref[3:4, :], bd_lo)
            fox_ref[:, 512 * c + i * LANES:512 * c + (i + 1) * LANES] = y.astype(BF16)
    fox_ref[:, 2048:2560] = _dot(xn, w_ref[:, 3328:3840]).astype(BF16)

    gf_ref[...] = _dot(xn, w_ref[:, 3840:3968])


def _proj_a(x2d, norm_g, w_perm, head_gains, cos, sin, seq):
    t = x2d.shape[0]
    tm = ROW_TILE
    n_pos = seq // tm
    bd = jnp.asarray(_block_diag_mean(LANES, HEAD_DIM), BF16)
    lo = np.zeros((LANES, LANES), np.float32)
    lo[:HEAD_DIM, :] = 1.0 / HEAD_DIM
    perm = jnp.asarray(_rope_perm(), BF16)
    full = lambda i: (0, 0)
    return pl.pallas_call(
        _proj_a_kernel,
        grid=(t // tm,),
        in_specs=[
            pl.BlockSpec((tm, D_MODEL), lambda i: (i, 0)),
            pl.BlockSpec((1, D_MODEL), full),
            pl.BlockSpec((D_MODEL, PROJ_A_COLS), full),
            pl.BlockSpec((8, LANES), full),
            pl.BlockSpec((tm, LANES), lambda i: (i % n_pos, 0)),
            pl.BlockSpec((tm, LANES), lambda i: (i % n_pos, 0)),
            pl.BlockSpec((LANES, LANES), full),
            pl.BlockSpec((LANES, LANES), full),
            pl.BlockSpec((LANES, LANES), full),
        ],
        out_specs=[
            pl.BlockSpec((tm, 512), lambda i: (i, 0)),
            pl.BlockSpec((tm, 768), lambda i: (i, 0)),
            pl.BlockSpec((tm, FOX_SLAB), lambda i: (i, 0)),
            pl.BlockSpec((tm, LANES), lambda i: (i, 0)),
        ],
        out_shape=[
            jax.ShapeDtypeStruct((t, 512), BF16),
            jax.ShapeDtypeStruct((t, 768), BF16),
            jax.ShapeDtypeStruct((t, FOX_SLAB), BF16),
            jax.ShapeDtypeStruct((t, LANES), F32),
        ],
        compiler_params=_cp(("parallel",)),
        name="proj_a",
    )(x2d, norm_g, w_perm, head_gains, cos, sin, bd, jnp.asarray(lo, BF16), perm)


CUM_CHUNK = 256
FOX_BIAS_LANE = 64


def _cum_kernel(gf_ref, bias_ref, tri_ref, k_ref, place_ref, out_ref, ka_ref):
    tri = tri_ref[...]
    n_chunk = gf_ref.shape[0] // CUM_CHUNK
    carry = jnp.zeros((1, LANES), F32)
    for c in range(n_chunk):
        rows = slice(c * CUM_CHUNK, (c + 1) * CUM_CHUNK)
        lf = _log_sigmoid(gf_ref[rows, :] + bias_ref[...])
        hi, mid, lo = _split3(lf)
        cs = _dot(tri, hi) + _dot(tri, mid) + _dot(tri, lo) + carry
        out_ref[rows, :] = cs
        carry = cs[CUM_CHUNK - 1:CUM_CHUNK, :]
        c_hi, c_mid, c_lo = _split3(cs)
        placed = _dot(c_hi, place_ref[0]) + _dot(c_mid, place_ref[1]) + _dot(c_lo, place_ref[2])
        ka_ref[rows, :] = (k_ref[rows, :].astype(F32) + placed).astype(BF16)


def _fox_cumlog(gf3, bias_row, fox3):
    b, s, _ = gf3.shape
    tri = jnp.asarray(np.tril(np.ones((CUM_CHUNK, CUM_CHUNK), np.float32)), BF16)
    place = np.zeros((3, LANES, FOX_HEADS * LANES), np.float32)
    for h in range(FOX_HEADS):
        for piece in range(3):
            place[piece, 24 + h, h * LANES + FOX_BIAS_LANE + piece] = 1.0
    return pl.pallas_call(
        _cum_kernel,
        grid=(b,),
        in_specs=[
            pl.BlockSpec((None, s, LANES), lambda i: (i, 0, 0)),
            pl.BlockSpec((1, LANES), lambda i: (0, 0)),
            pl.BlockSpec((CUM_CHUNK, CUM_CHUNK), lambda i: (0, 0)),
            pl.BlockSpec((None, s, FOX_HEADS * LANES), lambda i: (i, 0, 1)),
            pl.BlockSpec((3, LANES, FOX_HEADS * LANES), lambda i: (0, 0, 0)),
        ],
        out_specs=[
            pl.BlockSpec((None, s, LANES), lambda i: (i, 0, 0)),
            pl.BlockSpec((None, s, FOX_HEADS * LANES), lambda i: (i, 0, 0)),
        ],
        out_shape=[
            jax.ShapeDtypeStruct((b, s, LANES), F32),
            jax.ShapeDtypeStruct((b, s, FOX_HEADS * LANES), BF16),
        ],
        compiler_params=_cp(("parallel",)),
        name="fox_cumlog",
    )(gf3, bias_row, tri, fox3, jnp.asarray(place, BF16))


def _gelu_tanh(x):
    return 0.5 * x * (1.0 + jnp.tanh(0.7978845608028654 * (x + 0.044715 * x * x * x)))


def _cmp_kernel(kx_ref, vx_ref, pek_ref, pev_ref, w1k_ref, w2k_ref, w1v_ref, w2v_ref,
                kg_ref, cos_ref, sin_ref, bd_ref, perm_ref, kc_ref, vc_ref):
    ncp = kx_ref.shape[0]

    def compress(x_ref, pe_ref, w1_ref, w2_ref):
        x = x_ref[...].astype(F32)
        first = _dot((x + pe_ref[0:1, :]).astype(BF16), w1_ref[0])
        second = _dot((x + pe_ref[1:2, :]).astype(BF16), w1_ref[1])
        pre = first + pltpu.roll(second, ncp - 1, 0)
        return _dot(_gelu_tanh(pre).astype(BF16), w2_ref[...])

    k = compress(kx_ref, pek_ref, w1k_ref, w2k_ref)
    k = _rope(_seg_norm(k, kg_ref[...], bd_ref[...]), cos_ref[...], sin_ref[...], perm_ref[...])
    kc_ref[...] = k.astype(BF16)
    vc_ref[...] = compress(vx_ref, pev_ref, w1v_ref, w2v_ref).astype(BF16)


def _nsa_compress(kx, vx, pek, pev, w1k, w2k, w1v, w2v, kgain, ccos, csin):
    b, ncp, width = kx.shape
    bd = jnp.asarray(_block_diag_mean(LANES, HEAD_DIM), BF16)
    perm = jnp.asarray(_rope_perm(), BF16)
    c2 = lambda i: (0, 0)
    c3 = lambda i: (0, 0, 0)
    return pl.pallas_call(
        _cmp_kernel,
        grid=(b,),
        in_specs=[
            pl.BlockSpec((None, ncp, width), lambda i: (i, 0, 0)),
            pl.BlockSpec((None, ncp, width), lambda i: (i, 0, 0)),
            pl.BlockSpec((2, width), c2),
            pl.BlockSpec((2, width), c2),
            pl.BlockSpec((2, width, LANES), c3),
            pl.BlockSpec((LANES, LANES), c2),
            pl.BlockSpec((2, width, LANES), c3),
            pl.BlockSpec((LANES, LANES), c2),
            pl.BlockSpec((1, LANES), c2),
            pl.BlockSpec((ncp, LANES), c2),
            pl.BlockSpec((ncp, LANES), c2),
            pl.BlockSpec((LANES, LANES), c2),
            pl.BlockSpec((LANES, LANES), c2),
        ],
        out_specs=[
            pl.BlockSpec((None, ncp, LANES), lambda i: (i, 0, 0)),
            pl.BlockSpec((None, ncp, LANES), lambda i: (i, 0, 0)),
        ],
        out_shape=[
            jax.ShapeDtypeStruct((b, ncp, LANES), BF16),
            jax.ShapeDtypeStruct((b, ncp, LANES), BF16),
        ],
        compiler_params=_cp(("parallel",)),
        name="nsa_compress",
    )(kx, vx, pek, pev, w1k, w2k, w1v, w2v, kgain, ccos, csin, bd, perm)


def _nsa_cmp_kernel(q_ref, kc_ref, vc_ref, ovt_ref, eye_ref, oc_ref, sb_ref, *, n_cmp, n_slc):
    tq = ATT_Q
    q0 = pl.program_id(1) * tq
    kc = kc_ref[...]
    vc = vc_ref[...]
    ncp = kc.shape[0]
    ovt = ovt_ref[...]
    lo_half = lax.broadcasted_iota(jnp.int32, (tq, LANES), 1) < HEAD_DIM

    n_lane = lax.broadcasted_iota(jnp.int32, (tq, ncp), 1)
    t_sub = q0 + lax.broadcasted_iota(jnp.int32, (tq, ncp), 0)
    mask = (n_lane * CMP_STRIDE + (CMP_LEN - 1) <= t_sub) & (n_lane < n_cmp)
    n_sub = lax.broadcasted_iota(jnp.int32, (ncp, tq), 0)
    t_lane = q0 + lax.broadcasted_iota(jnp.int32, (ncp, tq), 1)
    mask_t = (n_sub * CMP_STRIDE + (CMP_LEN - 1) <= t_lane) & (n_sub < n_cmp)

    imp_t = [jnp.zeros((64, tq), F32), jnp.zeros((64, tq), F32)]
    for t in range(4):
        qt = q_ref[:, t * LANES:(t + 1) * LANES]
        outs = []
        for g in range(2):
            qm = jnp.where(lo_half if g == 0 else jnp.logical_not(lo_half), qt, jnp.zeros_like(qt))
            s = jnp.where(mask, _dot_nt(qm, kc), NEG)
            p = jnp.where(mask, jnp.exp(s - jnp.max(s, axis=-1, keepdims=True)), 0.0)
            p = p / jnp.maximum(jnp.sum(p, axis=-1, keepdims=True), 1e-30)
            outs.append(_dot(p.astype(BF16), vc))
            st = jnp.where(mask_t, _dot_nt(kc, qm), NEG)
            pt = jnp.where(mask_t, jnp.exp(st - jnp.max(st, axis=0, keepdims=True)), 0.0)
            pt = pt / jnp.maximum(jnp.sum(pt, axis=0, keepdims=True), 1e-30)
            hi, lo = _split2(pt)
            imp_t[g] = imp_t[g] + _dot(ovt, hi) + _dot(ovt, lo)
        oc_ref[:, t * LANES:(t + 1) * LANES] = jnp.where(lo_half, outs[0], outs[1]).astype(BF16)

    blk = lax.broadcasted_iota(jnp.int32, (64, tq), 0)
    cur = (q0 + lax.broadcasted_iota(jnp.int32, (64, tq), 1)) >> 6
    forced = (blk == 0) | (blk == cur) | (blk == cur - 1)
    future = blk > cur
    exists = blk < n_slc
    biases = []
    for g in range(2):
        v = jnp.where(forced, FORCE_SCORE, jnp.where(future, -FORCE_SCORE, imp_t[g]))
        v = jnp.where(exists, v, -3e38)
        sel = jnp.zeros((64, tq), jnp.int32)
        for _ in range(min(SLC_TOPK, n_slc)):
            m = jnp.max(v, axis=0, keepdims=True)
            first = jnp.min(jnp.where(v == m, blk, 64), axis=0, keepdims=True)
            hit = blk == first
            sel = jnp.where(hit, 1, sel)
            v = jnp.where(hit, -3.2e38, v)
        keep = (sel == 1) & jnp.logical_not(future) & exists
        biases.append(jnp.where(keep, 0.0, NEG).astype(BF16))
    sb_t = jnp.concatenate([biases[1], biases[0]], axis=0)
    sb_ref[...] = _dot_nt(eye_ref[...], sb_t).astype(BF16)


def _nsa_cmp_select(qa3, kcmp, vcmp, ovt, n_cmp, n_slc):
    b, s, _ = qa3.shape
    ncp = kcmp.shape[1]
    eye = jnp.asarray(np.eye(ATT_Q, dtype=np.float32), BF16)
    kern = functools.partial(_nsa_cmp_kernel, n_cmp=n_cmp, n_slc=n_slc)
    return pl.pallas_call(
        kern,
        grid=(b, s // ATT_Q),
        in_specs=[
            pl.BlockSpec((None, ATT_Q, 512), lambda i, j: (i, j, 0)),
            pl.BlockSpec((None, ncp, LANES), lambda i, j: (i, 0, 0)),
            pl.BlockSpec((None, ncp, LANES), lambda i, j: (i, 0, 0)),
            pl.BlockSpec((64, ncp), lambda i, j: (0, 0)),
            pl.BlockSpec((ATT_Q, ATT_Q), lambda i, j: (0, 0)),
        ],
        out_specs=[
            pl.BlockSpec((None, ATT_Q, 512), lambda i, j: (i, j, 0)),
            pl.BlockSpec((None, ATT_Q, LANES), lambda i, j: (i, j, 0)),
        ],
        out_shape=[
            jax.ShapeDtypeStruct((b, s, 512), BF16),
            jax.ShapeDtypeStruct((b, s, LANES), BF16),
        ],
        compiler_params=_cp(("parallel", "parallel")),
        name="nsa_cmp_select",
    )(qa3, kcmp, vcmp, ovt, eye)


V_ROWS = 80


def _flash_step_t(carry, s_t, v_t, *, query_bias=None, mask=None):
    m, acc = carry
    if mask is not None:
        s_t = jnp.where(mask, s_t, NEG)
    col_max = jnp.max(s_t, axis=0, keepdims=True)
    if query_bias is not None:
        m_new = jnp.maximum(m, col_max + query_bias)
        shift = m_new - query_bias
    else:
        m_new = jnp.maximum(m, col_max)
        shift = m_new
    p = jnp.exp(s_t - shift).astype(BF16)
    acc = jnp.exp(m - m_new) * acc + _dot(v_t, p)
    return m_new, acc


def _flash_init_t(queries):
    return (jnp.full((1, queries), NEG, F32), jnp.zeros((V_ROWS, queries), F32))


def _values_t(v):
    b, s, w = v.shape
    heads = w // HEAD_DIM
    vt = v.reshape(b, s, heads, HEAD_DIM).transpose(0, 2, 3, 1)
    ones = jnp.ones((b, heads, 1, s), v.dtype)
    pad = jnp.zeros((b, heads, V_ROWS - HEAD_DIM - 1, s), v.dtype)
    return jnp.concatenate([vt, ones, pad], axis=2)


def _nsa_slc_t_kernel(q_ref, sb_ref, k0_ref, k1_ref, v_ref, eye_ref, o_ref, sa_ref, sc_ref):
    tq, tk = ATT_Q, SLC_K
    q0 = pl.program_id(1) * tq
    sb = sb_ref[...]
    lo_half = lax.broadcasted_iota(jnp.int32, (tq, LANES), 1) < HEAD_DIM
    cols = 4 * tq
    n_before = q0 // tk
    k_refs = (k0_ref, k1_ref)
    slots = (sa_ref, sc_ref)

    qq = []
    for g in range(2):
        parts = []
        for t in range(4):
            qt = q_ref[:, t * LANES:(t + 1) * LANES]
            parts.append(jnp.where(lo_half, qt, sb) if g == 0 else jnp.where(lo_half, sb, qt))
        qq.append(jnp.concatenate(parts, axis=0))

    def issue(j, slot):
        off = pl.multiple_of(j * tk, tk)
        for g in range(2):
            slots[slot][g] = _dot_nt(k_refs[g][pl.ds(off, tk), :], qq[g])

    def consume(j, carry, slot, mask=None):
        off = pl.multiple_of(j * tk, tk)
        return tuple(_flash_step_t(carry[g], slots[slot][g], v_ref[g, :, pl.ds(off, tk)], mask=mask)
                     for g in range(2))

    def step(jj, carry):
        j = 2 * jj
        issue(j + 1, 1)
        carry = consume(j, carry, 0)
        issue(j + 2, 0)
        return consume(j + 1, carry, 1)

    issue(0, 0)
    carry = lax.fori_loop(0, n_before // 2, step, (_flash_init_t(cols), _flash_init_t(cols)))

    q_in_chunk = q0 - n_before * tk
    query = lax.broadcasted_iota(jnp.int32, (tk, cols), 1) & (tq - 1)
    causal = lax.broadcasted_iota(jnp.int32, (tk, cols), 0) - query <= q_in_chunk

    def even_tail(carry):
        return consume(n_before, carry, 0, causal)

    def odd_tail(carry):
        issue(n_before, 1)
        carry = consume(n_before - 1, carry, 0)
        return consume(n_before, carry, 1, causal)

    carry = lax.cond((n_before & 1) == 0, even_tail, odd_tail, carry)

    outs = [(acc[0:HEAD_DIM] / acc[HEAD_DIM:HEAD_DIM + 1]).astype(BF16) for (_, acc) in carry]
    eye = eye_ref[...]
    for t in range(4):
        cs = slice(t * tq, (t + 1) * tq)
        both = jnp.concatenate([outs[0][:, cs], outs[1][:, cs]], axis=0)
        o_ref[:, t * LANES:(t + 1) * LANES] = _dot_nt(eye, both).astype(BF16)


def _nsa_selected_t(qa3, sb, k_aug0, k_aug1, v_t):
    b, s, _ = qa3.shape
    blk = lambda i, j: (i, j, 0)
    whole = lambda i, j: (i, 0, 0)
    eye = jnp.asarray(np.eye(ATT_Q, dtype=np.float32), BF16)
    return pl.pallas_call(
        _nsa_slc_t_kernel,
        grid=(b, s // ATT_Q),
        in_specs=[
            pl.BlockSpec((None, ATT_Q, 512), blk),
            pl.BlockSpec((None, ATT_Q, LANES), blk),
            pl.BlockSpec((None, s, LANES), whole),
            pl.BlockSpec((None, s, LANES), whole),
            pl.BlockSpec((None, 2, V_ROWS, s), lambda i, j: (i, 0, 0, 0)),
            pl.BlockSpec((ATT_Q, ATT_Q), lambda i, j: (0, 0)),
        ],
        out_specs=pl.BlockSpec((None, ATT_Q, 512), blk),
        out_shape=jax.ShapeDtypeStruct((b, s, 512), BF16),
        scratch_shapes=[pltpu.VMEM((2, SLC_K, 4 * ATT_Q), F32), pltpu.VMEM((2, SLC_K, 4 * ATT_Q), F32)],
        compiler_params=_cp(("parallel", "parallel")),
        name="nsa_selected",
    )(qa3, sb, k_aug0, k_aug1, v_t, eye)


def _nsa_win_kernel(q_ref, k_ref, v_ref, eye_ref, o_ref):
    tq = ATT_Q
    span = WINDOW + tq
    q0 = pl.program_id(1) * tq
    start = pl.multiple_of(jnp.maximum(q0 - WINDOW, 0), tq)
    k = k_ref[pl.ds(start, span), :]
    cols = 4 * tq
    lo_half = lax.broadcasted_iota(jnp.int32, (tq, LANES), 1) < HEAD_DIM
    query = lax.broadcasted_iota(jnp.int32, (span, cols), 1) & (tq - 1)
    dist = (q0 - start) + query - lax.broadcasted_iota(jnp.int32, (span, cols), 0)
    mask = (dist >= 0) & (dist < WINDOW)
    scores = []
    for g in range(2):
        keep = lo_half if g == 0 else jnp.logical_not(lo_half)
        parts = []
        for t in range(4):
            qt = q_ref[:, t * LANES:(t + 1) * LANES]
            parts.append(jnp.where(keep, qt, jnp.zeros_like(qt)))
        scores.append(_dot_nt(k, jnp.concatenate(parts, axis=0)))
    outs = []
    for g in range(2):
        s_t = jnp.where(mask, scores[g], NEG)
        p = jnp.exp(s_t - jnp.max(s_t, axis=0, keepdims=True)).astype(BF16)
        acc = _dot(v_ref[g, :, pl.ds(start, span)], p)
        outs.append((acc[0:HEAD_DIM] / acc[HEAD_DIM:HEAD_DIM + 1]).astype(BF16))
    eye = eye_ref[...]
    for t in range(4):
        cs = slice(t * tq, (t + 1) * tq)
        both = jnp.concatenate([outs[0][:, cs], outs[1][:, cs]], axis=0)
        o_ref[:, t * LANES:(t + 1) * LANES] = _dot_nt(eye, both).astype(BF16)


def _nsa_window(qa3, k2, v_t):
    b, s, _ = qa3.shape
    blk = lambda i, j: (i, j, 0)
    whole = lambda i, j: (i, 0, 0)
    eye = jnp.asarray(np.eye(ATT_Q, dtype=np.float32), BF16)
    return pl.pallas_call(
        _nsa_win_kernel,
        grid=(b, s // ATT_Q),
        in_specs=[
            pl.BlockSpec((None, ATT_Q, 512), blk),
            pl.BlockSpec((None, s, LANES), whole),
            pl.BlockSpec((None, 2, V_ROWS, s), lambda i, j: (i, 0, 0, 0)),
            pl.BlockSpec((ATT_Q, ATT_Q), lambda i, j: (0, 0)),
        ],
        out_specs=pl.BlockSpec((None, ATT_Q, 512), blk),
        out_shape=jax.ShapeDtypeStruct((b, s, 512), BF16),
        compiler_params=_cp(("parallel", "parallel")),
        name="nsa_window",
    )(qa3, k2, v_t, eye)


def _fox_t_kernel(q_ref, k_ref, v_ref, cr_ref, o_ref, sa_ref, sb_ref):
    tq, tk = FOX_Q, FOX_K
    q0 = pl.program_id(1) * tq
    n_before = q0 // tk
    n_diag = tq // tk
    for pair in range(FOX_HEADS // 2):
        heads = (2 * pair, 2 * pair + 1)
        qs = tuple(q_ref[:, h * LANES:(h + 1) * LANES] for h in heads)
        cq = tuple(cr_ref[h:h + 1, pl.ds(pl.multiple_of(q0, tq), tq)] for h in heads)

        def scores(j, i, q_lo=0):
            off = pl.multiple_of(j * tk, tk)
            h = heads[i]
            return _dot_nt(k_ref[pl.ds(off, tk), h * LANES:(h + 1) * LANES], qs[i][q_lo:])

        slots = (sa_ref, sb_ref)

        def consume(j, carry, slot):
            off = pl.multiple_of(j * tk, tk)
            return tuple(_flash_step_t(carry[i], slots[slot][i], v_ref[heads[i], :, pl.ds(off, tk)],
                                       query_bias=cq[i]) for i in range(2))

        def step(jj, carry):
            j = 2 * jj
            for i in range(2):
                sb_ref[i] = scores(j + 1, i)
            carry = consume(j, carry, 0)
            for i in range(2):
                sa_ref[i] = scores(j + 2, i)
            return consume(j + 1, carry, 1)

        for i in range(2):
            sa_ref[i] = scores(0, i)
        carry = lax.fori_loop(0, n_before // 2, step, (_flash_init_t(tq), _flash_init_t(tq)))

        late = [[scores(n_before + d, i, d * tk) for i in range(2)] for d in range(1, n_diag)]
        for d in range(n_diag):
            q_lo = d * tk
            off = pl.multiple_of((n_before + d) * tk, tk)
            shape = (tk, tq - q_lo)
            mask = (lax.broadcasted_iota(jnp.int32, shape, 0) + (d * tk - q_lo)
                    <= lax.broadcasted_iota(jnp.int32, shape, 1))
            out = []
            for i in range(2):
                m, acc = carry[i]
                s_t = sa_ref[i] if d == 0 else late[d - 1][i]
                new = _flash_step_t((m[:, q_lo:], acc[:, q_lo:]), s_t, v_ref[heads[i], :, pl.ds(off, tk)],
                                    query_bias=cq[i][:, q_lo:], mask=mask)
                if q_lo:
                    new = tuple(jnp.concatenate([old[:, :q_lo], part], axis=1)
                                for old, part in zip((m, acc), new))
                out.append(new)
            carry = tuple(out)
        for i, h in enumerate(heads):
            acc = carry[i][1]
            o_ref[h * HEAD_DIM:(h + 1) * HEAD_DIM, :] = (
                acc[0:HEAD_DIM] / acc[HEAD_DIM:HEAD_DIM + 1]).astype(BF16)


def _fox_attention_t(fox3, k_aug, v_t, cum_row):
    b, s, _ = fox3.shape
    width = FOX_HEADS * LANES
    return pl.pallas_call(
        _fox_t_kernel,
        grid=(b, s // FOX_Q),
        in_specs=[
            pl.BlockSpec((None, FOX_Q, width), lambda i, j: (i, j, 0)),
            pl.BlockSpec((None, s, width), lambda i, j: (i, 0, 0)),
            pl.BlockSpec((None, FOX_HEADS, V_ROWS, s), lambda i, j: (i, 0, 0, 0)),
            pl.BlockSpec((None, FOX_HEADS, s), lambda i, j: (i, 0, 0)),
        ],
        out_specs=pl.BlockSpec((None, FOX_HEADS * HEAD_DIM, FOX_Q), lambda i, j: (i, 0, j)),
        out_shape=jax.ShapeDtypeStruct((b, FOX_HEADS * HEAD_DIM, s), BF16),
        scratch_shapes=[pltpu.VMEM((2, FOX_K, FOX_Q), F32), pltpu.VMEM((2, FOX_K, FOX_Q), F32)],
        compiler_params=_cp(("parallel", "parallel")),
        name="fox_attention",
    )(fox3, k_aug, v_t, cum_row)


def _mix_out(oc_ref, os_ref, ow_ref, ob_ref, gf_ref, gb_ref, gx_ref, x_ref, w_ref):
    gates = _sigmoid(gf_ref[...] + gb_ref[...])
    hi, lo = _split2(gates)
    o_a = None
    for j, br in enumerate((oc_ref, os_ref, ow_ref)):
        gexp = _dot(hi, gx_ref[j]) + _dot(lo, gx_ref[j])
        term = gexp * br[...].astype(F32)
        o_a = term if o_a is None else o_a + term
    y = _dot(o_a.astype(BF16), w_ref[0:512, :]) + _dot(ob_ref[...], w_ref[512:1024, :])
    return x_ref[...] + y


def _swiglu_body(xn, wg_ref, wu_ref, wd_ref, h_ref):
    for c in range(D_FF // FF_CHUNK):
        cols = slice(c * FF_CHUNK, (c + 1) * FF_CHUNK)
        g = _dot(xn, wg_ref[:, cols])
        u = _dot(xn, wu_ref[:, cols])
        h_ref[:, cols] = (g * _sigmoid(g) * u).astype(BF16)
    return _dot(h_ref[...], wd_ref[...])


def _out_ffn_kernel(oc_ref, os_ref, ow_ref, ob_ref, gf_ref, gb_ref, gx_ref, x_ref, w_ref,
                    g_ref, wg_ref, wu_ref, wd_ref, o_ref, h_ref):
    x1 = _mix_out(oc_ref, os_ref, ow_ref, ob_ref, gf_ref, gb_ref, gx_ref, x_ref, w_ref)
    xn = _rms_rows(x1, g_ref[...]).astype(BF16)
    o_ref[...] = x1 + _swiglu_body(xn, wg_ref, wu_ref, wd_ref, h_ref)


def _out_ffn(oc, os_, ow, ob, gf, gate_bias_row, gate_expand, x2d, w_out_perm, norm_g, wg, wu, wd):
    t = x2d.shape[0]
    tm = ROW_TILE
    row = lambda i: (i, 0)
    c2 = lambda i: (0, 0)
    return pl.pallas_call(
        _out_ffn_kernel,
        grid=(t // tm,),
        in_specs=[
            pl.BlockSpec((tm, 512), row),
            pl.BlockSpec((tm, 512), row),
            pl.BlockSpec((tm, 512), row),
            pl.BlockSpec((tm, 512), row),
            pl.BlockSpec((tm, LANES), row),
            pl.BlockSpec((1, LANES), c2),
            pl.BlockSpec((3, LANES, 512), lambda i: (0, 0, 0)),
            pl.BlockSpec((tm, D_MODEL), row),
            pl.BlockSpec((D_MODEL, D_MODEL), c2),
            pl.BlockSpec((1, D_MODEL), c2),
            pl.BlockSpec((D_MODEL, D_FF), c2),
            pl.BlockSpec((D_MODEL, D_FF), c2),
            pl.BlockSpec((D_FF, D_MODEL), c2),
        ],
        out_specs=pl.BlockSpec((tm, D_MODEL), row),
        out_shape=jax.ShapeDtypeStruct((t, D_MODEL), F32),
        scratch_shapes=[pltpu.VMEM((tm, D_FF), BF16)],
        compiler_params=_cp(("parallel",)),
        name="out_a_ffn",
    )(oc, os_, ow, ob, gf, gate_bias_row, gate_expand, x2d, w_out_perm, norm_g, wg, wu, wd)


def _ffn_expert_kernel(te_ref, nu_ref, cnt_ref, src_now_ref, src_next_ref, dst_ref,
                       xn_hbm, wg_ref, wu_ref, wd_ref, y_hbm, xbuf, ybuf, h_ref, gsem, ssem):
    tm = EXPERT_TILE
    i = pl.program_id(0)
    n_used = nu_ref[0]
    slot = i & 1

    def gather(idx_ref, s):
        def body(r, _):
            pltpu.make_async_copy(xn_hbm.at[idx_ref[0, 0, r]], xbuf.at[s, r], gsem.at[s]).start()
            return 0
        lax.fori_loop(0, tm, body, 0, unroll=8)

    def wait_rows(copy_of_row, count):
        def body(r, _):
            copy_of_row(r).wait()
            return 0
        lax.fori_loop(0, count, body, 0)

    def scatter_copy(s, r, row):
        return pltpu.make_async_copy(ybuf.at[s, r], y_hbm.at[row], ssem.at[s])

    def wait_scatter(tile, s):
        @pl.when(cnt_ref[tile] == tm)
        def _():
            pltpu.make_async_copy(ybuf.at[s], y_hbm.at[pl.ds(0, tm)], ssem.at[s]).wait()

        @pl.when(cnt_ref[tile] < tm)
        def _():
            wait_rows(lambda r: scatter_copy(s, r, 0), cnt_ref[tile])

    @pl.when(i == 0)
    def _():
        gather(src_now_ref, 0)

    @pl.when(i + 1 < n_used)
    def _():
        gather(src_next_ref, 1 - slot)

    @pl.when(i < n_used)
    def _():
        pltpu.make_async_copy(xn_hbm.at[pl.ds(0, tm)], xbuf.at[slot], gsem.at[slot]).wait()

        @pl.when(i >= 2)
        def _():
            wait_scatter(i - 2, slot)

        x = xbuf[slot].reshape(tm, D_MODEL)
        y = _swiglu_body(x, wg_ref, wu_ref, wd_ref, h_ref)
        ybuf[slot] = y.astype(BF16).reshape(tm, D_MODEL // LANES, LANES)

        def send(r, _):
            scatter_copy(slot, r, dst_ref[0, 0, r]).start()
            return 0

        @pl.when(cnt_ref[i] == tm)
        def _():
            lax.fori_loop(0, tm, send, 0, unroll=8)

        @pl.when(cnt_ref[i] < tm)
        def _():
            lax.fori_loop(0, cnt_ref[i], send, 0)

    @pl.when(i == pl.num_programs(0) - 1)
    def _():
        @pl.when(n_used >= 2)
        def _():
            wait_scatter(n_used - 2, (n_used - 2) & 1)
        wait_scatter(n_used - 1, (n_used - 1) & 1)


def _ffn_experts(tile_expert, n_used, tile_count, src_tok, dst_row, xn3, wg, wu, wd):
    tm = EXPERT_TILE
    n_tiles = src_tok.shape[0]
    t = xn3.shape[0]
    wsel = lambda i, te, nu, cnt: (te[i], 0, 0)
    idx = lambda i, te, nu, cnt: (i, 0, 0)
    idx_next = lambda i, te, nu, cnt: (jnp.minimum(i + 1, n_tiles - 1), 0, 0)
    smem_idx = lambda m: pl.BlockSpec((1, 1, tm), m, memory_space=pltpu.SMEM)
    rows = D_MODEL // LANES
    grid_spec = pltpu.PrefetchScalarGridSpec(
        num_scalar_prefetch=3,
        grid=(n_tiles,),
        in_specs=[
            smem_idx(idx),
            smem_idx(idx_next),
            smem_idx(idx),
            pl.BlockSpec(memory_space=pl.ANY),
            pl.BlockSpec((None, D_MODEL, D_FF), wsel),
            pl.BlockSpec((None, D_MODEL, D_FF), wsel),
            pl.BlockSpec((None, D_FF, D_MODEL), wsel),
        ],
        out_specs=pl.BlockSpec(memory_space=pl.ANY),
        scratch_shapes=[
            pltpu.VMEM((2, tm, rows, LANES), BF16),
            pltpu.VMEM((2, tm, rows, LANES), BF16),
            pltpu.VMEM((tm, D_FF), BF16),
            pltpu.SemaphoreType.DMA((2,)),
            pltpu.SemaphoreType.DMA((2,)),
        ],
    )
    return pl.pallas_call(
        _ffn_expert_kernel,
        grid_spec=grid_spec,
        out_shape=jax.ShapeDtypeStruct((TOP_K * t, rows, LANES), BF16),
        compiler_params=_cp(("arbitrary",), has_side_effects=True),
        name="ffn_experts",
    )(tile_expert, n_used, tile_count, src_tok, src_tok, dst_row, xn3, wg, wu, wd)


def _proj_c_kernel(x_ref, g_ref, w_ref, wa_ref, ba_ref, tri_ref, qk_ref, v_ref, r_ref, la_ref):
    xn = _rms_rows(x_ref[...], g_ref[...]).astype(BF16)
    for c in range(2):
        qk_ref[:, 512 * c:512 * (c + 1)] = _dot(xn, w_ref[:, 512 * c:512 * (c + 1)]).astype(BF16)
    for c in range(2):
        v_ref[:, 512 * c:512 * (c + 1)] = _dot(xn, w_ref[:, 1024 + 512 * c:1536 + 512 * c]).astype(BF16)
    for c in range(2):
        r_ref[:, 512 * c:512 * (c + 1)] = _dot(xn, w_ref[:, 2048 + 512 * c:2560 + 512 * c]).astype(BF16)
    a1 = _dot(xn, w_ref[:, 3072:3200])
    hi, lo = _split2(a1)
    wa = wa_ref[...]
    pre = _dot(hi, wa) + _dot(lo, wa) + ba_ref[...]
    la = _log_sigmoid(pre) * (1.0 / GLA_TAU)
    tri = tri_ref[...]
    c_hi, c_mid, c_lo = _split3(la)
    la_ref[...] = _dot(tri, c_hi) + _dot(tri, c_mid) + _dot(tri, c_lo)


def _proj_c(x2d, norm_g, w_pad, wa_pad, ba_row):
    t = x2d.shape[0]
    tm = ROW_TILE
    idx = np.arange(tm)
    tri = ((idx[:, None] // GLA_CHUNK == idx[None, :] // GLA_CHUNK) & (idx[:, None] >= idx[None, :])).astype(np.float32)
    row = lambda i: (i, 0)
    c2 = lambda i: (0, 0)
    return pl.pallas_call(
        _proj_c_kernel,
        grid=(t // tm,),
        in_specs=[
            pl.BlockSpec((tm, D_MODEL), row),
            pl.BlockSpec((1, D_MODEL), c2),
            pl.BlockSpec((D_MODEL, 3200), c2),
            pl.BlockSpec((LANES, GLA_DK), c2),
            pl.BlockSpec((1, GLA_DK), c2),
            pl.BlockSpec((tm, tm), c2),
        ],
        out_specs=[
            pl.BlockSpec((tm, 1024), row),
            pl.BlockSpec((tm, 1024), row),
            pl.BlockSpec((tm, 1024), row),
            pl.BlockSpec((tm, GLA_DK), row),
        ],
        out_shape=[
            jax.ShapeDtypeStruct((t, 1024), BF16),
            jax.ShapeDtypeStruct((t, 1024), BF16),
            jax.ShapeDtypeStruct((t, 1024), BF16),
            jax.ShapeDtypeStruct((t, GLA_DK), F32),
        ],
        compiler_params=_cp(("parallel",)),
        name="proj_c",
    )(x2d, norm_g, w_pad, wa_pad, ba_row, jnp.asarray(tri, BF16))


GLA_STEP_HEADS = 2
LOG2E = 1.4426950408889634


def _gla_kernel(q_ref, k_ref, v_ref, b_ref, eye_ref, o_ref, st_ref):
    c_len, sub = GLA_CHUNK, GLA_SUB
    n_sub = c_len // sub
    hs = range(GLA_STEP_HEADS)
    dk = q_ref.shape[1] // GLA_STEP_HEADS
    dv = v_ref.shape[1] // GLA_STEP_HEADS
    scale = dk ** -0.5
    eye = eye_ref[...]
    st_ref[...] = jnp.zeros_like(st_ref)
    row = lax.broadcasted_iota(jnp.int32, (c_len, c_len), 0)
    col = lax.broadcasted_iota(jnp.int32, (c_len, c_len), 1)
    col_in_sub = col & (sub - 1)
    same_sub_causal = ((row >> 4) == (col >> 4)) & ((row & (sub - 1)) >= col_in_sub)
    earlier_sub = (col >> 4) < (row >> 4)

    def chunk(c, _):
        rows = pl.ds(pl.multiple_of(c * c_len, c_len), c_len)
        q = [q_ref[rows, h * dk:(h + 1) * dk].astype(F32) * scale for h in hs]
        k = [k_ref[rows, h * dk:(h + 1) * dk].astype(F32) for h in hs]
        vb = [v_ref[rows, h * dv:(h + 1) * dv] for h in hs]
        b = [b_ref[rows, h * dk:(h + 1) * dk] * LOG2E for h in hs]
        b_last = [b[h][c_len - 1:c_len, :] for h in hs]
        st_t = [st_ref[h] for h in hs]

        v_t = [_dot_nt(eye, vb[h]).astype(BF16) for h in hs]
        inter = [_dot_nt((q[h] * jnp.exp2(b[h])).astype(BF16), st_t[h].astype(BF16)) for h in hs]
        att_rows = [[jnp.zeros((sub, c_len), F32)] for _ in hs]
        for i in range(1, n_sub):
            for h in hs:
                r = b[h][i * sub:i * sub + 1, :]
                qi = (q[h][i * sub:(i + 1) * sub] * jnp.exp2(b[h][i * sub:(i + 1) * sub] - r)).astype(BF16)
                kj = (k[h] * jnp.exp2(jnp.minimum(r - b[h], 0.0))).astype(BF16)
                att_rows[h].append(_dot_nt(qi, kj))
        for h in hs:
            kd = (k[h] * jnp.exp2(b_last[h] - b[h])).astype(BF16)
            st_ref[h] = st_t[h] * jnp.exp2(b_last[h]) + _dot(v_t[h], kd)

        att_d = [jnp.zeros((c_len, c_len), F32) for _ in hs]
        for j in range(sub):
            def rows_j(x):
                return jnp.concatenate(
                    [jnp.broadcast_to(x[s * sub + j:s * sub + j + 1, :], (sub, x.shape[1]))
                     for s in range(n_sub)], axis=0)
            for h in hs:
                e = jnp.exp2(jnp.minimum(b[h] - rows_j(b[h]), 0.0))
                a = jnp.sum(q[h] * e * rows_j(k[h]), axis=-1, keepdims=True)
                att_d[h] = jnp.where(col_in_sub == j, a, att_d[h])
        for h in hs:
            att = jnp.where(earlier_sub, jnp.concatenate(att_rows[h], axis=0),
                            jnp.where(same_sub_causal, att_d[h], 0.0)).astype(BF16)
            o_ref[rows, h * dv:(h + 1) * dv] = (inter[h] + _dot(att, vb[h])).astype(BF16)
        return 0

    def two_chunks(cc, carry):
        chunk(2 * cc, carry)
        return chunk(2 * cc + 1, carry)

    lax.fori_loop(0, q_ref.shape[0] // (2 * c_len), two_chunks, 0)


def _gla(qk3, v3, b3):
    b, s, _ = qk3.shape
    n = GLA_STEP_HEADS
    dk = GLA_DK // GLA_HEADS
    dv = GLA_DV // GLA_HEADS
    steps = GLA_HEADS // n
    eye = jnp.asarray(np.eye(dv, dtype=np.float32), BF16)
    return pl.pallas_call(
        _gla_kernel,
        grid=(b, steps),
        in_specs=[
            pl.BlockSpec((None, s, n * dk), lambda i, h: (i, 0, h)),
            pl.BlockSpec((None, s, n * dk), lambda i, h: (i, 0, steps + h)),
            pl.BlockSpec((None, s, n * dv), lambda i, h: (i, 0, h)),
            pl.BlockSpec((None, s, n * dk), lambda i, h: (i, 0, h)),
            pl.BlockSpec((dv, dv), lambda i, h: (0, 0)),
        ],
        out_specs=pl.BlockSpec((None, s, n * dv), lambda i, h: (i, 0, h)),
        out_shape=jax.ShapeDtypeStruct((b, s, GLA_DV), BF16),
        scratch_shapes=[pltpu.VMEM((n, dv, dk), F32)],
        compiler_params=_cp(("parallel", "parallel")),
        name="gla",
    )(qk3, qk3, v3, b3, eye)


def _out_c_kernel(o_ref, r_ref, x_ref, on_ref, bd_ref, w_ref, mg_ref, rt_ref,
                  x3_ref, xn_ref, e1_ref, e2_ref, w1_ref, w2_ref):
    dv = GLA_DV // GLA_HEADS
    bd = bd_ref[...]
    parts = []
    for h in range(GLA_HEADS):
        cols = slice(h * dv, (h + 1) * dv)
        oh = _seg_norm(o_ref[:, cols].astype(F32), on_ref[...], bd)
        r = r_ref[:, cols].astype(F32)
        parts.append((oh * (r * _sigmoid(r))).astype(BF16))
    x3 = x_ref[...] + _dot(jnp.concatenate(parts, axis=1), w_ref[...])
    x3_ref[...] = x3

    xn = _rms_rows(x3, mg_ref[...])
    xn_ref[...] = xn.astype(BF16).reshape(xn_ref.shape)
    xh, xl = _split2(xn)
    logits = _dot(xh, rt_ref[0]) + _dot(xl, rt_ref[0]) + _dot(xh, rt_ref[1])
    lane = lax.broadcasted_iota(jnp.int32, logits.shape, 1)
    logits = jnp.where(lane < N_EXPERTS, logits, -3e38)
    v1 = jnp.max(logits, axis=-1, keepdims=True)
    i1 = jnp.min(jnp.where(logits == v1, lane, LANES), axis=-1, keepdims=True)
    rest = jnp.where(lane == i1, -3e38, logits)
    v2 = jnp.max(rest, axis=-1, keepdims=True)
    i2 = jnp.min(jnp.where(rest == v2, lane, LANES), axis=-1, keepdims=True)
    ex = jnp.exp(v2 - v1)
    den = 1.0 + ex
    e1_ref[...] = jnp.broadcast_to(i1, logits.shape)
    e2_ref[...] = jnp.broadcast_to(i2, logits.shape)
    w1_ref[...] = jnp.broadcast_to(1.0 / den, logits.shape)
    w2_ref[...] = jnp.broadcast_to(ex / den, logits.shape)


def _out_c(o2d, r2d, x2d, onorm_row, w_out, mnorm_row, router2):
    t = x2d.shape[0]
    tm = ROW_TILE
    dv = GLA_DV // GLA_HEADS
    bd = jnp.asarray(_block_diag_mean(dv, dv), BF16)
    row = lambda i: (i, 0)
    c2 = lambda i: (0, 0)
    slab = pl.BlockSpec((tm, LANES), row)
    return pl.pallas_call(
        _out_c_kernel,
        grid=(t // tm,),
        in_specs=[
            pl.BlockSpec((tm, GLA_DV), row),
            pl.BlockSpec((tm, GLA_DV), row),
            pl.BlockSpec((tm, D_MODEL), row),
            pl.BlockSpec((1, dv), c2),
            pl.BlockSpec((dv, dv), c2),
            pl.BlockSpec((D_MODEL, D_MODEL), c2),
            pl.BlockSpec((1, D_MODEL), c2),
            pl.BlockSpec((2, D_MODEL, LANES), lambda i: (0, 0, 0)),
        ],
        out_specs=[
            pl.BlockSpec((tm, D_MODEL), row),
            pl.BlockSpec((tm, D_MODEL // LANES, LANES), lambda i: (i, 0, 0)),
            slab, slab, slab, slab,
        ],
        out_shape=[
            jax.ShapeDtypeStruct((t, D_MODEL), F32),
            jax.ShapeDtypeStruct((t, D_MODEL // LANES, LANES), BF16),
            jax.ShapeDtypeStruct((t, LANES), jnp.int32),
            jax.ShapeDtypeStruct((t, LANES), jnp.int32),
            jax.ShapeDtypeStruct((t, LANES), F32),
            jax.ShapeDtypeStruct((t, LANES), F32),
        ],
        compiler_params=_cp(("parallel",)),
        name="out_c_router",
    )(o2d, r2d, x2d, onorm_row, bd, w_out, mnorm_row, router2)


def _combine_kernel(ya_ref, yb_ref, x_ref, w1_ref, w2_ref, o_ref):
    tm = x_ref.shape[0]
    reps = D_MODEL // LANES
    w1 = jnp.concatenate([w1_ref[...]] * reps, axis=1)
    w2 = jnp.concatenate([w2_ref[...]] * reps, axis=1)
    ya = ya_ref[...].reshape(tm, D_MODEL).astype(F32)
    yb = yb_ref[...].reshape(tm, D_MODEL).astype(F32)
    o_ref[...] = x_ref[...] + w1 * ya + w2 * yb


def _moe_combine(ys3, x2d, w1b, w2b):
    t = x2d.shape[0]
    tm = ROW_TILE
    row = lambda i: (i, 0)
    rows = D_MODEL // LANES
    second = t // tm
    return pl.pallas_call(
        _combine_kernel,
        grid=(t // tm,),
        in_specs=[
            pl.BlockSpec((tm, rows, LANES), lambda i: (i, 0, 0)),
            pl.BlockSpec((tm, rows, LANES), lambda i: (second + i, 0, 0)),
            pl.BlockSpec((tm, D_MODEL), row),
            pl.BlockSpec((tm, LANES), row),
            pl.BlockSpec((tm, LANES), row),
        ],
        out_specs=pl.BlockSpec((tm, D_MODEL), row),
        out_shape=jax.ShapeDtypeStruct((t, D_MODEL), F32),
        compiler_params=_cp(("parallel",)),
        name="moe_combine",
    )(ys3, ys3, x2d, w1b, w2b)


def _nsa_head_perm():
    order = []
    for t in range(4):
        order += [t, 4 + t]
    cols = np.concatenate([np.arange(h * HEAD_DIM, (h + 1) * HEAD_DIM) for h in order])
    return order, cols


def _layer_even(x, a_norm, a_w_in, a_q_norm, a_k_norm, a_pe_k, a_pe_v, ck1, ck2, cv1, cv2,
                gate_bias, fq_norm, fk_norm, f_bias, a_w_out, f_norm, f_wg, f_wu, f_wd):
    b, s, d = x.shape
    t = b * s
    x2d = x.reshape(t, d)
    order, qcols = _nsa_head_perm()

    def head_tiles(cols0):
        wq = a_w_in[:, cols0:cols0 + 512].reshape(d, FOX_HEADS, HEAD_DIM)
        return jnp.pad(wq, ((0, 0), (0, 0), (0, LANES - HEAD_DIM))).reshape(d, FOX_HEADS * LANES)

    w = jnp.concatenate([
        jnp.take(a_w_in, jnp.asarray(qcols), axis=1),
        a_w_in[:, 512:1280],
        head_tiles(1304),
        head_tiles(1816),
        a_w_in[:, 2328:2840],
        a_w_in[:, 1280:1304],
        a_w_in[:, 2840:2848],
        jnp.zeros((d, LANES - 32), F32),
    ], axis=1).astype(BF16)

    tile2 = lambda g: jnp.concatenate([g, g])
    lo_only = lambda g: jnp.concatenate([g, jnp.zeros((LANES - HEAD_DIM,), F32)])
    q_bias_lanes = jnp.zeros((LANES,), F32).at[FOX_BIAS_LANE:FOX_BIAS_LANE + 3].set(-1.0)
    head_gains = jnp.stack([tile2(a_q_norm), tile2(a_k_norm), lo_only(fq_norm), lo_only(fk_norm), q_bias_lanes]
                           + [jnp.zeros((LANES,), F32)] * 3)
    cos, sin = _rope_tables(np.arange(s))
    qa, kv6, fox, gf = _proj_a(x2d, a_norm.reshape(1, d), w, head_gains, cos, sin, s)

    fox3 = fox.reshape(b, s, FOX_SLAB)
    bias_row = jnp.zeros((1, LANES), F32).at[0, 24:32].set(f_bias)
    cum, k_aug = _fox_cumlog(gf.reshape(b, s, LANES), bias_row, fox3)
    cum_row = cum[:, :, 24:32].transpose(0, 2, 1)
    o_b = _fox_attention_t(fox3, k_aug, _values_t(fox3[:, :, 2048:2560]), cum_row).transpose(0, 2, 1)

    ncp = s // CMP_STRIDE
    n_cmp = (s - CMP_LEN) // CMP_STRIDE + 1
    n_slc = s // SLC_BLOCK
    kv3 = kv6.reshape(b, s, 768)
    kx = kv3[:, :, 0:128].reshape(b, ncp, CMP_STRIDE * LANES)
    vx = kv3[:, :, 128:256].reshape(b, ncp, CMP_STRIDE * LANES)

    def pe_rows(pe):
        p2 = jnp.concatenate([pe, pe], axis=1)
        return p2.reshape(2, CMP_STRIDE * LANES)

    def w1_blocks(w1):
        w4 = w1.reshape(2, CMP_STRIDE, HEAD_DIM, HEAD_DIM)
        z = jnp.zeros_like(w4)
        top = jnp.concatenate([w4, z], axis=-1)
        bot = jnp.concatenate([z, w4], axis=-1)
        return jnp.stack([top, bot], axis=2).reshape(2, CMP_STRIDE * LANES, LANES).astype(BF16)

    def w2_block(w2):
        z = jnp.zeros_like(w2)
        return jnp.concatenate([jnp.concatenate([w2, z], 1), jnp.concatenate([z, w2], 1)], 0).astype(BF16)

    ccos, csin = _rope_tables(np.arange(ncp) * CMP_STRIDE + CMP_LEN - 1)
    kcmp, vcmp = _nsa_compress(kx, vx, pe_rows(a_pe_k), pe_rows(a_pe_v), w1_blocks(ck1), w2_block(ck2),
                               w1_blocks(cv1), w2_block(cv2), tile2(a_k_norm).reshape(1, LANES), ccos, csin)

    c_start = np.arange(ncp) * CMP_STRIDE
    s_start = np.arange(64) * SLC_BLOCK
    ov = np.maximum(np.minimum(c_start[:, None] + CMP_LEN, s_start[None, :] + SLC_BLOCK)
                    - np.maximum(c_start[:, None], s_start[None, :]), 0).astype(np.float32) / CMP_LEN
    ov[n_cmp:, :] = 0.0
    ov[:, n_slc:] = 0.0
    qa3 = qa.reshape(b, s, 512)
    o_c, sb = _nsa_cmp_select(qa3, kcmp, vcmp, jnp.asarray(ov.T, BF16), n_cmp, n_slc)

    onehot = jnp.asarray((np.arange(s)[:, None] // SLC_BLOCK) == np.arange(64)[None, :], BF16)
    onehot = jnp.broadcast_to(onehot[None], (b, s, 64))
    ksl = kv3[:, :, 256:384]
    k_aug0 = jnp.concatenate([ksl[:, :, :64], onehot], axis=-1)
    k_aug1 = jnp.concatenate([onehot, ksl[:, :, 64:]], axis=-1)
    o_s = _nsa_selected_t(qa3, sb, k_aug0, k_aug1, _values_t(kv3[:, :, 384:512]))

    o_w = _nsa_window(qa3, kv3[:, :, 512:640], _values_t(kv3[:, :, 640:768]))

    gexp = np.zeros((3, LANES, 512), np.float32)
    for tile_i in range(4):
        for half, h in enumerate((tile_i, 4 + tile_i)):
            for j in range(3):
                gexp[j, h * 3 + j, tile_i * LANES + half * HEAD_DIM:tile_i * LANES + (half + 1) * HEAD_DIM] = 1.0
    gate_bias_row = jnp.zeros((1, LANES), F32).at[0, 0:24].set(gate_bias)
    w_out = jnp.concatenate([jnp.take(a_w_out[:512], jnp.asarray(qcols), axis=0), a_w_out[512:]], 0).astype(BF16)
    x2 = _out_ffn(o_c.reshape(t, 512), o_s.reshape(t, 512), o_w.reshape(t, 512), o_b.reshape(t, 512),
                  gf, gate_bias_row, jnp.asarray(gexp, BF16), x2d, w_out,
                  f_norm.reshape(1, d), f_wg.astype(BF16), f_wu.astype(BF16), f_wd.astype(BF16))
    return x2.reshape(b, s, d)


def _layer_odd(x, c_norm, c_w_in, c_w_a2, c_b_a, c_o_norm, c_w_out, m_norm, m_router, m_wg, m_wu, m_wd):
    b, s, d = x.shape
    t = b * s
    x2d = x.reshape(t, d)
    w = jnp.pad(c_w_in, ((0, 0), (0, 3200 - c_w_in.shape[1]))).astype(BF16)
    wa = jnp.pad(c_w_a2, ((0, LANES - GLA_RANK), (0, 0))).astype(BF16)
    qk, v, r, la = _proj_c(x2d, c_norm.reshape(1, d), w, wa, c_b_a.reshape(1, GLA_DK))
    o = _gla(qk.reshape(b, s, 1024), v.reshape(b, s, 1024), la.reshape(b, s, GLA_DK))

    rt = jnp.pad(m_router, ((0, 0), (0, LANES - N_EXPERTS)))
    rt_hi = rt.astype(BF16)
    rt_lo = (rt - rt_hi.astype(F32)).astype(BF16)
    x3, xn, e1, e2, w1b, w2b = _out_c(o.reshape(t, GLA_DV), r, x2d, c_o_norm.reshape(1, -1),
                                      c_w_out.astype(BF16), m_norm.reshape(1, d), jnp.stack([rt_hi, rt_lo]))

    tm = EXPERT_TILE
    n_slot = TOP_K * t + N_EXPERTS * tm
    n_tiles = n_slot // tm
    flat_e = jnp.concatenate([e1[:, 0], e2[:, 0]])
    onehot = (flat_e[:, None] == jnp.arange(N_EXPERTS)[None, :]).astype(jnp.int32)
    counts = jnp.sum(onehot, axis=0)
    padded = ((counts + tm - 1) // tm) * tm
    ends = jnp.cumsum(padded)
    offs = ends - padded
    n_used = (ends[-1] // tm).astype(jnp.int32)
    tile_start = jnp.arange(n_tiles, dtype=jnp.int32) * tm
    tile_e = jnp.sum((tile_start[:, None] >= ends[None, :]).astype(jnp.int32), axis=1)
    last_e = jnp.sum((((n_used - 1) * tm) >= ends).astype(jnp.int32))
    tile_e = jnp.where(jnp.arange(n_tiles) < n_used, tile_e, last_e).astype(jnp.int32)
    order = jnp.argsort(flat_e, stable=True).astype(jnp.int32)
    order = jnp.concatenate([order, jnp.zeros((n_slot,), jnp.int32)])
    starts = jnp.cumsum(counts) - counts
    row_of = jnp.zeros((n_slot + TOP_K * t,), jnp.int32)
    for e in range(N_EXPERTS):
        piece = lax.dynamic_slice(order, (starts[e],), (TOP_K * t,))
        row_of = lax.dynamic_update_slice(row_of, piece, (offs[e],))
    row_of = row_of[:n_slot]
    src_tok = (row_of % t).reshape(n_tiles, 1, tm)
    dst_row = row_of.reshape(n_tiles, 1, tm)
    tile_off = tile_start - jnp.sum(jnp.where(tile_start[:, None] >= ends[None, :], padded[None, :], 0), axis=1)
    tile_count = jnp.clip(counts[jnp.minimum(tile_e, N_EXPERTS - 1)] - tile_off, 0, tm).astype(jnp.int32)
    tile_count = jnp.where(jnp.arange(n_tiles) < n_used, tile_count, 0)

    ys = _ffn_experts(tile_e, n_used.reshape(1), tile_count, src_tok, dst_row, xn,
                      m_wg.astype(BF16), m_wu.astype(BF16), m_wd.astype(BF16))
    out = _moe_combine(ys, x3, w1b, w2b)
    return out.reshape(b, s, d)


def kernel(x, a_norm, a_w_in, a_q_norm, a_k_norm, a_pe_k, a_pe_v, a_cmp_k_w1, a_cmp_k_w2, a_cmp_v_w1, a_cmp_v_w2, a_gate_bias, a_fox_q_norm, a_fox_k_norm, a_fox_f_bias, a_w_out, f_norm, f_w_gate, f_w_up, f_w_down, c_norm, c_w_in, c_w_a2, c_b_a, c_o_norm, c_w_out, m_norm, m_router, m_w_gate, m_w_up, m_w_down):
    x = _layer_even(x, a_norm[0], a_w_in[0], a_q_norm[0], a_k_norm[0], a_pe_k[0], a_pe_v[0],
                    a_cmp_k_w1[0], a_cmp_k_w2[0], a_cmp_v_w1[0], a_cmp_v_w2[0], a_gate_bias[0],
                    a_fox_q_norm[0], a_fox_k_norm[0], a_fox_f_bias[0], a_w_out[0],
                    f_norm[0], f_w_gate[0], f_w_up[0], f_w_down[0])
    x = _layer_odd(x, c_norm[0], c_w_in[0], c_w_a2[0], c_b_a[0], c_o_norm[0], c_w_out[0],
                   m_norm[0], m_router[0], m_w_gate[0], m_w_up[0], m_w_down[0])
    return x
```

```python
import functools

import numpy as np
import jax
import jax.numpy as jnp
from jax import lax
from jax.experimental import pallas as pl
from jax.experimental.pallas import tpu as pltpu

F32 = jnp.float32
BF16 = jnp.bfloat16

D_MODEL = 1024
HEAD_DIM = 64
ROPE_DIM = 16
ROPE_THETA = 500000.0
NORM_EPS = 1e-6
NEG = -1e30
FORCE_SCORE = 1e6

NSA_HEADS = 8
CMP_LEN = 32
CMP_STRIDE = 16
SLC_BLOCK = 64
SLC_TOPK = 16
WINDOW = 512
FOX_HEADS = 8

GLA_HEADS = 4
GLA_DK = 512
GLA_DV = 1024
GLA_RANK = 16
GLA_TAU = 16.0
GLA_CHUNK = 64
GLA_SUB = 16

D_FF = 2816
N_EXPERTS = 8
TOP_K = 2

LANES = 128
FF_CHUNK = 256
VMEM_LIMIT = 56 * 1024 * 1024

ROW_TILE = 512
ATT_Q = 128
FOX_Q = 512
FOX_K = 256
SLC_K = 256
EXPERT_TILE = 512

NT_DIMS = (((1,), (1,)), ((), ()))


def _cp(sem, **kw):
    return pltpu.CompilerParams(dimension_semantics=sem, vmem_limit_bytes=VMEM_LIMIT, **kw)


def _dot(a, b):
    return jnp.dot(a, b, preferred_element_type=F32)


def _dot_nt(a, b):
    return lax.dot_general(a, b, NT_DIMS, preferred_element_type=F32)


def _split2(x):
    hi = x.astype(BF16)
    lo = (x - hi.astype(F32)).astype(BF16)
    return hi, lo


def _split3(x):
    hi = x.astype(BF16)
    r = x - hi.astype(F32)
    mid = r.astype(BF16)
    lo = (r - mid.astype(F32)).astype(BF16)
    return hi, mid, lo


def _rms_rows(x, gain):
    return x * lax.rsqrt(jnp.mean(x * x, axis=-1, keepdims=True) + NORM_EPS) * gain


def _log_sigmoid(x):
    return jnp.minimum(x, 0.0) - jnp.log1p(jnp.exp(-jnp.abs(x)))


def _sigmoid(x):
    return 1.0 / (1.0 + jnp.exp(-x))


def _seg_norm(y, gain, bd):
    ms = _dot((y * y).astype(BF16), bd)
    return y * lax.rsqrt(ms + NORM_EPS) * gain


def _rope(y, cos, sin, perm):
    return y * cos + _dot(y.astype(BF16), perm) * sin


def _block_diag_mean(width, seg):
    i = np.arange(width)
    return ((i[:, None] // seg) == (i[None, :] // seg)).astype(np.float32) / seg


def _rope_perm():
    p = np.zeros((LANES, LANES), np.float32)
    half = ROPE_DIM // 2
    for j in range(LANES):
        d = j % HEAD_DIM
        if d < half:
            p[j + half, j] = -1.0
        elif d < ROPE_DIM:
            p[j - half, j] = 1.0
    return p


def _rope_tables(pos):
    inv = ROPE_THETA ** (-np.arange(0, ROPE_DIM, 2, dtype=np.float64) / ROPE_DIM)
    ang = pos.astype(np.float64)[:, None] * inv[None, :]
    cos8, sin8 = np.cos(ang), np.sin(ang)
    n = pos.shape[0]
    ones = np.ones((n, HEAD_DIM - ROPE_DIM))
    zeros = np.zeros((n, HEAD_DIM - ROPE_DIM))
    c = np.concatenate([cos8, cos8, ones], axis=1)
    s = np.concatenate([sin8, sin8, zeros], axis=1)
    return (jnp.asarray(np.concatenate([c, c], axis=1), F32),
            jnp.asarray(np.concatenate([s, s], axis=1), F32))


PROJ_A_COLS = 3968
FOX_SLAB = 2560


def _proj_a_kernel(x_ref, g_ref, w_ref, hg_ref, cos_ref, sin_ref, bd_ref, bdlo_ref, perm_ref,
                   qa_ref, kv_ref, fox_ref, gf_ref):
    xn = _rms_rows(x_ref[...], g_ref[...]).astype(BF16)
    bd = bd_ref[...]
    perm = perm_ref[...]
    cos = cos_ref[...]
    sin = sin_ref[...]

    def tile(acc, i):
        return acc[:, i * LANES:(i + 1) * LANES]

    acc = _dot(xn, w_ref[:, 0:512])
    for i in range(4):
        y = _rope(_seg_norm(tile(acc, i), hg_ref[0:1, :], bd), cos, sin, perm)
        qa_ref[:, i * LANES:(i + 1) * LANES] = (y * 0.125).astype(BF16)

    acc = _dot(xn, w_ref[:, 512:1280])
    for i in range(6):
        y = tile(acc, i)
        if i in (2, 4):
            y = _rope(_seg_norm(y, hg_ref[1:2, :], bd), cos, sin, perm)
        kv_ref[:, i * LANES:(i + 1) * LANES] = y.astype(BF16)

    bd_lo = bdlo_ref[...]
    for c in range(4):
        acc = _dot(xn, w_ref[:, 1280 + 512 * c:1280 + 512 * (c + 1)])
        for i in range(4):
            y = tile(acc, i)
            if c < 2:
                y = _seg_norm(y, hg_ref[2:3, :], bd_lo) * 0.125 + hg_ref[4:5, :]
            else:
                y = _seg_norm(y, hg_ref[3:4, :], bd_lo)
            fox_ref[:, 512 * c + i * LANES:512 * c + (i + 1) * LANES] = y.astype(BF16)
    fox_ref[:, 2048:2560] = _dot(xn, w_ref[:, 3328:3840]).astype(BF16)

    gf_ref[...] = _dot(xn, w_ref[:, 3840:3968])


def _proj_a(x2d, norm_g, w_perm, head_gains, cos, sin, seq):
    t = x2d.shape[0]
    tm = ROW_TILE
    n_pos = seq // tm
    bd = jnp.asarray(_block_diag_mean(LANES, HEAD_DIM), BF16)
    lo = np.zeros((LANES, LANES), np.float32)
    lo[:HEAD_DIM, :] = 1.0 / HEAD_DIM
    perm = jnp.asarray(_rope_perm(), BF16)
    full = lambda i: (0, 0)
    return pl.pallas_call(
        _proj_a_kernel,
        grid=(t // tm,),
        in_specs=[
            pl.BlockSpec((tm, D_MODEL), lambda i: (i, 0)),
            pl.BlockSpec((1, D_MODEL), full),
            pl.BlockSpec((D_MODEL, PROJ_A_COLS), full),
            pl.BlockSpec((8, LANES), full),
            pl.BlockSpec((tm, LANES), lambda i: (i % n_pos, 0)),
            pl.BlockSpec((tm, LANES), lambda i: (i % n_pos, 0)),
            pl.BlockSpec((LANES, LANES), full),
            pl.BlockSpec((LANES, LANES), full),
            pl.BlockSpec((LANES, LANES), full),
        ],
        out_specs=[
            pl.BlockSpec((tm, 512), lambda i: (i, 0)),
            pl.BlockSpec((tm, 768), lambda i: (i, 0)),
            pl.BlockSpec((tm, FOX_SLAB), lambda i: (i, 0)),
            pl.BlockSpec((tm, LANES), lambda i: (i, 0)),
        ],
        out_shape=[
            jax.ShapeDtypeStruct((t, 512), BF16),
            jax.ShapeDtypeStruct((t, 768), BF16),
            jax.ShapeDtypeStruct((t, FOX_SLAB), BF16),
            jax.ShapeDtypeStruct((t, LANES), F32),
        ],
        compiler_params=_cp(("parallel",)),
        name="proj_a",
    )(x2d, norm_g, w_perm, head_gains, cos, sin, bd, jnp.asarray(lo, BF16), perm)


CUM_CHUNK = 256
FOX_BIAS_LANE = 64


def _cum_kernel(gf_ref, bias_ref, tri_ref, k_ref, place_ref, out_ref, ka_ref):
    tri = tri_ref[...]
    n_chunk = gf_ref.shape[0] // CUM_CHUNK
    carry = jnp.zeros((1, LANES), F32)
    for c in range(n_chunk):
        rows = slice(c * CUM_CHUNK, (c + 1) * CUM_CHUNK)
        lf = _log_sigmoid(gf_ref[rows, :] + bias_ref[...])
        hi, mid, lo = _split3(lf)
        cs = _dot(tri, hi) + _dot(tri, mid) + _dot(tri, lo) + carry
        out_ref[rows, :] = cs
        carry = cs[CUM_CHUNK - 1:CUM_CHUNK, :]
        c_hi, c_mid, c_lo = _split3(cs)
        placed = _dot(c_hi, place_ref[0]) + _dot(c_mid, place_ref[1]) + _dot(c_lo, place_ref[2])
        ka_ref[rows, :] = (k_ref[rows, :].astype(F32) + placed).astype(BF16)


def _fox_cumlog(gf3, bias_row, fox3):
    b, s, _ = gf3.shape
    tri = jnp.asarray(np.tril(np.ones((CUM_CHUNK, CUM_CHUNK), np.float32)), BF16)
    place = np.zeros((3, LANES, FOX_HEADS * LANES), np.float32)
    for h in range(FOX_HEADS):
        for piece in range(3):
            place[piece, 24 + h, h * LANES + FOX_BIAS_LANE + piece] = 1.0
    return pl.pallas_call(
        _cum_kernel,
        grid=(b,),
        in_specs=[
            pl.BlockSpec((None, s, LANES), lambda i: (i, 0, 0)),
            pl.BlockSpec((1, LANES), lambda i: (0, 0)),
            pl.BlockSpec((CUM_CHUNK, CUM_CHUNK), lambda i: (0, 0)),
            pl.BlockSpec((None, s, FOX_HEADS * LANES), lambda i: (i, 0, 1)),
            pl.BlockSpec((3, LANES, FOX_HEADS * LANES), lambda i: (0, 0, 0)),
        ],
        out_specs=[
            pl.BlockSpec((None, s, LANES), lambda i: (i, 0, 0)),
            pl.BlockSpec((None, s, FOX_HEADS * LANES), lambda i: (i, 0, 0)),
        ],
        out_shape=[
            jax.ShapeDtypeStruct((b, s, LANES), F32),
            jax.ShapeDtypeStruct((b, s, FOX_HEADS * LANES), BF16),
        ],
        compiler_params=_cp(("parallel",)),
        name="fox_cumlog",
    )(gf3, bias_row, tri, fox3, jnp.asarray(place, BF16))


def _gelu_tanh(x):
    return 0.5 * x * (1.0 + jnp.tanh(0.7978845608028654 * (x + 0.044715 * x * x * x)))


def _cmp_kernel(kx_ref, vx_ref, pek_ref, pev_ref, w1k_ref, w2k_ref, w1v_ref, w2v_ref,
                kg_ref, cos_ref, sin_ref, bd_ref, perm_ref, kc_ref, vc_ref):
    ncp = kx_ref.shape[0]

    def compress(x_ref, pe_ref, w1_ref, w2_ref):
        x = x_ref[...].astype(F32)
        first = _dot((x + pe_ref[0:1, :]).astype(BF16), w1_ref[0])
        second = _dot((x + pe_ref[1:2, :]).astype(BF16), w1_ref[1])
        pre = first + pltpu.roll(second, ncp - 1, 0)
        return _dot(_gelu_tanh(pre).astype(BF16), w2_ref[...])

    k = compress(kx_ref, pek_ref, w1k_ref, w2k_ref)
    k = _rope(_seg_norm(k, kg_ref[...], bd_ref[...]), cos_ref[...], sin_ref[...], perm_ref[...])
    kc_ref[...] = k.astype(BF16)
    vc_ref[...] = compress(vx_ref, pev_ref, w1v_ref, w2v_ref).astype(BF16)


def _nsa_compress(kx, vx, pek, pev, w1k, w2k, w1v, w2v, kgain, ccos, csin):
    b, ncp, width = kx.shape
    bd = jnp.asarray(_block_diag_mean(LANES, HEAD_DIM), BF16)
    perm = jnp.asarray(_rope_perm(), BF16)
    c2 = lambda i: (0, 0)
    c3 = lambda i: (0, 0, 0)
    return pl.pallas_call(
        _cmp_kernel,
        grid=(b,),
        in_specs=[
            pl.BlockSpec((None, ncp, width), lambda i: (i, 0, 0)),
            pl.BlockSpec((None, ncp, width), lambda i: (i, 0, 0)),
            pl.BlockSpec((2, width), c2),
            pl.BlockSpec((2, width), c2),
            pl.BlockSpec((2, width, LANES), c3),
            pl.BlockSpec((LANES, LANES), c2),
            pl.BlockSpec((2, width, LANES), c3),
            pl.BlockSpec((LANES, LANES), c2),
            pl.BlockSpec((1, LANES), c2),
            pl.BlockSpec((ncp, LANES), c2),
            pl.BlockSpec((ncp, LANES), c2),
            pl.BlockSpec((LANES, LANES), c2),
            pl.BlockSpec((LANES, LANES), c2),
        ],
        out_specs=[
            pl.BlockSpec((None, ncp, LANES), lambda i: (i, 0, 0)),
            pl.BlockSpec((None, ncp, LANES), lambda i: (i, 0, 0)),
        ],
        out_shape=[
            jax.ShapeDtypeStruct((b, ncp, LANES), BF16),
            jax.ShapeDtypeStruct((b, ncp, LANES), BF16),
        ],
        compiler_params=_cp(("parallel",)),
        name="nsa_compress",
    )(kx, vx, pek, pev, w1k, w2k, w1v, w2v, kgain, ccos, csin, bd, perm)


def _nsa_cmp_kernel(q_ref, kc_ref, vc_ref, ovt_ref, eye_ref, kw_ref, vw_ref, oc_ref, sb_ref, ow_ref, *, n_cmp, n_slc):
    tq = ATT_Q
    _window_attention(q_ref, kw_ref, vw_ref, eye_ref, ow_ref)
    q0 = pl.program_id(1) * tq
    kc = kc_ref[...]
    vc = vc_ref[...]
    ncp = kc.shape[0]
    ovt = ovt_ref[...]
    lo_half = lax.broadcasted_iota(jnp.int32, (tq, LANES), 1) < HEAD_DIM

    n_lane = lax.broadcasted_iota(jnp.int32, (tq, ncp), 1)
    t_sub = q0 + lax.broadcasted_iota(jnp.int32, (tq, ncp), 0)
    mask = (n_lane * CMP_STRIDE + (CMP_LEN - 1) <= t_sub) & (n_lane < n_cmp)
    n_sub = lax.broadcasted_iota(jnp.int32, (ncp, tq), 0)
    t_lane = q0 + lax.broadcasted_iota(jnp.int32, (ncp, tq), 1)
    mask_t = (n_sub * CMP_STRIDE + (CMP_LEN - 1) <= t_lane) & (n_sub < n_cmp)

    imp_t = [jnp.zeros((64, tq), F32), jnp.zeros((64, tq), F32)]
    for t in range(4):
        qt = q_ref[:, t * LANES:(t + 1) * LANES]
        outs = []
        for g in range(2):
            qm = jnp.where(lo_half if g == 0 else jnp.logical_not(lo_half), qt, jnp.zeros_like(qt))
            s = jnp.where(mask, _dot_nt(qm, kc), NEG)
            p = jnp.where(mask, jnp.exp(s - jnp.max(s, axis=-1, keepdims=True)), 0.0)
            p = p / jnp.maximum(jnp.sum(p, axis=-1, keepdims=True), 1e-30)
            outs.append(_dot(p.astype(BF16), vc))
            st = jnp.where(mask_t, _dot_nt(kc, qm), NEG)
            pt = jnp.where(mask_t, jnp.exp(st - jnp.max(st, axis=0, keepdims=True)), 0.0)
            pt = pt / jnp.maximum(jnp.sum(pt, axis=0, keepdims=True), 1e-30)
            hi, lo = _split2(pt)
            imp_t[g] = imp_t[g] + _dot(ovt, hi) + _dot(ovt, lo)
        oc_ref[:, t * LANES:(t + 1) * LANES] = jnp.where(lo_half, outs[0], outs[1]).astype(BF16)

    blk = lax.broadcasted_iota(jnp.int32, (64, tq), 0)
    cur = (q0 + lax.broadcasted_iota(jnp.int32, (64, tq), 1)) >> 6
    forced = (blk == 0) | (blk == cur) | (blk == cur - 1)
    future = blk > cur
    exists = blk < n_slc
    biases = []
    for g in range(2):
        v = jnp.where(forced, FORCE_SCORE, jnp.where(future, -FORCE_SCORE, imp_t[g]))
        v = jnp.where(exists, v, -3e38)
        sel = jnp.zeros((64, tq), jnp.int32)
        for _ in range(min(SLC_TOPK, n_slc)):
            m = jnp.max(v, axis=0, keepdims=True)
            first = jnp.min(jnp.where(v == m, blk, 64), axis=0, keepdims=True)
            hit = blk == first
            sel = jnp.where(hit, 1, sel)
            v = jnp.where(hit, -3.2e38, v)
        keep = (sel == 1) & jnp.logical_not(future) & exists
        biases.append(jnp.where(keep, 0.0, NEG).astype(BF16))
    sb_t = jnp.concatenate([biases[1], biases[0]], axis=0)
    sb_ref[...] = _dot_nt(eye_ref[...], sb_t).astype(BF16)


def _nsa_cmp_select(qa3, kcmp, vcmp, ovt, k_win, v_win_t, n_cmp, n_slc):
    b, s, _ = qa3.shape
    ncp = kcmp.shape[1]
    eye = jnp.asarray(np.eye(ATT_Q, dtype=np.float32), BF16)
    kern = functools.partial(_nsa_cmp_kernel, n_cmp=n_cmp, n_slc=n_slc)
    return pl.pallas_call(
        kern,
        grid=(b, s // ATT_Q),
        in_specs=[
            pl.BlockSpec((None, ATT_Q, 512), lambda i, j: (i, j, 0)),
            pl.BlockSpec((None, ncp, LANES), lambda i, j: (i, 0, 0)),
            pl.BlockSpec((None, ncp, LANES), lambda i, j: (i, 0, 0)),
            pl.BlockSpec((64, ncp), lambda i, j: (0, 0)),
            pl.BlockSpec((ATT_Q, ATT_Q), lambda i, j: (0, 0)),
            pl.BlockSpec((None, s, LANES), lambda i, j: (i, 0, 0)),
            pl.BlockSpec((None, 2, V_ROWS, s), lambda i, j: (i, 0, 0, 0)),
        ],
        out_specs=[
            pl.BlockSpec((None, ATT_Q, 512), lambda i, j: (i, j, 0)),
            pl.BlockSpec((None, ATT_Q, LANES), lambda i, j: (i, j, 0)),
            pl.BlockSpec((None, ATT_Q, 512), lambda i, j: (i, j, 0)),
        ],
        out_shape=[
            jax.ShapeDtypeStruct((b, s, 512), BF16),
            jax.ShapeDtypeStruct((b, s, LANES), BF16),
            jax.ShapeDtypeStruct((b, s, 512), BF16),
        ],
        compiler_params=_cp(("parallel", "parallel")),
        name="nsa_cmp_win_select",
    )(qa3, kcmp, vcmp, ovt, eye, k_win, v_win_t)


V_ROWS = 80


def _flash_step_t(carry, s_t, v_t, *, query_bias=None, mask=None):
    m, acc = carry
    if mask is not None:
        s_t = jnp.where(mask, s_t, NEG)
    col_max = jnp.max(s_t, axis=0, keepdims=True)
    if query_bias is not None:
        m_new = jnp.maximum(m, col_max + query_bias)
        shift = m_new - query_bias
    else:
        m_new = jnp.maximum(m, col_max)
        shift = m_new
    p = jnp.exp(s_t - shift).astype(BF16)
    acc = jnp.exp(m - m_new) * acc + _dot(v_t, p)
    return m_new, acc


def _flash_init_t(queries):
    return (jnp.full((1, queries), NEG, F32), jnp.zeros((V_ROWS, queries), F32))


def _values_t(v):
    b, s, w = v.shape
    heads = w // HEAD_DIM
    vt = v.reshape(b, s, heads, HEAD_DIM).transpose(0, 2, 3, 1)
    ones = jnp.ones((b, heads, 1, s), v.dtype)
    pad = jnp.zeros((b, heads, V_ROWS - HEAD_DIM - 1, s), v.dtype)
    return jnp.concatenate([vt, ones, pad], axis=2)


def _nsa_slc_t_kernel(q_ref, sb_ref, k0_ref, k1_ref, v_ref, eye_ref, o_ref, sa_ref, sc_ref):
    tq, tk = ATT_Q, SLC_K
    q0 = pl.program_id(1) * tq
    sb = sb_ref[...]
    lo_half = lax.broadcasted_iota(jnp.int32, (tq, LANES), 1) < HEAD_DIM
    cols = 4 * tq
    n_before = q0 // tk
    k_refs = (k0_ref, k1_ref)
    slots = (sa_ref, sc_ref)

    qq = []
    for g in range(2):
        parts = []
        for t in range(4):
            qt = q_ref[:, t * LANES:(t + 1) * LANES]
            parts.append(jnp.where(lo_half, qt, sb) if g == 0 else jnp.where(lo_half, sb, qt))
        qq.append(jnp.concatenate(parts, axis=0))

    def issue(j, slot):
        off = pl.multiple_of(j * tk, tk)
        for g in range(2):
            slots[slot][g] = _dot_nt(k_refs[g][pl.ds(off, tk), :], qq[g])

    def consume(j, carry, slot, mask=None):
        off = pl.multiple_of(j * tk, tk)
        return tuple(_flash_step_t(carry[g], slots[slot][g], v_ref[g, :, pl.ds(off, tk)], mask=mask)
                     for g in range(2))

    def step(jj, carry):
        j = 2 * jj
        issue(j + 1, 1)
        carry = consume(j, carry, 0)
        issue(j + 2, 0)
        return consume(j + 1, carry, 1)

    issue(0, 0)
    carry = lax.fori_loop(0, n_before // 2, step, (_flash_init_t(cols), _flash_init_t(cols)))

    q_in_chunk = q0 - n_before * tk
    query = lax.broadcasted_iota(jnp.int32, (tk, cols), 1) & (tq - 1)
    causal = lax.broadcasted_iota(jnp.int32, (tk, cols), 0) - query <= q_in_chunk

    def even_tail(carry):
        return consume(n_before, carry, 0, causal)

    def odd_tail(carry):
        issue(n_before, 1)
        carry = consume(n_before - 1, carry, 0)
        return consume(n_before, carry, 1, causal)

    carry = lax.cond((n_before & 1) == 0, even_tail, odd_tail, carry)

    outs = [(acc[0:HEAD_DIM] / acc[HEAD_DIM:HEAD_DIM + 1]).astype(BF16) for (_, acc) in carry]
    eye = eye_ref[...]
    for t in range(4):
        cs = slice(t * tq, (t + 1) * tq)
        both = jnp.concatenate([outs[0][:, cs], outs[1][:, cs]], axis=0)
        o_ref[:, t * LANES:(t + 1) * LANES] = _dot_nt(eye, both).astype(BF16)


def _nsa_selected_t(qa3, sb, k_aug0, k_aug1, v_t):
    b, s, _ = qa3.shape
    blk = lambda i, j: (i, j, 0)
    whole = lambda i, j: (i, 0, 0)
    eye = jnp.asarray(np.eye(ATT_Q, dtype=np.float32), BF16)
    return pl.pallas_call(
        _nsa_slc_t_kernel,
        grid=(b, s // ATT_Q),
        in_specs=[
            pl.BlockSpec((None, ATT_Q, 512), blk),
            pl.BlockSpec((None, ATT_Q, LANES), blk),
            pl.BlockSpec((None, s, LANES), whole),
            pl.BlockSpec((None, s, LANES), whole),
            pl.BlockSpec((None, 2, V_ROWS, s), lambda i, j: (i, 0, 0, 0)),
            pl.BlockSpec((ATT_Q, ATT_Q), lambda i, j: (0, 0)),
        ],
        out_specs=pl.BlockSpec((None, ATT_Q, 512), blk),
        out_shape=jax.ShapeDtypeStruct((b, s, 512), BF16),
        scratch_shapes=[pltpu.VMEM((2, SLC_K, 4 * ATT_Q), F32), pltpu.VMEM((2, SLC_K, 4 * ATT_Q), F32)],
        compiler_params=_cp(("parallel", "parallel")),
        name="nsa_selected",
    )(qa3, sb, k_aug0, k_aug1, v_t, eye)


def _window_attention(q_ref, k_ref, v_ref, eye_ref, o_ref):
    tq = ATT_Q
    span = WINDOW + tq
    q0 = pl.program_id(1) * tq
    start = pl.multiple_of(jnp.maximum(q0 - WINDOW, 0), tq)
    k = k_ref[pl.ds(start, span), :]
    cols = 4 * tq
    lo_half = lax.broadcasted_iota(jnp.int32, (tq, LANES), 1) < HEAD_DIM
    query = lax.broadcasted_iota(jnp.int32, (span, cols), 1) & (tq - 1)
    dist = (q0 - start) + query - lax.broadcasted_iota(jnp.int32, (span, cols), 0)
    mask = (dist >= 0) & (dist < WINDOW)
    scores = []
    for g in range(2):
        keep = lo_half if g == 0 else jnp.logical_not(lo_half)
        parts = []
        for t in range(4):
            qt = q_ref[:, t * LANES:(t + 1) * LANES]
            parts.append(jnp.where(keep, qt, jnp.zeros_like(qt)))
        scores.append(_dot_nt(k, jnp.concatenate(parts, axis=0)))
    outs = []
    for g in range(2):
        s_t = jnp.where(mask, scores[g], NEG)
        p = jnp.exp(s_t - jnp.max(s_t, axis=0, keepdims=True)).astype(BF16)
        acc = _dot(v_ref[g, :, pl.ds(start, span)], p)
        outs.append((acc[0:HEAD_DIM] / acc[HEAD_DIM:HEAD_DIM + 1]).astype(BF16))
    eye = eye_ref[...]
    for t in range(4):
        cs = slice(t * tq, (t + 1) * tq)
        both = jnp.concatenate([outs[0][:, cs], outs[1][:, cs]], axis=0)
        o_ref[:, t * LANES:(t + 1) * LANES] = _dot_nt(eye, both).astype(BF16)


def _fox_t_kernel(q_ref, k_ref, v_ref, cr_ref, o_ref, sa_ref, sb_ref):
    tq, tk = FOX_Q, FOX_K
    q0 = pl.program_id(1) * tq
    n_before = q0 // tk
    n_diag = tq // tk
    for pair in range(FOX_HEADS // 2):
        heads = (2 * pair, 2 * pair + 1)
        qs = tuple(q_ref[:, h * LANES:(h + 1) * LANES] for h in heads)
        cq = tuple(cr_ref[h:h + 1, pl.ds(pl.multiple_of(q0, tq), tq)] for h in heads)

        def scores(j, i, q_lo=0):
            off = pl.multiple_of(j * tk, tk)
            h = heads[i]
            return _dot_nt(k_ref[pl.ds(off, tk), h * LANES:(h + 1) * LANES], qs[i][q_lo:])

        slots = (sa_ref, sb_ref)

        def consume(j, carry, slot):
            off = pl.multiple_of(j * tk, tk)
            return tuple(_flash_step_t(carry[i], slots[slot][i], v_ref[heads[i], :, pl.ds(off, tk)],
                                       query_bias=cq[i]) for i in range(2))

        def step(jj, carry):
            j = 2 * jj
            for i in range(2):
                sb_ref[i] = scores(j + 1, i)
            carry = consume(j, carry, 0)
            for i in range(2):
                sa_ref[i] = scores(j + 2, i)
            return consume(j + 1, carry, 1)

        for i in range(2):
            sa_ref[i] = scores(0, i)
        carry = lax.fori_loop(0, n_before // 2, step, (_flash_init_t(tq), _flash_init_t(tq)))

        late = [[scores(n_before + d, i, d * tk) for i in range(2)] for d in range(1, n_diag)]
        for d in range(n_diag):
            q_lo = d * tk
            off = pl.multiple_of((n_before + d) * tk, tk)
            shape = (tk, tq - q_lo)
            mask = (lax.broadcasted_iota(jnp.int32, shape, 0) + (d * tk - q_lo)
                    <= lax.broadcasted_iota(jnp.int32, shape, 1))
            out = []
            for i in range(2):
                m, acc = carry[i]
                s_t = sa_ref[i] if d == 0 else late[d - 1][i]
                new = _flash_step_t((m[:, q_lo:], acc[:, q_lo:]), s_t, v_ref[heads[i], :, pl.ds(off, tk)],
                                    query_bias=cq[i][:, q_lo:], mask=mask)
                if q_lo:
                    new = tuple(jnp.concatenate([old[:, :q_lo], part], axis=1)
                                for old, part in zip((m, acc), new))
                out.append(new)
            carry = tuple(out)
        for i, h in enumerate(heads):
            acc = carry[i][1]
            o_ref[h * HEAD_DIM:(h + 1) * HEAD_DIM, :] = (
                acc[0:HEAD_DIM] / acc[HEAD_DIM:HEAD_DIM + 1]).astype(BF16)


def _fox_attention_t(fox3, k_aug, v_t, cum_row):
    b, s, _ = fox3.shape
    width = FOX_HEADS * LANES
    return pl.pallas_call(
        _fox_t_kernel,
        grid=(b, s // FOX_Q),
        in_specs=[
            pl.BlockSpec((None, FOX_Q, width), lambda i, j: (i, j, 0)),
            pl.BlockSpec((None, s, width), lambda i, j: (i, 0, 0)),
            pl.BlockSpec((None, FOX_HEADS, V_ROWS, s), lambda i, j: (i, 0, 0, 0)),
            pl.BlockSpec((None, FOX_HEADS, s), lambda i, j: (i, 0, 0)),
        ],
        out_specs=pl.BlockSpec((None, FOX_HEADS * HEAD_DIM, FOX_Q), lambda i, j: (i, 0, j)),
        out_shape=jax.ShapeDtypeStruct((b, FOX_HEADS * HEAD_DIM, s), BF16),
        scratch_shapes=[pltpu.VMEM((2, FOX_K, FOX_Q), F32), pltpu.VMEM((2, FOX_K, FOX_Q), F32)],
        compiler_params=_cp(("parallel", "parallel")),
        name="fox_attention",
    )(fox3, k_aug, v_t, cum_row)


def _mix_out(oc_ref, os_ref, ow_ref, ob_ref, gf_ref, gb_ref, gx_ref, x_ref, w_ref):
    gates = _sigmoid(gf_ref[...] + gb_ref[...])
    hi, lo = _split2(gates)
    o_a = None
    for j, br in enumerate((oc_ref, os_ref, ow_ref)):
        gexp = _dot(hi, gx_ref[j]) + _dot(lo, gx_ref[j])
        term = gexp * br[...].astype(F32)
        o_a = term if o_a is None else o_a + term
    y = _dot(o_a.astype(BF16), w_ref[0:512, :]) + _dot(ob_ref[...], w_ref[512:1024, :])
    return x_ref[...] + y


def _swiglu_body(xn, wg_ref, wu_ref, wd_ref, h_ref):
    for c in range(D_FF // FF_CHUNK):
        cols = slice(c * FF_CHUNK, (c + 1) * FF_CHUNK)
        g = _dot(xn, wg_ref[:, cols])
        u = _dot(xn, wu_ref[:, cols])
        h_ref[:, cols] = (g * _sigmoid(g) * u).astype(BF16)
    return _dot(h_ref[...], wd_ref[...])


def _out_ffn_kernel(oc_ref, os_ref, ow_ref, ob_ref, gf_ref, gb_ref, gx_ref, x_ref, w_ref,
                    g_ref, wg_ref, wu_ref, wd_ref, o_ref, h_ref):
    x1 = _mix_out(oc_ref, os_ref, ow_ref, ob_ref, gf_ref, gb_ref, gx_ref, x_ref, w_ref)
    xn = _rms_rows(x1, g_ref[...]).astype(BF16)
    o_ref[...] = x1 + _swiglu_body(xn, wg_ref, wu_ref, wd_ref, h_ref)


def _out_ffn(oc, os_, ow, ob, gf, gate_bias_row, gate_expand, x2d, w_out_perm, norm_g, wg, wu, wd):
    t = x2d.shape[0]
    tm = ROW_TILE
    row = lambda i: (i, 0)
    c2 = lambda i: (0, 0)
    return pl.pallas_call(
        _out_ffn_kernel,
        grid=(t // tm,),
        in_specs=[
            pl.BlockSpec((tm, 512), row),
            pl.BlockSpec((tm, 512), row),
            pl.BlockSpec((tm, 512), row),
            pl.BlockSpec((tm, 512), row),
            pl.BlockSpec((tm, LANES), row),
            pl.BlockSpec((1, LANES), c2),
            pl.BlockSpec((3, LANES, 512), lambda i: (0, 0, 0)),
            pl.BlockSpec((tm, D_MODEL), row),
            pl.BlockSpec((D_MODEL, D_MODEL), c2),
            pl.BlockSpec((1, D_MODEL), c2),
            pl.BlockSpec((D_MODEL, D_FF), c2),
            pl.BlockSpec((D_MODEL, D_FF), c2),
            pl.BlockSpec((D_FF, D_MODEL), c2),
        ],
        out_specs=pl.BlockSpec((tm, D_MODEL), row),
        out_shape=jax.ShapeDtypeStruct((t, D_MODEL), F32),
        scratch_shapes=[pltpu.VMEM((tm, D_FF), BF16)],
        compiler_params=_cp(("parallel",)),
        name="out_a_ffn",
    )(oc, os_, ow, ob, gf, gate_bias_row, gate_expand, x2d, w_out_perm, norm_g, wg, wu, wd)


def _ffn_expert_kernel(te_ref, nu_ref, cnt_ref, src_now_ref, src_next_ref, dst_ref,
                       xn_hbm, wg_ref, wu_ref, wd_ref, y_hbm, xbuf, ybuf, h_ref, gsem, ssem):
    tm = EXPERT_TILE
    i = pl.program_id(0)
    n_used = nu_ref[0]
    slot = i & 1

    def gather(idx_ref, s):
        def body(r, _):
            pltpu.make_async_copy(xn_hbm.at[idx_ref[0, 0, r]], xbuf.at[s, r], gsem.at[s]).start()
            return 0
        lax.fori_loop(0, tm, body, 0, unroll=8)

    def wait_rows(copy_of_row, count):
        def body(r, _):
            copy_of_row(r).wait()
            return 0
        lax.fori_loop(0, count, body, 0)

    def scatter_copy(s, r, row):
        return pltpu.make_async_copy(ybuf.at[s, r], y_hbm.at[row], ssem.at[s])

    def wait_scatter(tile, s):
        @pl.when(cnt_ref[tile] == tm)
        def _():
            pltpu.make_async_copy(ybuf.at[s], y_hbm.at[pl.ds(0, tm)], ssem.at[s]).wait()

        @pl.when(cnt_ref[tile] < tm)
        def _():
            wait_rows(lambda r: scatter_copy(s, r, 0), cnt_ref[tile])

    @pl.when(i == 0)
    def _():
        gather(src_now_ref, 0)

    @pl.when(i + 1 < n_used)
    def _():
        gather(src_next_ref, 1 - slot)

    @pl.when(i < n_used)
    def _():
        pltpu.make_async_copy(xn_hbm.at[pl.ds(0, tm)], xbuf.at[slot], gsem.at[slot]).wait()

        @pl.when(i >= 2)
        def _():
            wait_scatter(i - 2, slot)

        x = xbuf[slot].reshape(tm, D_MODEL)
        y = _swiglu_body(x, wg_ref, wu_ref, wd_ref, h_ref)
        ybuf[slot] = y.astype(BF16).reshape(tm, D_MODEL // LANES, LANES)

        def send(r, _):
            scatter_copy(slot, r, dst_ref[0, 0, r]).start()
            return 0

        @pl.when(cnt_ref[i] == tm)
        def _():
            lax.fori_loop(0, tm, send, 0, unroll=8)

        @pl.when(cnt_ref[i] < tm)
        def _():
            lax.fori_loop(0, cnt_ref[i], send, 0)

    @pl.when(i == pl.num_programs(0) - 1)
    def _():
        @pl.when(n_used >= 2)
        def _():
            wait_scatter(n_used - 2, (n_used - 2) & 1)
        wait_scatter(n_used - 1, (n_used - 1) & 1)


def _ffn_experts(tile_expert, n_used, tile_count, src_tok, dst_row, xn3, wg, wu, wd):
    tm = EXPERT_TILE
    n_tiles = src_tok.shape[0]
    t = xn3.shape[0]
    wsel = lambda i, te, nu, cnt: (te[i], 0, 0)
    idx = lambda i, te, nu, cnt: (i, 0, 0)
    idx_next = lambda i, te, nu, cnt: (jnp.minimum(i + 1, n_tiles - 1), 0, 0)
    smem_idx = lambda m: pl.BlockSpec((1, 1, tm), m, memory_space=pltpu.SMEM)
    rows = D_MODEL // LANES
    grid_spec = pltpu.PrefetchScalarGridSpec(
        num_scalar_prefetch=3,
        grid=(n_tiles,),
        in_specs=[
            smem_idx(idx),
            smem_idx(idx_next),
            smem_idx(idx),
            pl.BlockSpec(memory_space=pl.ANY),
            pl.BlockSpec((None, D_MODEL, D_FF), wsel),
            pl.BlockSpec((None, D_MODEL, D_FF), wsel),
            pl.BlockSpec((None, D_FF, D_MODEL), wsel),
        ],
        out_specs=pl.BlockSpec(memory_space=pl.ANY),
        scratch_shapes=[
            pltpu.VMEM((2, tm, rows, LANES), BF16),
            pltpu.VMEM((2, tm, rows, LANES), BF16),
            pltpu.VMEM((tm, D_FF), BF16),
            pltpu.SemaphoreType.DMA((2,)),
            pltpu.SemaphoreType.DMA((2,)),
        ],
    )
    return pl.pallas_call(
        _ffn_expert_kernel,
        grid_spec=grid_spec,
        out_shape=jax.ShapeDtypeStruct((TOP_K * t, rows, LANES), BF16),
        compiler_params=_cp(("arbitrary",), has_side_effects=True),
        name="ffn_experts",
    )(tile_expert, n_used, tile_count, src_tok, src_tok, dst_row, xn3, wg, wu, wd)


def _proj_c_kernel(x_ref, g_ref, w_ref, wa_ref, ba_ref, tri_ref, qk_ref, v_ref, r_ref, la_ref):
    xn = _rms_rows(x_ref[...], g_ref[...]).astype(BF16)
    for c in range(2):
        qk_ref[:, 512 * c:512 * (c + 1)] = _dot(xn, w_ref[:, 512 * c:512 * (c + 1)]).astype(BF16)
    for c in range(2):
        v_ref[:, 512 * c:512 * (c + 1)] = _dot(xn, w_ref[:, 1024 + 512 * c:1536 + 512 * c]).astype(BF16)
    for c in range(2):
        r_ref[:, 512 * c:512 * (c + 1)] = _dot(xn, w_ref[:, 2048 + 512 * c:2560 + 512 * c]).astype(BF16)
    a1 = _dot(xn, w_ref[:, 3072:3200])
    hi, lo = _split2(a1)
    wa = wa_ref[...]
    pre = _dot(hi, wa) + _dot(lo, wa) + ba_ref[...]
    la = _log_sigmoid(pre) * (1.0 / GLA_TAU)
    tri = tri_ref[...]
    c_hi, c_mid, c_lo = _split3(la)
    la_ref[...] = _dot(tri, c_hi) + _dot(tri, c_mid) + _dot(tri, c_lo)


def _proj_c(x2d, norm_g, w_pad, wa_pad, ba_row):
    t = x2d.shape[0]
    tm = ROW_TILE
    idx = np.arange(tm)
    tri = ((idx[:, None] // GLA_CHUNK == idx[None, :] // GLA_CHUNK) & (idx[:, None] >= idx[None, :])).astype(np.float32)
    row = lambda i: (i, 0)
    c2 = lambda i: (0, 0)
    return pl.pallas_call(
        _proj_c_kernel,
        grid=(t // tm,),
        in_specs=[
            pl.BlockSpec((tm, D_MODEL), row),
            pl.BlockSpec((1, D_MODEL), c2),
            pl.BlockSpec((D_MODEL, 3200), c2),
            pl.BlockSpec((LANES, GLA_DK), c2),
            pl.BlockSpec((1, GLA_DK), c2),
            pl.BlockSpec((tm, tm), c2),
        ],
        out_specs=[
            pl.BlockSpec((tm, 1024), row),
            pl.BlockSpec((tm, 1024), row),
            pl.BlockSpec((tm, 1024), row),
            pl.BlockSpec((tm, GLA_DK), row),
        ],
        out_shape=[
            jax.ShapeDtypeStruct((t, 1024), BF16),
            jax.ShapeDtypeStruct((t, 1024), BF16),
            jax.ShapeDtypeStruct((t, 1024), BF16),
            jax.ShapeDtypeStruct((t, GLA_DK), F32),
        ],
        compiler_params=_cp(("parallel",)),
        name="proj_c",
    )(x2d, norm_g, w_pad, wa_pad, ba_row, jnp.asarray(tri, BF16))


GLA_STEP_HEADS = 2
LOG2E = 1.4426950408889634


def _gla_kernel(q_ref, k_ref, v_ref, b_ref, eye_ref, o_ref, st_ref):
    c_len, sub = GLA_CHUNK, GLA_SUB
    n_sub = c_len // sub
    hs = range(GLA_STEP_HEADS)
    dk = q_ref.shape[1] // GLA_STEP_HEADS
    dv = v_ref.shape[1] // GLA_STEP_HEADS
    scale = dk ** -0.5
    eye = eye_ref[...]
    st_ref[...] = jnp.zeros_like(st_ref)
    row = lax.broadcasted_iota(jnp.int32, (c_len, c_len), 0)
    col = lax.broadcasted_iota(jnp.int32, (c_len, c_len), 1)
    col_in_sub = col & (sub - 1)
    same_sub_causal = ((row >> 4) == (col >> 4)) & ((row & (sub - 1)) >= col_in_sub)
    earlier_sub = (col >> 4) < (row >> 4)

    def chunk(c, _):
        rows = pl.ds(pl.multiple_of(c * c_len, c_len), c_len)
        q = [q_ref[rows, h * dk:(h + 1) * dk].astype(F32) * scale for h in hs]
        k = [k_ref[rows, h * dk:(h + 1) * dk].astype(F32) for h in hs]
        vb = [v_ref[rows, h * dv:(h + 1) * dv] for h in hs]
        b = [b_ref[rows, h * dk:(h + 1) * dk] * LOG2E for h in hs]
        b_last = [b[h][c_len - 1:c_len, :] for h in hs]
        st_t = [st_ref[h] for h in hs]

        v_t = [_dot_nt(eye, vb[h]).astype(BF16) for h in hs]
        inter = [_dot_nt((q[h] * jnp.exp2(b[h])).astype(BF16), st_t[h].astype(BF16)) for h in hs]
        att_rows = [[jnp.zeros((sub, c_len), F32)] for _ in hs]
        for i in range(1, n_sub):
            for h in hs:
                r = b[h][i * sub:i * sub + 1, :]
                qi = (q[h][i * sub:(i + 1) * sub] * jnp.exp2(b[h][i * sub:(i + 1) * sub] - r)).astype(BF16)
                kj = (k[h] * jnp.exp2(jnp.minimum(r - b[h], 0.0))).astype(BF16)
                att_rows[h].append(_dot_nt(qi, kj))
        for h in hs:
            kd = (k[h] * jnp.exp2(b_last[h] - b[h])).astype(BF16)
            st_ref[h] = st_t[h] * jnp.exp2(b_last[h]) + _dot(v_t[h], kd)

        att_d = [jnp.zeros((c_len, c_len), F32) for _ in hs]
        for j in range(sub):
            def rows_j(x):
                return jnp.concatenate(
                    [jnp.broadcast_to(x[s * sub + j:s * sub + j + 1, :], (sub, x.shape[1]))
                     for s in range(n_sub)], axis=0)
            for h in hs:
                e = jnp.exp2(jnp.minimum(b[h] - rows_j(b[h]), 0.0))
                a = jnp.sum(q[h] * e * rows_j(k[h]), axis=-1, keepdims=True)
                att_d[h] = jnp.where(col_in_sub == j, a, att_d[h])
        for h in hs:
            att = jnp.where(earlier_sub, jnp.concatenate(att_rows[h], axis=0),
                            jnp.where(same_sub_causal, att_d[h], 0.0)).astype(BF16)
            o_ref[rows, h * dv:(h + 1) * dv] = (inter[h] + _dot(att, vb[h])).astype(BF16)
        return 0

    def two_chunks(cc, carry):
        chunk(2 * cc, carry)
        return chunk(2 * cc + 1, carry)

    lax.fori_loop(0, q_ref.shape[0] // (2 * c_len), two_chunks, 0)


def _gla(qk3, v3, b3):
    b, s, _ = qk3.shape
    n = GLA_STEP_HEADS
    dk = GLA_DK // GLA_HEADS
    dv = GLA_DV // GLA_HEADS
    steps = GLA_HEADS // n
    eye = jnp.asarray(np.eye(dv, dtype=np.float32), BF16)
    return pl.pallas_call(
        _gla_kernel,
        grid=(b, steps),
        in_specs=[
            pl.BlockSpec((None, s, n * dk), lambda i, h: (i, 0, h)),
            pl.BlockSpec((None, s, n * dk), lambda i, h: (i, 0, steps + h)),
            pl.BlockSpec((None, s, n * dv), lambda i, h: (i, 0, h)),
            pl.BlockSpec((None, s, n * dk), lambda i, h: (i, 0, h)),
            pl.BlockSpec((dv, dv), lambda i, h: (0, 0)),
        ],
        out_specs=pl.BlockSpec((None, s, n * dv), lambda i, h: (i, 0, h)),
        out_shape=jax.ShapeDtypeStruct((b, s, GLA_DV), BF16),
        scratch_shapes=[pltpu.VMEM((n, dv, dk), F32)],
        compiler_params=_cp(("parallel", "parallel")),
        name="gla",
    )(qk3, qk3, v3, b3, eye)


def _out_c_kernel(o_ref, r_ref, x_ref, on_ref, bd_ref, w_ref, mg_ref, rt_ref,
                  x3_ref, xn_ref, e1_ref, e2_ref, w1_ref, w2_ref):
    dv = GLA_DV // GLA_HEADS
    bd = bd_ref[...]
    parts = []
    for h in range(GLA_HEADS):
        cols = slice(h * dv, (h + 1) * dv)
        oh = _seg_norm(o_ref[:, cols].astype(F32), on_ref[...], bd)
        r = r_ref[:, cols].astype(F32)
        parts.append((oh * (r * _sigmoid(r))).astype(BF16))
    x3 = x_ref[...] + _dot(jnp.concatenate(parts, axis=1), w_ref[...])
    x3_ref[...] = x3

    xn = _rms_rows(x3, mg_ref[...])
    xn_ref[...] = xn.astype(BF16).reshape(xn_ref.shape)
    xh, xl = _split2(xn)
    logits = _dot(xh, rt_ref[0]) + _dot(xl, rt_ref[0]) + _dot(xh, rt_ref[1])
    lane = lax.broadcasted_iota(jnp.int32, logits.shape, 1)
    logits = jnp.where(lane < N_EXPERTS, logits, -3e38)
    v1 = jnp.max(logits, axis=-1, keepdims=True)
    i1 = jnp.min(jnp.where(logits == v1, lane, LANES), axis=-1, keepdims=True)
    rest = jnp.where(lane == i1, -3e38, logits)
    v2 = jnp.max(rest, axis=-1, keepdims=True)
    i2 = jnp.min(jnp.where(rest == v2, lane, LANES), axis=-1, keepdims=True)
    ex = jnp.exp(v2 - v1)
    den = 1.0 + ex
    e1_ref[...] = jnp.broadcast_to(i1, logits.shape)
    e2_ref[...] = jnp.broadcast_to(i2, logits.shape)
    w1_ref[...] = jnp.broadcast_to(1.0 / den, logits.shape)
    w2_ref[...] = jnp.broadcast_to(ex / den, logits.shape)


def _out_c(o2d, r2d, x2d, onorm_row, w_out, mnorm_row, router2):
    t = x2d.shape[0]
    tm = ROW_TILE
    dv = GLA_DV // GLA_HEADS
    bd = jnp.asarray(_block_diag_mean(dv, dv), BF16)
    row = lambda i: (i, 0)
    c2 = lambda i: (0, 0)
    slab = pl.BlockSpec((tm, LANES), row)
    return pl.pallas_call(
        _out_c_kernel,
        grid=(t // tm,),
        in_specs=[
            pl.BlockSpec((tm, GLA_DV), row),
            pl.BlockSpec((tm, GLA_DV), row),
            pl.BlockSpec((tm, D_MODEL), row),
            pl.BlockSpec((1, dv), c2),
            pl.BlockSpec((dv, dv), c2),
            pl.BlockSpec((D_MODEL, D_MODEL), c2),
            pl.BlockSpec((1, D_MODEL), c2),
            pl.BlockSpec((2, D_MODEL, LANES), lambda i: (0, 0, 0)),
        ],
        out_specs=[
            pl.BlockSpec((tm, D_MODEL), row),
            pl.BlockSpec((tm, D_MODEL // LANES, LANES), lambda i: (i, 0, 0)),
            slab, slab, slab, slab,
        ],
        out_shape=[
            jax.ShapeDtypeStruct((t, D_MODEL), F32),
            jax.ShapeDtypeStruct((t, D_MODEL // LANES, LANES), BF16),
            jax.ShapeDtypeStruct((t, LANES), jnp.int32),
            jax.ShapeDtypeStruct((t, LANES), jnp.int32),
            jax.ShapeDtypeStruct((t, LANES), F32),
            jax.ShapeDtypeStruct((t, LANES), F32),
        ],
        compiler_params=_cp(("parallel",)),
        name="out_c_router",
    )(o2d, r2d, x2d, onorm_row, bd, w_out, mnorm_row, router2)


def _combine_kernel(ya_ref, yb_ref, x_ref, w1_ref, w2_ref, o_ref):
    tm = x_ref.shape[0]
    reps = D_MODEL // LANES
    w1 = jnp.concatenate([w1_ref[...]] * reps, axis=1)
    w2 = jnp.concatenate([w2_ref[...]] * reps, axis=1)
    ya = ya_ref[...].reshape(tm, D_MODEL).astype(F32)
    yb = yb_ref[...].reshape(tm, D_MODEL).astype(F32)
    o_ref[...] = x_ref[...] + w1 * ya + w2 * yb


def _moe_combine(ys3, x2d, w1b, w2b):
    t = x2d.shape[0]
    tm = ROW_TILE
    row = lambda i: (i, 0)
    rows = D_MODEL // LANES
    second = t // tm
    return pl.pallas_call(
        _combine_kernel,
        grid=(t // tm,),
        in_specs=[
            pl.BlockSpec((tm, rows, LANES), lambda i: (i, 0, 0)),
            pl.BlockSpec((tm, rows, LANES), lambda i: (second + i, 0, 0)),
            pl.BlockSpec((tm, D_MODEL), row),
            pl.BlockSpec((tm, LANES), row),
            pl.BlockSpec((tm, LANES), row),
        ],
        out_specs=pl.BlockSpec((tm, D_MODEL), row),
        out_shape=jax.ShapeDtypeStruct((t, D_MODEL), F32),
        compiler_params=_cp(("parallel",)),
        name="moe_combine",
    )(ys3, ys3, x2d, w1b, w2b)


def _nsa_head_perm():
    order = []
    for t in range(4):
        order += [t, 4 + t]
    cols = np.concatenate([np.arange(h * HEAD_DIM, (h + 1) * HEAD_DIM) for h in order])
    return order, cols


def _layer_even(x, a_norm, a_w_in, a_q_norm, a_k_norm, a_pe_k, a_pe_v, ck1, ck2, cv1, cv2,
                gate_bias, fq_norm, fk_norm, f_bias, a_w_out, f_norm, f_wg, f_wu, f_wd):
    b, s, d = x.shape
    t = b * s
    x2d = x.reshape(t, d)
    order, qcols = _nsa_head_perm()

    def head_tiles(cols0):
        wq = a_w_in[:, cols0:cols0 + 512].reshape(d, FOX_HEADS, HEAD_DIM)
        return jnp.pad(wq, ((0, 0), (0, 0), (0, LANES - HEAD_DIM))).reshape(d, FOX_HEADS * LANES)

    w = jnp.concatenate([
        jnp.take(a_w_in, jnp.asarray(qcols), axis=1),
        a_w_in[:, 512:1280],
        head_tiles(1304),
        head_tiles(1816),
        a_w_in[:, 2328:2840],
        a_w_in[:, 1280:1304],
        a_w_in[:, 2840:2848],
        jnp.zeros((d, LANES - 32), F32),
    ], axis=1).astype(BF16)

    tile2 = lambda g: jnp.concatenate([g, g])
    lo_only = lambda g: jnp.concatenate([g, jnp.zeros((LANES - HEAD_DIM,), F32)])
    q_bias_lanes = jnp.zeros((LANES,), F32).at[FOX_BIAS_LANE:FOX_BIAS_LANE + 3].set(-1.0)
    head_gains = jnp.stack([tile2(a_q_norm), tile2(a_k_norm), lo_only(fq_norm), lo_only(fk_norm), q_bias_lanes]
                           + [jnp.zeros((LANES,), F32)] * 3)
    cos, sin = _rope_tables(np.arange(s))
    qa, kv6, fox, gf = _proj_a(x2d, a_norm.reshape(1, d), w, head_gains, cos, sin, s)

    fox3 = fox.reshape(b, s, FOX_SLAB)
    bias_row = jnp.zeros((1, LANES), F32).at[0, 24:32].set(f_bias)
    cum, k_aug = _fox_cumlog(gf.reshape(b, s, LANES), bias_row, fox3)
    cum_row = cum[:, :, 24:32].transpose(0, 2, 1)
    o_b = _fox_attention_t(fox3, k_aug, _values_t(fox3[:, :, 2048:2560]), cum_row).transpose(0, 2, 1)

    ncp = s // CMP_STRIDE
    n_cmp = (s - CMP_LEN) // CMP_STRIDE + 1
    n_slc = s // SLC_BLOCK
    kv3 = kv6.reshape(b, s, 768)
    kx = kv3[:, :, 0:128].reshape(b, ncp, CMP_STRIDE * LANES)
    vx = kv3[:, :, 128:256].reshape(b, ncp, CMP_STRIDE * LANES)

    def pe_rows(pe):
        p2 = jnp.concatenate([pe, pe], axis=1)
        return p2.reshape(2, CMP_STRIDE * LANES)

    def w1_blocks(w1):
        w4 = w1.reshape(2, CMP_STRIDE, HEAD_DIM, HEAD_DIM)
        z = jnp.zeros_like(w4)
        top = jnp.concatenate([w4, z], axis=-1)
        bot = jnp.concatenate([z, w4], axis=-1)
        return jnp.stack([top, bot], axis=2).reshape(2, CMP_STRIDE * LANES, LANES).astype(BF16)

    def w2_block(w2):
        z = jnp.zeros_like(w2)
        return jnp.concatenate([jnp.concatenate([w2, z], 1), jnp.concatenate([z, w2], 1)], 0).astype(BF16)

    ccos, csin = _rope_tables(np.arange(ncp) * CMP_STRIDE + CMP_LEN - 1)
    kcmp, vcmp = _nsa_compress(kx, vx, pe_rows(a_pe_k), pe_rows(a_pe_v), w1_blocks(ck1), w2_block(ck2),
                               w1_blocks(cv1), w2_block(cv2), tile2(a_k_norm).reshape(1, LANES), ccos, csin)

    c_start = np.arange(ncp) * CMP_STRIDE
    s_start = np.arange(64) * SLC_BLOCK
    ov = np.maximum(np.minimum(c_start[:, None] + CMP_LEN, s_start[None, :] + SLC_BLOCK)
                    - np.maximum(c_start[:, None], s_start[None, :]), 0).astype(np.float32) / CMP_LEN
    ov[n_cmp:, :] = 0.0
    ov[:, n_slc:] = 0.0
    qa3 = qa.reshape(b, s, 512)
    o_c, sb, o_w = _nsa_cmp_select(qa3, kcmp, vcmp, jnp.asarray(ov.T, BF16), kv3[:, :, 512:640],
                                   _values_t(kv3[:, :, 640:768]), n_cmp, n_slc)

    onehot = jnp.asarray((np.arange(s)[:, None] // SLC_BLOCK) == np.arange(64)[None, :], BF16)
    onehot = jnp.broadcast_to(onehot[None], (b, s, 64))
    ksl = kv3[:, :, 256:384]
    k_aug0 = jnp.concatenate([ksl[:, :, :64], onehot], axis=-1)
    k_aug1 = jnp.concatenate([onehot, ksl[:, :, 64:]], axis=-1)
    o_s = _nsa_selected_t(qa3, sb, k_aug0, k_aug1, _values_t(kv3[:, :, 384:512]))

    gexp = np.zeros((3, LANES, 512), np.float32)
    for tile_i in range(4):
        for half, h in enumerate((tile_i, 4 + tile_i)):
            for j in range(3):
                gexp[j, h * 3 + j, tile_i * LANES + half * HEAD_DIM:tile_i * LANES + (half + 1) * HEAD_DIM] = 1.0
    gate_bias_row = jnp.zeros((1, LANES), F32).at[0, 0:24].set(gate_bias)
    w_out = jnp.concatenate([jnp.take(a_w_out[:512], jnp.asarray(qcols), axis=0), a_w_out[512:]], 0).astype(BF16)
    x2 = _out_ffn(o_c.reshape(t, 512), o_s.reshape(t, 512), o_w.reshape(t, 512), o_b.reshape(t, 512),
                  gf, gate_bias_row, jnp.asarray(gexp, BF16), x2d, w_out,
                  f_norm.reshape(1, d), f_wg.astype(BF16), f_wu.astype(BF16), f_wd.astype(BF16))
    return x2.reshape(b, s, d)


def _layer_odd(x, c_norm, c_w_in, c_w_a2, c_b_a, c_o_norm, c_w_out, m_norm, m_router, m_wg, m_wu, m_wd):
    b, s, d = x.shape
    t = b * s
    x2d = x.reshape(t, d)
    w = jnp.pad(c_w_in, ((0, 0), (0, 3200 - c_w_in.shape[1]))).astype(BF16)
    wa = jnp.pad(c_w_a2, ((0, LANES - GLA_RANK), (0, 0))).astype(BF16)
    qk, v, r, la = _proj_c(x2d, c_norm.reshape(1, d), w, wa, c_b_a.reshape(1, GLA_DK))
    o = _gla(qk.reshape(b, s, 1024), v.reshape(b, s, 1024), la.reshape(b, s, GLA_DK))

    rt = jnp.pad(m_router, ((0, 0), (0, LANES - N_EXPERTS)))
    rt_hi = rt.astype(BF16)
    rt_lo = (rt - rt_hi.astype(F32)).astype(BF16)
    x3, xn, e1, e2, w1b, w2b = _out_c(o.reshape(t, GLA_DV), r, x2d, c_o_norm.reshape(1, -1),
                                      c_w_out.astype(BF16), m_norm.reshape(1, d), jnp.stack([rt_hi, rt_lo]))

    tm = EXPERT_TILE
    n_slot = TOP_K * t + N_EXPERTS * tm
    n_tiles = n_slot // tm
    flat_e = jnp.concatenate([e1[:, 0], e2[:, 0]])
    onehot = (flat_e[:, None] == jnp.arange(N_EXPERTS)[None, :]).astype(jnp.int32)
    counts = jnp.sum(onehot, axis=0)
    padded = ((counts + tm - 1) // tm) * tm
    ends = jnp.cumsum(padded)
    offs = ends - padded
    n_used = (ends[-1] // tm).astype(jnp.int32)
    tile_start = jnp.arange(n_tiles, dtype=jnp.int32) * tm
    tile_e = jnp.sum((tile_start[:, None] >= ends[None, :]).astype(jnp.int32), axis=1)
    last_e = jnp.sum((((n_used - 1) * tm) >= ends).astype(jnp.int32))
    tile_e = jnp.where(jnp.arange(n_tiles) < n_used, tile_e, last_e).astype(jnp.int32)
    order = jnp.argsort(flat_e, stable=True).astype(jnp.int32)
    order = jnp.concatenate([order, jnp.zeros((n_slot,), jnp.int32)])
    starts = jnp.cumsum(counts) - counts
    row_of = jnp.zeros((n_slot + TOP_K * t,), jnp.int32)
    for e in range(N_EXPERTS):
        piece = lax.dynamic_slice(order, (starts[e],), (TOP_K * t,))
        row_of = lax.dynamic_update_slice(row_of, piece, (offs[e],))
    row_of = row_of[:n_slot]
    src_tok = (row_of % t).reshape(n_tiles, 1, tm)
    dst_row = row_of.reshape(n_tiles, 1, tm)
    tile_off = tile_start - jnp.sum(jnp.where(tile_start[:, None] >= ends[None, :], padded[None, :], 0), axis=1)
    tile_count = jnp.clip(counts[jnp.minimum(tile_e, N_EXPERTS - 1)] - tile_off, 0, tm).astype(jnp.int32)
    tile_count = jnp.where(jnp.arange(n_tiles) < n_used, tile_count, 0)

    ys = _ffn_experts(tile_e, n_used.reshape(1), tile_count, src_tok, dst_row, xn,
                      m_wg.astype(BF16), m_wu.astype(BF16), m_wd.astype(BF16))
    out = _moe_combine(ys, x3, w1b, w2b)
    return out.reshape(b, s, d)


def kernel(x, a_norm, a_w_in, a_q_norm, a_k_norm, a_pe_k, a_pe_v, a_cmp_k_w1, a_cmp_k_w2, a_cmp_v_w1, a_cmp_v_w2, a_gate_bias, a_fox_q_norm, a_fox_k_norm, a_fox_f_bias, a_w_out, f_norm, f_w_gate, f_w_up, f_w_down, c_norm, c_w_in, c_w_a2, c_b_a, c_o_norm, c_w_out, m_norm, m_router, m_w_gate, m_w_up, m_w_down):
    x = _layer_even(x, a_norm[0], a_w_in[0], a_q_norm[0], a_k_norm[0], a_pe_k[0], a_pe_v[0],
                    a_cmp_k_w1[0], a_cmp_k_w2[0], a_cmp_v_w1[0], a_cmp_v_w2[0], a_gate_bias[0],
                    a_fox_q_norm[0], a_fox_k_norm[0], a_fox_f_bias[0], a_w_out[0],
                    f_norm[0], f_w_gate[0], f_w_up[0], f_w_down[0])
    x = _layer_odd(x, c_norm[0], c_w_in[0], c_w_a2[0], c_b_a[0], c_o_norm[0], c_w_out[0],
                   m_norm[0], m_router[0], m_w_gate[0], m_w_up[0], m_w_down[0])
    return x
```

```python
import functools

import numpy as np
import jax
import jax.numpy as jnp
from jax import lax
from jax.experimental import pallas as pl
from jax.experimental.pallas import tpu as pltpu

F32 = jnp.float32
BF16 = jnp.bfloat16

D_MODEL = 1024
HEAD_DIM = 64
ROPE_DIM = 16
ROPE_THETA = 500000.0
NORM_EPS = 1e-6
NEG = -1e30
FORCE_SCORE = 1e6

NSA_HEADS = 8
CMP_LEN = 32
CMP_STRIDE = 16
SLC_BLOCK = 64
SLC_TOPK = 16
WINDOW = 512
FOX_HEADS = 8

GLA_HEADS = 4
GLA_DK = 512
GLA_DV = 1024
GLA_RANK = 16
GLA_TAU = 16.0
GLA_CHUNK = 64
GLA_SUB = 16

D_FF = 2816
N_EXPERTS = 8
TOP_K = 2

LANES = 128
FF_CHUNK = 256
VMEM_LIMIT = 56 * 1024 * 1024

ROW_TILE = 512
WIDE_ROW_TILE = 1024
ATT_Q = 128
FOX_Q = 512
FOX_K = 256
SLC_K = 256
EXPERT_TILE = 512

NT_DIMS = (((1,), (1,)), ((), ()))


def _cp(sem, **kw):
    return pltpu.CompilerParams(dimension_semantics=sem, vmem_limit_bytes=VMEM_LIMIT, **kw)


def _dot(a, b):
    return jnp.dot(a, b, preferred_element_type=F32)


def _dot_nt(a, b):
    return lax.dot_general(a, b, NT_DIMS, preferred_element_type=F32)


def _split2(x):
    hi = x.astype(BF16)
    lo = (x - hi.astype(F32)).astype(BF16)
    return hi, lo


def _split3(x):
    hi = x.astype(BF16)
    r = x - hi.astype(F32)
    mid = r.astype(BF16)
    lo = (r - mid.astype(F32)).astype(BF16)
    return hi, mid, lo


def _rms_rows(x, gain):
    return x * lax.rsqrt(jnp.mean(x * x, axis=-1, keepdims=True) + NORM_EPS) * gain


def _log_sigmoid(x):
    return jnp.minimum(x, 0.0) - jnp.log1p(jnp.exp(-jnp.abs(x)))


def _sigmoid(x):
    return 1.0 / (1.0 + jnp.exp(-x))


def _seg_norm(y, gain, bd):
    ms = _dot((y * y).astype(BF16), bd)
    return y * lax.rsqrt(ms + NORM_EPS) * gain


def _rope(y, cos, sin, perm):
    return y * cos + _dot(y.astype(BF16), perm) * sin


def _block_diag_mean(width, seg):
    i = np.arange(width)
    return ((i[:, None] // seg) == (i[None, :] // seg)).astype(np.float32) / seg


def _rope_perm():
    p = np.zeros((LANES, LANES), np.float32)
    half = ROPE_DIM // 2
    for j in range(LANES):
        d = j % HEAD_DIM
        if d < half:
            p[j + half, j] = -1.0
        elif d < ROPE_DIM:
            p[j - half, j] = 1.0
    return p


def _rope_tables(pos):
    inv = ROPE_THETA ** (-np.arange(0, ROPE_DIM, 2, dtype=np.float64) / ROPE_DIM)
    ang = pos.astype(np.float64)[:, None] * inv[None, :]
    cos8, sin8 = np.cos(ang), np.sin(ang)
    n = pos.shape[0]
    ones = np.ones((n, HEAD_DIM - ROPE_DIM))
    zeros = np.zeros((n, HEAD_DIM - ROPE_DIM))
    c = np.concatenate([cos8, cos8, ones], axis=1)
    s = np.concatenate([sin8, sin8, zeros], axis=1)
    return (jnp.asarray(np.concatenate([c, c], axis=1), F32),
            jnp.asarray(np.concatenate([s, s], axis=1), F32))


PROJ_A_COLS = 3968
FOX_SLAB = 2560


def _proj_a_kernel(x_ref, g_ref, w_ref, hg_ref, cos_ref, sin_ref, bd_ref, bdlo_ref, perm_ref,
                   qa_ref, kv_ref, fox_ref, gf_ref):
    xn = _rms_rows(x_ref[...], g_ref[...]).astype(BF16)
    bd = bd_ref[...]
    perm = perm_ref[...]
    cos = cos_ref[...]
    sin = sin_ref[...]

    def tile(acc, i):
        return acc[:, i * LANES:(i + 1) * LANES]

    acc = _dot(xn, w_ref[:, 0:512])
    for i in range(4):
        y = _rope(_seg_norm(tile(acc, i), hg_ref[0:1, :], bd), cos, sin, perm)
        qa_ref[:, i * LANES:(i + 1) * LANES] = (y * 0.125).astype(BF16)

    acc = _dot(xn, w_ref[:, 512:1280])
    for i in range(6):
        y = tile(acc, i)
        if i in (2, 4):
            y = _rope(_seg_norm(y, hg_ref[1:2, :], bd), cos, sin, perm)
        kv_ref[:, i * LANES:(i + 1) * LANES] = y.astype(BF16)

    bd_lo = bdlo_ref[...]
    for c in range(4):
        acc = _dot(xn, w_ref[:, 1280 + 512 * c:1280 + 512 * (c + 1)])
        for i in range(4):
            y = tile(acc, i)
            if c < 2:
                y = _seg_norm(y, hg_ref[2:3, :], bd_lo) * 0.125 + hg_ref[4:5, :]
            else:
                y = _seg_norm(y, hg_ref[3:4, :], bd_lo)
            fox_ref[:, 512 * c + i * LANES:512 * c + (i + 1) * LANES] = y.astype(BF16)
    fox_ref[:, 2048:2560] = _dot(xn, w_ref[:, 3328:3840]).astype(BF16)

    gf_ref[...] = _dot(xn, w_ref[:, 3840:3968])


def _proj_a(x2d, norm_g, w_perm, head_gains, cos, sin, seq):
    t = x2d.shape[0]
    tm = WIDE_ROW_TILE
    n_pos = seq // tm
    bd = jnp.asarray(_block_diag_mean(LANES, HEAD_DIM), BF16)
    lo = np.zeros((LANES, LANES), np.float32)
    lo[:HEAD_DIM, :] = 1.0 / HEAD_DIM
    perm = jnp.asarray(_rope_perm(), BF16)
    full = lambda i: (0, 0)
    return pl.pallas_call(
        _proj_a_kernel,
        grid=(t // tm,),
        in_specs=[
            pl.BlockSpec((tm, D_MODEL), lambda i: (i, 0)),
            pl.BlockSpec((1, D_MODEL), full),
            pl.BlockSpec((D_MODEL, PROJ_A_COLS), full),
            pl.BlockSpec((8, LANES), full),
            pl.BlockSpec((tm, LANES), lambda i: (i % n_pos, 0)),
            pl.BlockSpec((tm, LANES), lambda i: (i % n_pos, 0)),
            pl.BlockSpec((LANES, LANES), full),
            pl.BlockSpec((LANES, LANES), full),
            pl.BlockSpec((LANES, LANES), full),
        ],
        out_specs=[
            pl.BlockSpec((tm, 512), lambda i: (i, 0)),
            pl.BlockSpec((tm, 768), lambda i: (i, 0)),
            pl.BlockSpec((tm, FOX_SLAB), lambda i: (i, 0)),
            pl.BlockSpec((tm, LANES), lambda i: (i, 0)),
        ],
        out_shape=[
            jax.ShapeDtypeStruct((t, 512), BF16),
            jax.ShapeDtypeStruct((t, 768), BF16),
            jax.ShapeDtypeStruct((t, FOX_SLAB), BF16),
            jax.ShapeDtypeStruct((t, LANES), F32),
        ],
        compiler_params=_cp(("parallel",)),
        name="proj_a",
    )(x2d, norm_g, w_perm, head_gains, cos, sin, bd, jnp.asarray(lo, BF16), perm)


CUM_CHUNK = 256
FOX_BIAS_LANE = 64


def _cum_kernel(gf_ref, bias_ref, tri_ref, k_ref, place_ref, out_ref, ka_ref):
    tri = tri_ref[...]
    n_chunk = gf_ref.shape[0] // CUM_CHUNK
    carry = jnp.zeros((1, LANES), F32)
    for c in range(n_chunk):
        rows = slice(c * CUM_CHUNK, (c + 1) * CUM_CHUNK)
        lf = _log_sigmoid(gf_ref[rows, :] + bias_ref[...])
        hi, mid, lo = _split3(lf)
        cs = _dot(tri, hi) + _dot(tri, mid) + _dot(tri, lo) + carry
        out_ref[rows, :] = cs
        carry = cs[CUM_CHUNK - 1:CUM_CHUNK, :]
        c_hi, c_mid, c_lo = _split3(cs)
        placed = _dot(c_hi, place_ref[0]) + _dot(c_mid, place_ref[1]) + _dot(c_lo, place_ref[2])
        ka_ref[rows, :] = (k_ref[rows, :].astype(F32) + placed).astype(BF16)


def _fox_cumlog(gf3, bias_row, fox3):
    b, s, _ = gf3.shape
    tri = jnp.asarray(np.tril(np.ones((CUM_CHUNK, CUM_CHUNK), np.float32)), BF16)
    place = np.zeros((3, LANES, FOX_HEADS * LANES), np.float32)
    for h in range(FOX_HEADS):
        for piece in range(3):
            place[piece, 24 + h, h * LANES + FOX_BIAS_LANE + piece] = 1.0
    return pl.pallas_call(
        _cum_kernel,
        grid=(b,),
        in_specs=[
            pl.BlockSpec((None, s, LANES), lambda i: (i, 0, 0)),
            pl.BlockSpec((1, LANES), lambda i: (0, 0)),
            pl.BlockSpec((CUM_CHUNK, CUM_CHUNK), lambda i: (0, 0)),
            pl.BlockSpec((None, s, FOX_HEADS * LANES), lambda i: (i, 0, 1)),
            pl.BlockSpec((3, LANES, FOX_HEADS * LANES), lambda i: (0, 0, 0)),
        ],
        out_specs=[
            pl.BlockSpec((None, s, LANES), lambda i: (i, 0, 0)),
            pl.BlockSpec((None, s, FOX_HEADS * LANES), lambda i: (i, 0, 0)),
        ],
        out_shape=[
            jax.ShapeDtypeStruct((b, s, LANES), F32),
            jax.ShapeDtypeStruct((b, s, FOX_HEADS * LANES), BF16),
        ],
        compiler_params=_cp(("parallel",)),
        name="fox_cumlog",
    )(gf3, bias_row, tri, fox3, jnp.asarray(place, BF16))


def _gelu_tanh(x):
    return 0.5 * x * (1.0 + jnp.tanh(0.7978845608028654 * (x + 0.044715 * x * x * x)))


def _cmp_kernel(kx_ref, vx_ref, pek_ref, pev_ref, w1k_ref, w2k_ref, w1v_ref, w2v_ref,
                kg_ref, cos_ref, sin_ref, bd_ref, perm_ref, kc_ref, vc_ref):
    ncp = kx_ref.shape[0]

    def compress(x_ref, pe_ref, w1_ref, w2_ref):
        x = x_ref[...].astype(F32)
        first = _dot((x + pe_ref[0:1, :]).astype(BF16), w1_ref[0])
        second = _dot((x + pe_ref[1:2, :]).astype(BF16), w1_ref[1])
        pre = first + pltpu.roll(second, ncp - 1, 0)
        return _dot(_gelu_tanh(pre).astype(BF16), w2_ref[...])

    k = compress(kx_ref, pek_ref, w1k_ref, w2k_ref)
    k = _rope(_seg_norm(k, kg_ref[...], bd_ref[...]), cos_ref[...], sin_ref[...], perm_ref[...])
    kc_ref[...] = k.astype(BF16)
    vc_ref[...] = compress(vx_ref, pev_ref, w1v_ref, w2v_ref).astype(BF16)


def _nsa_compress(kx, vx, pek, pev, w1k, w2k, w1v, w2v, kgain, ccos, csin):
    b, ncp, width = kx.shape
    bd = jnp.asarray(_block_diag_mean(LANES, HEAD_DIM), BF16)
    perm = jnp.asarray(_rope_perm(), BF16)
    c2 = lambda i: (0, 0)
    c3 = lambda i: (0, 0, 0)
    return pl.pallas_call(
        _cmp_kernel,
        grid=(b,),
        in_specs=[
            pl.BlockSpec((None, ncp, width), lambda i: (i, 0, 0)),
            pl.BlockSpec((None, ncp, width), lambda i: (i, 0, 0)),
            pl.BlockSpec((2, width), c2),
            pl.BlockSpec((2, width), c2),
            pl.BlockSpec((2, width, LANES), c3),
            pl.BlockSpec((LANES, LANES), c2),
            pl.BlockSpec((2, width, LANES), c3),
            pl.BlockSpec((LANES, LANES), c2),
            pl.BlockSpec((1, LANES), c2),
            pl.BlockSpec((ncp, LANES), c2),
            pl.BlockSpec((ncp, LANES), c2),
            pl.BlockSpec((LANES, LANES), c2),
            pl.BlockSpec((LANES, LANES), c2),
        ],
        out_specs=[
            pl.BlockSpec((None, ncp, LANES), lambda i: (i, 0, 0)),
            pl.BlockSpec((None, ncp, LANES), lambda i: (i, 0, 0)),
        ],
        out_shape=[
            jax.ShapeDtypeStruct((b, ncp, LANES), BF16),
            jax.ShapeDtypeStruct((b, ncp, LANES), BF16),
        ],
        compiler_params=_cp(("parallel",)),
        name="nsa_compress",
    )(kx, vx, pek, pev, w1k, w2k, w1v, w2v, kgain, ccos, csin, bd, perm)


def _nsa_cmp_kernel(q_ref, kc_ref, vc_ref, ovt_ref, eye_ref, kw_ref, vw_ref, oc_ref, sb_ref, ow_ref, *, n_cmp, n_slc):
    tq = ATT_Q
    _window_attention(q_ref, kw_ref, vw_ref, eye_ref, ow_ref)
    q0 = pl.program_id(1) * tq
    kc = kc_ref[...]
    vc = vc_ref[...]
    ncp = kc.shape[0]
    ovt = ovt_ref[...]
    lo_half = lax.broadcasted_iota(jnp.int32, (tq, LANES), 1) < HEAD_DIM

    n_lane = lax.broadcasted_iota(jnp.int32, (tq, ncp), 1)
    t_sub = q0 + lax.broadcasted_iota(jnp.int32, (tq, ncp), 0)
    mask = (n_lane * CMP_STRIDE + (CMP_LEN - 1) <= t_sub) & (n_lane < n_cmp)
    n_sub = lax.broadcasted_iota(jnp.int32, (ncp, tq), 0)
    t_lane = q0 + lax.broadcasted_iota(jnp.int32, (ncp, tq), 1)
    mask_t = (n_sub * CMP_STRIDE + (CMP_LEN - 1) <= t_lane) & (n_sub < n_cmp)

    imp_t = [jnp.zeros((64, tq), F32), jnp.zeros((64, tq), F32)]
    for t in range(4):
        qt = q_ref[:, t * LANES:(t + 1) * LANES]
        outs = []
        for g in range(2):
            qm = jnp.where(lo_half if g == 0 else jnp.logical_not(lo_half), qt, jnp.zeros_like(qt))
            s = jnp.where(mask, _dot_nt(qm, kc), NEG)
            p = jnp.where(mask, jnp.exp(s - jnp.max(s, axis=-1, keepdims=True)), 0.0)
            p = p / jnp.maximum(jnp.sum(p, axis=-1, keepdims=True), 1e-30)
            outs.append(_dot(p.astype(BF16), vc))
            st = jnp.where(mask_t, _dot_nt(kc, qm), NEG)
            pt = jnp.where(mask_t, jnp.exp(st - jnp.max(st, axis=0, keepdims=True)), 0.0)
            pt = pt / jnp.maximum(jnp.sum(pt, axis=0, keepdims=True), 1e-30)
            hi, lo = _split2(pt)
            imp_t[g] = imp_t[g] + _dot(ovt, hi) + _dot(ovt, lo)
        oc_ref[:, t * LANES:(t + 1) * LANES] = jnp.where(lo_half, outs[0], outs[1]).astype(BF16)

    blk = lax.broadcasted_iota(jnp.int32, (64, tq), 0)
    cur = (q0 + lax.broadcasted_iota(jnp.int32, (64, tq), 1)) >> 6
    forced = (blk == 0) | (blk == cur) | (blk == cur - 1)
    future = blk > cur
    exists = blk < n_slc
    biases = []
    for g in range(2):
        v = jnp.where(forced, FORCE_SCORE, jnp.where(future, -FORCE_SCORE, imp_t[g]))
        v = jnp.where(exists, v, -3e38)
        sel = jnp.zeros((64, tq), jnp.int32)
        for _ in range(min(SLC_TOPK, n_slc)):
            m = jnp.max(v, axis=0, keepdims=True)
            first = jnp.min(jnp.where(v == m, blk, 64), axis=0, keepdims=True)
            hit = blk == first
            sel = jnp.where(hit, 1, sel)
            v = jnp.where(hit, -3.2e38, v)
        keep = (sel == 1) & jnp.logical_not(future) & exists
        biases.append(jnp.where(keep, 0.0, NEG).astype(BF16))
    sb_t = jnp.concatenate([biases[1], biases[0]], axis=0)
    sb_ref[...] = _dot_nt(eye_ref[...], sb_t).astype(BF16)


def _nsa_cmp_select(qa3, kcmp, vcmp, ovt, k_win, v_win_t, n_cmp, n_slc):
    b, s, _ = qa3.shape
    ncp = kcmp.shape[1]
    eye = jnp.asarray(np.eye(ATT_Q, dtype=np.float32), BF16)
    kern = functools.partial(_nsa_cmp_kernel, n_cmp=n_cmp, n_slc=n_slc)
    return pl.pallas_call(
        kern,
        grid=(b, s // ATT_Q),
        in_specs=[
            pl.BlockSpec((None, ATT_Q, 512), lambda i, j: (i, j, 0)),
            pl.BlockSpec((None, ncp, LANES), lambda i, j: (i, 0, 0)),
            pl.BlockSpec((None, ncp, LANES), lambda i, j: (i, 0, 0)),
            pl.BlockSpec((64, ncp), lambda i, j: (0, 0)),
            pl.BlockSpec((ATT_Q, ATT_Q), lambda i, j: (0, 0)),
            pl.BlockSpec((None, s, LANES), lambda i, j: (i, 0, 0)),
            pl.BlockSpec((None, 2, V_ROWS, s), lambda i, j: (i, 0, 0, 0)),
        ],
        out_specs=[
            pl.BlockSpec((None, ATT_Q, 512), lambda i, j: (i, j, 0)),
            pl.BlockSpec((None, ATT_Q, LANES), lambda i, j: (i, j, 0)),
            pl.BlockSpec((None, ATT_Q, 512), lambda i, j: (i, j, 0)),
        ],
        out_shape=[
            jax.ShapeDtypeStruct((b, s, 512), BF16),
            jax.ShapeDtypeStruct((b, s, LANES), BF16),
            jax.ShapeDtypeStruct((b, s, 512), BF16),
        ],
        compiler_params=_cp(("parallel", "parallel")),
        name="nsa_cmp_win_select",
    )(qa3, kcmp, vcmp, ovt, eye, k_win, v_win_t)


V_ROWS = 80


def _flash_step_t(carry, s_t, v_t, *, query_bias=None, mask=None):
    m, acc = carry
    if mask is not None:
        s_t = jnp.where(mask, s_t, NEG)
    col_max = jnp.max(s_t, axis=0, keepdims=True)
    if query_bias is not None:
        m_new = jnp.maximum(m, col_max + query_bias)
        shift = m_new - query_bias
    else:
        m_new = jnp.maximum(m, col_max)
        shift = m_new
    p = jnp.exp(s_t - shift).astype(BF16)
    acc = jnp.exp(m - m_new) * acc + _dot(v_t, p)
    return m_new, acc


def _flash_init_t(queries):
    return (jnp.full((1, queries), NEG, F32), jnp.zeros((V_ROWS, queries), F32))


def _values_t(v):
    b, s, w = v.shape
    heads = w // HEAD_DIM
    vt = v.reshape(b, s, heads, HEAD_DIM).transpose(0, 2, 3, 1)
    ones = jnp.ones((b, heads, 1, s), v.dtype)
    pad = jnp.zeros((b, heads, V_ROWS - HEAD_DIM - 1, s), v.dtype)
    return jnp.concatenate([vt, ones, pad], axis=2)


def _nsa_slc_t_kernel(q_ref, sb_ref, k0_ref, k1_ref, v_ref, eye_ref, o_ref, sa_ref, sc_ref):
    tq, tk = ATT_Q, SLC_K
    q0 = pl.program_id(1) * tq
    sb = sb_ref[...]
    lo_half = lax.broadcasted_iota(jnp.int32, (tq, LANES), 1) < HEAD_DIM
    cols = 4 * tq
    n_before = q0 // tk
    k_refs = (k0_ref, k1_ref)
    slots = (sa_ref, sc_ref)

    qq = []
    for g in range(2):
        parts = []
        for t in range(4):
            qt = q_ref[:, t * LANES:(t + 1) * LANES]
            parts.append(jnp.where(lo_half, qt, sb) if g == 0 else jnp.where(lo_half, sb, qt))
        qq.append(jnp.concatenate(parts, axis=0))

    def issue(j, slot):
        off = pl.multiple_of(j * tk, tk)
        for g in range(2):
            slots[slot][g] = _dot_nt(k_refs[g][pl.ds(off, tk), :], qq[g])

    def consume(j, carry, slot, mask=None):
        off = pl.multiple_of(j * tk, tk)
        return tuple(_flash_step_t(carry[g], slots[slot][g], v_ref[g, :, pl.ds(off, tk)], mask=mask)
                     for g in range(2))

    def step(jj, carry):
        j = 2 * jj
        issue(j + 1, 1)
        carry = consume(j, carry, 0)
        issue(j + 2, 0)
        return consume(j + 1, carry, 1)

    issue(0, 0)
    carry = lax.fori_loop(0, n_before // 2, step, (_flash_init_t(cols), _flash_init_t(cols)))

    q_in_chunk = q0 - n_before * tk
    query = lax.broadcasted_iota(jnp.int32, (tk, cols), 1) & (tq - 1)
    causal = lax.broadcasted_iota(jnp.int32, (tk, cols), 0) - query <= q_in_chunk

    def even_tail(carry):
        return consume(n_before, carry, 0, causal)

    def odd_tail(carry):
        issue(n_before, 1)
        carry = consume(n_before - 1, carry, 0)
        return consume(n_before, carry, 1, causal)

    carry = lax.cond((n_before & 1) == 0, even_tail, odd_tail, carry)

    outs = [(acc[0:HEAD_DIM] / acc[HEAD_DIM:HEAD_DIM + 1]).astype(BF16) for (_, acc) in carry]
    eye = eye_ref[...]
    for t in range(4):
        cs = slice(t * tq, (t + 1) * tq)
        both = jnp.concatenate([outs[0][:, cs], outs[1][:, cs]], axis=0)
        o_ref[:, t * LANES:(t + 1) * LANES] = _dot_nt(eye, both).astype(BF16)


def _nsa_selected_t(qa3, sb, k_aug0, k_aug1, v_t):
    b, s, _ = qa3.shape
    blk = lambda i, j: (i, j, 0)
    whole = lambda i, j: (i, 0, 0)
    eye = jnp.asarray(np.eye(ATT_Q, dtype=np.float32), BF16)
    return pl.pallas_call(
        _nsa_slc_t_kernel,
        grid=(b, s // ATT_Q),
        in_specs=[
            pl.BlockSpec((None, ATT_Q, 512), blk),
            pl.BlockSpec((None, ATT_Q, LANES), blk),
            pl.BlockSpec((None, s, LANES), whole),
            pl.BlockSpec((None, s, LANES), whole),
            pl.BlockSpec((None, 2, V_ROWS, s), lambda i, j: (i, 0, 0, 0)),
            pl.BlockSpec((ATT_Q, ATT_Q), lambda i, j: (0, 0)),
        ],
        out_specs=pl.BlockSpec((None, ATT_Q, 512), blk),
        out_shape=jax.ShapeDtypeStruct((b, s, 512), BF16),
        scratch_shapes=[pltpu.VMEM((2, SLC_K, 4 * ATT_Q), F32), pltpu.VMEM((2, SLC_K, 4 * ATT_Q), F32)],
        compiler_params=_cp(("parallel", "parallel")),
        name="nsa_selected",
    )(qa3, sb, k_aug0, k_aug1, v_t, eye)


def _window_attention(q_ref, k_ref, v_ref, eye_ref, o_ref):
    tq = ATT_Q
    span = WINDOW + tq
    q0 = pl.program_id(1) * tq
    start = pl.multiple_of(jnp.maximum(q0 - WINDOW, 0), tq)
    k = k_ref[pl.ds(start, span), :]
    cols = 4 * tq
    lo_half = lax.broadcasted_iota(jnp.int32, (tq, LANES), 1) < HEAD_DIM
    query = lax.broadcasted_iota(jnp.int32, (span, cols), 1) & (tq - 1)
    dist = (q0 - start) + query - lax.broadcasted_iota(jnp.int32, (span, cols), 0)
    mask = (dist >= 0) & (dist < WINDOW)
    scores = []
    for g in range(2):
        keep = lo_half if g == 0 else jnp.logical_not(lo_half)
        parts = []
        for t in range(4):
            qt = q_ref[:, t * LANES:(t + 1) * LANES]
            parts.append(jnp.where(keep, qt, jnp.zeros_like(qt)))
        scores.append(_dot_nt(k, jnp.concatenate(parts, axis=0)))
    outs = []
    for g in range(2):
        s_t = jnp.where(mask, scores[g], NEG)
        p = jnp.exp(s_t - jnp.max(s_t, axis=0, keepdims=True)).astype(BF16)
        acc = _dot(v_ref[g, :, pl.ds(start, span)], p)
        outs.append((acc[0:HEAD_DIM] / acc[HEAD_DIM:HEAD_DIM + 1]).astype(BF16))
    eye = eye_ref[...]
    for t in range(4):
        cs = slice(t * tq, (t + 1) * tq)
        both = jnp.concatenate([outs[0][:, cs], outs[1][:, cs]], axis=0)
        o_ref[:, t * LANES:(t + 1) * LANES] = _dot_nt(eye, both).astype(BF16)


def _fox_t_kernel(q_ref, k_ref, v_ref, cr_ref, o_ref, sa_ref, sb_ref):
    tq, tk = FOX_Q, FOX_K
    q0 = pl.program_id(1) * tq
    n_before = q0 // tk
    n_diag = tq // tk
    for pair in range(FOX_HEADS // 2):
        heads = (2 * pair, 2 * pair + 1)
        qs = tuple(q_ref[:, h * LANES:(h + 1) * LANES] for h in heads)
        cq = tuple(cr_ref[h:h + 1, pl.ds(pl.multiple_of(q0, tq), tq)] for h in heads)

        def scores(j, i, q_lo=0):
            off = pl.multiple_of(j * tk, tk)
            h = heads[i]
            return _dot_nt(k_ref[pl.ds(off, tk), h * LANES:(h + 1) * LANES], qs[i][q_lo:])

        slots = (sa_ref, sb_ref)

        def consume(j, carry, slot):
            off = pl.multiple_of(j * tk, tk)
            return tuple(_flash_step_t(carry[i], slots[slot][i], v_ref[heads[i], :, pl.ds(off, tk)],
                                       query_bias=cq[i]) for i in range(2))

        def step(jj, carry):
            j = 2 * jj
            for i in range(2):
                sb_ref[i] = scores(j + 1, i)
            carry = consume(j, carry, 0)
            for i in range(2):
                sa_ref[i] = scores(j + 2, i)
            return consume(j + 1, carry, 1)

        for i in range(2):
            sa_ref[i] = scores(0, i)
        carry = lax.fori_loop(0, n_before // 2, step, (_flash_init_t(tq), _flash_init_t(tq)))

        late = [[scores(n_before + d, i, d * tk) for i in range(2)] for d in range(1, n_diag)]
        for d in range(n_diag):
            q_lo = d * tk
            off = pl.multiple_of((n_before + d) * tk, tk)
            shape = (tk, tq - q_lo)
            mask = (lax.broadcasted_iota(jnp.int32, shape, 0) + (d * tk - q_lo)
                    <= lax.broadcasted_iota(jnp.int32, shape, 1))
            out = []
            for i in range(2):
                m, acc = carry[i]
                s_t = sa_ref[i] if d == 0 else late[d - 1][i]
                new = _flash_step_t((m[:, q_lo:], acc[:, q_lo:]), s_t, v_ref[heads[i], :, pl.ds(off, tk)],
                                    query_bias=cq[i][:, q_lo:], mask=mask)
                if q_lo:
                    new = tuple(jnp.concatenate([old[:, :q_lo], part], axis=1)
                                for old, part in zip((m, acc), new))
                out.append(new)
            carry = tuple(out)
        for i, h in enumerate(heads):
            acc = carry[i][1]
            o_ref[h * HEAD_DIM:(h + 1) * HEAD_DIM, :] = (
                acc[0:HEAD_DIM] / acc[HEAD_DIM:HEAD_DIM + 1]).astype(BF16)


def _fox_attention_t(fox3, k_aug, v_t, cum_row):
    b, s, _ = fox3.shape
    width = FOX_HEADS * LANES
    return pl.pallas_call(
        _fox_t_kernel,
        grid=(b, s // FOX_Q),
        in_specs=[
            pl.BlockSpec((None, FOX_Q, width), lambda i, j: (i, j, 0)),
            pl.BlockSpec((None, s, width), lambda i, j: (i, 0, 0)),
            pl.BlockSpec((None, FOX_HEADS, V_ROWS, s), lambda i, j: (i, 0, 0, 0)),
            pl.BlockSpec((None, FOX_HEADS, s), lambda i, j: (i, 0, 0)),
        ],
        out_specs=pl.BlockSpec((None, FOX_HEADS * HEAD_DIM, FOX_Q), lambda i, j: (i, 0, j)),
        out_shape=jax.ShapeDtypeStruct((b, FOX_HEADS * HEAD_DIM, s), BF16),
        scratch_shapes=[pltpu.VMEM((2, FOX_K, FOX_Q), F32), pltpu.VMEM((2, FOX_K, FOX_Q), F32)],
        compiler_params=_cp(("parallel", "parallel")),
        name="fox_attention",
    )(fox3, k_aug, v_t, cum_row)


def _mix_out(oc_ref, os_ref, ow_ref, ob_ref, gf_ref, gb_ref, gx_ref, x_ref, w_ref):
    gates = _sigmoid(gf_ref[...] + gb_ref[...])
    hi, lo = _split2(gates)
    o_a = None
    for j, br in enumerate((oc_ref, os_ref, ow_ref)):
        gexp = _dot(hi, gx_ref[j]) + _dot(lo, gx_ref[j])
        term = gexp * br[...].astype(F32)
        o_a = term if o_a is None else o_a + term
    y = _dot(o_a.astype(BF16), w_ref[0:512, :]) + _dot(ob_ref[...], w_ref[512:1024, :])
    return x_ref[...] + y


def _swiglu_body(xn, wg_ref, wu_ref, wd_ref, h_ref):
    for c in range(D_FF // FF_CHUNK):
        cols = slice(c * FF_CHUNK, (c + 1) * FF_CHUNK)
        g = _dot(xn, wg_ref[:, cols])
        u = _dot(xn, wu_ref[:, cols])
        h_ref[:, cols] = (g * _sigmoid(g) * u).astype(BF16)
    return _dot(h_ref[...], wd_ref[...])


def _out_ffn_kernel(oc_ref, os_ref, ow_ref, ob_ref, gf_ref, gb_ref, gx_ref, x_ref, w_ref,
                    g_ref, wg_ref, wu_ref, wd_ref, o_ref, h_ref):
    x1 = _mix_out(oc_ref, os_ref, ow_ref, ob_ref, gf_ref, gb_ref, gx_ref, x_ref, w_ref)
    xn = _rms_rows(x1, g_ref[...]).astype(BF16)
    o_ref[...] = x1 + _swiglu_body(xn, wg_ref, wu_ref, wd_ref, h_ref)


def _out_ffn(oc, os_, ow, ob, gf, gate_bias_row, gate_expand, x2d, w_out_perm, norm_g, wg, wu, wd):
    t = x2d.shape[0]
    tm = ROW_TILE
    row = lambda i: (i, 0)
    c2 = lambda i: (0, 0)
    return pl.pallas_call(
        _out_ffn_kernel,
        grid=(t // tm,),
        in_specs=[
            pl.BlockSpec((tm, 512), row),
            pl.BlockSpec((tm, 512), row),
            pl.BlockSpec((tm, 512), row),
            pl.BlockSpec((tm, 512), row),
            pl.BlockSpec((tm, LANES), row),
            pl.BlockSpec((1, LANES), c2),
            pl.BlockSpec((3, LANES, 512), lambda i: (0, 0, 0)),
            pl.BlockSpec((tm, D_MODEL), row),
            pl.BlockSpec((D_MODEL, D_MODEL), c2),
            pl.BlockSpec((1, D_MODEL), c2),
            pl.BlockSpec((D_MODEL, D_FF), c2),
            pl.BlockSpec((D_MODEL, D_FF), c2),
            pl.BlockSpec((D_FF, D_MODEL), c2),
        ],
        out_specs=pl.BlockSpec((tm, D_MODEL), row),
        out_shape=jax.ShapeDtypeStruct((t, D_MODEL), F32),
        scratch_shapes=[pltpu.VMEM((tm, D_FF), BF16)],
        compiler_params=_cp(("parallel",)),
        name="out_a_ffn",
    )(oc, os_, ow, ob, gf, gate_bias_row, gate_expand, x2d, w_out_perm, norm_g, wg, wu, wd)


def _ffn_expert_kernel(te_ref, nu_ref, cnt_ref, src_now_ref, src_next_ref, dst_ref,
                       xn_hbm, wg_ref, wu_ref, wd_ref, y_hbm, xbuf, ybuf, h_ref, gsem, ssem):
    tm = EXPERT_TILE
    i = pl.program_id(0)
    n_used = nu_ref[0]
    slot = i & 1

    def gather(idx_ref, s):
        def body(r, _):
            pltpu.make_async_copy(xn_hbm.at[idx_ref[0, 0, r]], xbuf.at[s, r], gsem.at[s]).start()
            return 0
        lax.fori_loop(0, tm, body, 0, unroll=8)

    def wait_rows(copy_of_row, count):
        def body(r, _):
            copy_of_row(r).wait()
            return 0
        lax.fori_loop(0, count, body, 0)

    def scatter_copy(s, r, row):
        return pltpu.make_async_copy(ybuf.at[s, r], y_hbm.at[row], ssem.at[s])

    def wait_scatter(tile, s):
        @pl.when(cnt_ref[tile] == tm)
        def _():
            pltpu.make_async_copy(ybuf.at[s], y_hbm.at[pl.ds(0, tm)], ssem.at[s]).wait()

        @pl.when(cnt_ref[tile] < tm)
        def _():
            wait_rows(lambda r: scatter_copy(s, r, 0), cnt_ref[tile])

    @pl.when(i == 0)
    def _():
        gather(src_now_ref, 0)

    @pl.when(i + 1 < n_used)
    def _():
        gather(src_next_ref, 1 - slot)

    @pl.when(i < n_used)
    def _():
        pltpu.make_async_copy(xn_hbm.at[pl.ds(0, tm)], xbuf.at[slot], gsem.at[slot]).wait()

        @pl.when(i >= 2)
        def _():
            wait_scatter(i - 2, slot)

        x = xbuf[slot].reshape(tm, D_MODEL)
        y = _swiglu_body(x, wg_ref, wu_ref, wd_ref, h_ref)
        ybuf[slot] = y.astype(BF16).reshape(tm, D_MODEL // LANES, LANES)

        def send(r, _):
            scatter_copy(slot, r, dst_ref[0, 0, r]).start()
            return 0

        @pl.when(cnt_ref[i] == tm)
        def _():
            lax.fori_loop(0, tm, send, 0, unroll=8)

        @pl.when(cnt_ref[i] < tm)
        def _():
            lax.fori_loop(0, cnt_ref[i], send, 0)

    @pl.when(i == pl.num_programs(0) - 1)
    def _():
        @pl.when(n_used >= 2)
        def _():
            wait_scatter(n_used - 2, (n_used - 2) & 1)
        wait_scatter(n_used - 1, (n_used - 1) & 1)


def _ffn_experts(tile_expert, n_used, tile_count, src_tok, dst_row, xn3, wg, wu, wd):
    tm = EXPERT_TILE
    n_tiles = src_tok.shape[0]
    t = xn3.shape[0]
    wsel = lambda i, te, nu, cnt: (te[i], 0, 0)
    idx = lambda i, te, nu, cnt: (i, 0, 0)
    idx_next = lambda i, te, nu, cnt: (jnp.minimum(i + 1, n_tiles - 1), 0, 0)
    smem_idx = lambda m: pl.BlockSpec((1, 1, tm), m, memory_space=pltpu.SMEM)
    rows = D_MODEL // LANES
    grid_spec = pltpu.PrefetchScalarGridSpec(
        num_scalar_prefetch=3,
        grid=(n_tiles,),
        in_specs=[
            smem_idx(idx),
            smem_idx(idx_next),
            smem_idx(idx),
            pl.BlockSpec(memory_space=pl.ANY),
            pl.BlockSpec((None, D_MODEL, D_FF), wsel),
            pl.BlockSpec((None, D_MODEL, D_FF), wsel),
            pl.BlockSpec((None, D_FF, D_MODEL), wsel),
        ],
        out_specs=pl.BlockSpec(memory_space=pl.ANY),
        scratch_shapes=[
            pltpu.VMEM((2, tm, rows, LANES), BF16),
            pltpu.VMEM((2, tm, rows, LANES), BF16),
            pltpu.VMEM((tm, D_FF), BF16),
            pltpu.SemaphoreType.DMA((2,)),
            pltpu.SemaphoreType.DMA((2,)),
        ],
    )
    return pl.pallas_call(
        _ffn_expert_kernel,
        grid_spec=grid_spec,
        out_shape=jax.ShapeDtypeStruct((TOP_K * t, rows, LANES), BF16),
        compiler_params=_cp(("arbitrary",), has_side_effects=True),
        name="ffn_experts",
    )(tile_expert, n_used, tile_count, src_tok, src_tok, dst_row, xn3, wg, wu, wd)


def _proj_c_kernel(x_ref, g_ref, w_ref, wa_ref, ba_ref, tri_ref, qk_ref, v_ref, r_ref, la_ref):
    xn = _rms_rows(x_ref[...], g_ref[...]).astype(BF16)
    for c in range(2):
        qk_ref[:, 512 * c:512 * (c + 1)] = _dot(xn, w_ref[:, 512 * c:512 * (c + 1)]).astype(BF16)
    for c in range(2):
        v_ref[:, 512 * c:512 * (c + 1)] = _dot(xn, w_ref[:, 1024 + 512 * c:1536 + 512 * c]).astype(BF16)
    for c in range(2):
        r_ref[:, 512 * c:512 * (c + 1)] = _dot(xn, w_ref[:, 2048 + 512 * c:2560 + 512 * c]).astype(BF16)
    a1 = _dot(xn, w_ref[:, 3072:3200])
    hi, lo = _split2(a1)
    wa = wa_ref[...]
    pre = _dot(hi, wa) + _dot(lo, wa) + ba_ref[...]
    la = _log_sigmoid(pre) * (1.0 / GLA_TAU)
    tri = tri_ref[...]
    c_hi, c_mid, c_lo = _split3(la)
    la_ref[...] = _dot(tri, c_hi) + _dot(tri, c_mid) + _dot(tri, c_lo)


def _proj_c(x2d, norm_g, w_pad, wa_pad, ba_row):
    t = x2d.shape[0]
    tm = ROW_TILE
    idx = np.arange(tm)
    tri = ((idx[:, None] // GLA_CHUNK == idx[None, :] // GLA_CHUNK) & (idx[:, None] >= idx[None, :])).astype(np.float32)
    row = lambda i: (i, 0)
    c2 = lambda i: (0, 0)
    return pl.pallas_call(
        _proj_c_kernel,
        grid=(t // tm,),
        in_specs=[
            pl.BlockSpec((tm, D_MODEL), row),
            pl.BlockSpec((1, D_MODEL), c2),
            pl.BlockSpec((D_MODEL, 3200), c2),
            pl.BlockSpec((LANES, GLA_DK), c2),
            pl.BlockSpec((1, GLA_DK), c2),
            pl.BlockSpec((tm, tm), c2),
        ],
        out_specs=[
            pl.BlockSpec((tm, 1024), row),
            pl.BlockSpec((tm, 1024), row),
            pl.BlockSpec((tm, 1024), row),
            pl.BlockSpec((tm, GLA_DK), row),
        ],
        out_shape=[
            jax.ShapeDtypeStruct((t, 1024), BF16),
            jax.ShapeDtypeStruct((t, 1024), BF16),
            jax.ShapeDtypeStruct((t, 1024), BF16),
            jax.ShapeDtypeStruct((t, GLA_DK), F32),
        ],
        compiler_params=_cp(("parallel",)),
        name="proj_c",
    )(x2d, norm_g, w_pad, wa_pad, ba_row, jnp.asarray(tri, BF16))


GLA_STEP_HEADS = 2
LOG2E = 1.4426950408889634


def _gla_kernel(q_ref, k_ref, v_ref, b_ref, eye_ref, o_ref, st_ref):
    c_len, sub = GLA_CHUNK, GLA_SUB
    n_sub = c_len // sub
    hs = range(GLA_STEP_HEADS)
    dk = q_ref.shape[1] // GLA_STEP_HEADS
    dv = v_ref.shape[1] // GLA_STEP_HEADS
    scale = dk ** -0.5
    eye = eye_ref[...]
    st_ref[...] = jnp.zeros_like(st_ref)
    row = lax.broadcasted_iota(jnp.int32, (c_len, c_len), 0)
    col = lax.broadcasted_iota(jnp.int32, (c_len, c_len), 1)
    col_in_sub = col & (sub - 1)
    same_sub_causal = ((row >> 4) == (col >> 4)) & ((row & (sub - 1)) >= col_in_sub)
    earlier_sub = (col >> 4) < (row >> 4)

    def chunk(c, _):
        rows = pl.ds(pl.multiple_of(c * c_len, c_len), c_len)
        q = [q_ref[rows, h * dk:(h + 1) * dk].astype(F32) * scale for h in hs]
        k = [k_ref[rows, h * dk:(h + 1) * dk].astype(F32) for h in hs]
        vb = [v_ref[rows, h * dv:(h + 1) * dv] for h in hs]
        b = [b_ref[rows, h * dk:(h + 1) * dk] * LOG2E for h in hs]
        b_last = [b[h][c_len - 1:c_len, :] for h in hs]
        st_t = [st_ref[h] for h in hs]

        v_t = [_dot_nt(eye, vb[h]).astype(BF16) for h in hs]
        inter = [_dot_nt((q[h] * jnp.exp2(b[h])).astype(BF16), st_t[h].astype(BF16)) for h in hs]
        att_rows = [[jnp.zeros((sub, c_len), F32)] for _ in hs]
        for i in range(1, n_sub):
            for h in hs:
                r = b[h][i * sub:i * sub + 1, :]
                qi = (q[h][i * sub:(i + 1) * sub] * jnp.exp2(b[h][i * sub:(i + 1) * sub] - r)).astype(BF16)
                kj = (k[h] * jnp.exp2(jnp.minimum(r - b[h], 0.0))).astype(BF16)
                att_rows[h].append(_dot_nt(qi, kj))
        for h in hs:
            kd = (k[h] * jnp.exp2(b_last[h] - b[h])).astype(BF16)
            st_ref[h] = st_t[h] * jnp.exp2(b_last[h]) + _dot(v_t[h], kd)

        att_d = [jnp.zeros((c_len, c_len), F32) for _ in hs]
        for j in range(sub):
            def rows_j(x):
                return jnp.concatenate(
                    [jnp.broadcast_to(x[s * sub + j:s * sub + j + 1, :], (sub, x.shape[1]))
                     for s in range(n_sub)], axis=0)
            for h in hs:
                e = jnp.exp2(jnp.minimum(b[h] - rows_j(b[h]), 0.0))
                a = jnp.sum(q[h] * e * rows_j(k[h]), axis=-1, keepdims=True)
                att_d[h] = jnp.where(col_in_sub == j, a, att_d[h])
        for h in hs:
            att = jnp.where(earlier_sub, jnp.concatenate(att_rows[h], axis=0),
                            jnp.where(same_sub_causal, att_d[h], 0.0)).astype(BF16)
            o_ref[rows, h * dv:(h + 1) * dv] = (inter[h] + _dot(att, vb[h])).astype(BF16)
        return 0

    def two_chunks(cc, carry):
        chunk(2 * cc, carry)
        return chunk(2 * cc + 1, carry)

    lax.fori_loop(0, q_ref.shape[0] // (2 * c_len), two_chunks, 0)


def _gla(qk3, v3, b3):
    b, s, _ = qk3.shape
    n = GLA_STEP_HEADS
    dk = GLA_DK // GLA_HEADS
    dv = GLA_DV // GLA_HEADS
    steps = GLA_HEADS // n
    eye = jnp.asarray(np.eye(dv, dtype=np.float32), BF16)
    return pl.pallas_call(
        _gla_kernel,
        grid=(b, steps),
        in_specs=[
            pl.BlockSpec((None, s, n * dk), lambda i, h: (i, 0, h)),
            pl.BlockSpec((None, s, n * dk), lambda i, h: (i, 0, steps + h)),
            pl.BlockSpec((None, s, n * dv), lambda i, h: (i, 0, h)),
            pl.BlockSpec((None, s, n * dk), lambda i, h: (i, 0, h)),
            pl.BlockSpec((dv, dv), lambda i, h: (0, 0)),
        ],
        out_specs=pl.BlockSpec((None, s, n * dv), lambda i, h: (i, 0, h)),
        out_shape=jax.ShapeDtypeStruct((b, s, GLA_DV), BF16),
        scratch_shapes=[pltpu.VMEM((n, dv, dk), F32)],
        compiler_params=_cp(("parallel", "parallel")),
        name="gla",
    )(qk3, qk3, v3, b3, eye)


def _out_c_kernel(o_ref, r_ref, x_ref, on_ref, bd_ref, w_ref, mg_ref, rt_ref,
                  x3_ref, xn_ref, e1_ref, e2_ref, w1_ref, w2_ref):
    dv = GLA_DV // GLA_HEADS
    bd = bd_ref[...]
    parts = []
    for h in range(GLA_HEADS):
        cols = slice(h * dv, (h + 1) * dv)
        oh = _seg_norm(o_ref[:, cols].astype(F32), on_ref[...], bd)
        r = r_ref[:, cols].astype(F32)
        parts.append((oh * (r * _sigmoid(r))).astype(BF16))
    x3 = x_ref[...] + _dot(jnp.concatenate(parts, axis=1), w_ref[...])
    x3_ref[...] = x3

    xn = _rms_rows(x3, mg_ref[...])
    xn_ref[...] = xn.astype(BF16).reshape(xn_ref.shape)
    xh, xl = _split2(xn)
    logits = _dot(xh, rt_ref[0]) + _dot(xl, rt_ref[0]) + _dot(xh, rt_ref[1])
    lane = lax.broadcasted_iota(jnp.int32, logits.shape, 1)
    logits = jnp.where(lane < N_EXPERTS, logits, -3e38)
    v1 = jnp.max(logits, axis=-1, keepdims=True)
    i1 = jnp.min(jnp.where(logits == v1, lane, LANES), axis=-1, keepdims=True)
    rest = jnp.where(lane == i1, -3e38, logits)
    v2 = jnp.max(rest, axis=-1, keepdims=True)
    i2 = jnp.min(jnp.where(rest == v2, lane, LANES), axis=-1, keepdims=True)
    ex = jnp.exp(v2 - v1)
    den = 1.0 + ex
    e1_ref[...] = jnp.broadcast_to(i1, logits.shape)
    e2_ref[...] = jnp.broadcast_to(i2, logits.shape)
    w1_ref[...] = jnp.broadcast_to(1.0 / den, logits.shape)
    w2_ref[...] = jnp.broadcast_to(ex / den, logits.shape)


def _out_c(o2d, r2d, x2d, onorm_row, w_out, mnorm_row, router2):
    t = x2d.shape[0]
    tm = WIDE_ROW_TILE
    dv = GLA_DV // GLA_HEADS
    bd = jnp.asarray(_block_diag_mean(dv, dv), BF16)
    row = lambda i: (i, 0)
    c2 = lambda i: (0, 0)
    slab = pl.BlockSpec((tm, LANES), row)
    return pl.pallas_call(
        _out_c_kernel,
        grid=(t // tm,),
        in_specs=[
            pl.BlockSpec((tm, GLA_DV), row),
            pl.BlockSpec((tm, GLA_DV), row),
            pl.BlockSpec((tm, D_MODEL), row),
            pl.BlockSpec((1, dv), c2),
            pl.BlockSpec((dv, dv), c2),
            pl.BlockSpec((D_MODEL, D_MODEL), c2),
            pl.BlockSpec((1, D_MODEL), c2),
            pl.BlockSpec((2, D_MODEL, LANES), lambda i: (0, 0, 0)),
        ],
        out_specs=[
            pl.BlockSpec((tm, D_MODEL), row),
            pl.BlockSpec((tm, D_MODEL // LANES, LANES), lambda i: (i, 0, 0)),
            slab, slab, slab, slab,
        ],
        out_shape=[
            jax.ShapeDtypeStruct((t, D_MODEL), F32),
            jax.ShapeDtypeStruct((t, D_MODEL // LANES, LANES), BF16),
            jax.ShapeDtypeStruct((t, LANES), jnp.int32),
            jax.ShapeDtypeStruct((t, LANES), jnp.int32),
            jax.ShapeDtypeStruct((t, LANES), F32),
            jax.ShapeDtypeStruct((t, LANES), F32),
        ],
        compiler_params=_cp(("parallel",)),
        name="out_c_router",
    )(o2d, r2d, x2d, onorm_row, bd, w_out, mnorm_row, router2)


def _combine_kernel(ya_ref, yb_ref, x_ref, w1_ref, w2_ref, o_ref):
    tm = x_ref.shape[0]
    reps = D_MODEL // LANES
    w1 = jnp.concatenate([w1_ref[...]] * reps, axis=1)
    w2 = jnp.concatenate([w2_ref[...]] * reps, axis=1)
    ya = ya_ref[...].reshape(tm, D_MODEL).astype(F32)
    yb = yb_ref[...].reshape(tm, D_MODEL).astype(F32)
    o_ref[...] = x_ref[...] + w1 * ya + w2 * yb


def _moe_combine(ys3, x2d, w1b, w2b):
    t = x2d.shape[0]
    tm = ROW_TILE
    row = lambda i: (i, 0)
    rows = D_MODEL // LANES
    second = t // tm
    return pl.pallas_call(
        _combine_kernel,
        grid=(t // tm,),
        in_specs=[
            pl.BlockSpec((tm, rows, LANES), lambda i: (i, 0, 0)),
            pl.BlockSpec((tm, rows, LANES), lambda i: (second + i, 0, 0)),
            pl.BlockSpec((tm, D_MODEL), row),
            pl.BlockSpec((tm, LANES), row),
            pl.BlockSpec((tm, LANES), row),
        ],
        out_specs=pl.BlockSpec((tm, D_MODEL), row),
        out_shape=jax.ShapeDtypeStruct((t, D_MODEL), F32),
        compiler_params=_cp(("parallel",)),
        name="moe_combine",
    )(ys3, ys3, x2d, w1b, w2b)


def _nsa_head_perm():
    order = []
    for t in range(4):
        order += [t, 4 + t]
    cols = np.concatenate([np.arange(h * HEAD_DIM, (h + 1) * HEAD_DIM) for h in order])
    return order, cols


def _layer_even(x, a_norm, a_w_in, a_q_norm, a_k_norm, a_pe_k, a_pe_v, ck1, ck2, cv1, cv2,
                gate_bias, fq_norm, fk_norm, f_bias, a_w_out, f_norm, f_wg, f_wu, f_wd):
    b, s, d = x.shape
    t = b * s
    x2d = x.reshape(t, d)
    order, qcols = _nsa_head_perm()

    def head_tiles(cols0):
        wq = a_w_in[:, cols0:cols0 + 512].reshape(d, FOX_HEADS, HEAD_DIM)
        return jnp.pad(wq, ((0, 0), (0, 0), (0, LANES - HEAD_DIM))).reshape(d, FOX_HEADS * LANES)

    w = jnp.concatenate([
        jnp.take(a_w_in, jnp.asarray(qcols), axis=1),
        a_w_in[:, 512:1280],
        head_tiles(1304),
        head_tiles(1816),
        a_w_in[:, 2328:2840],
        a_w_in[:, 1280:1304],
        a_w_in[:, 2840:2848],
        jnp.zeros((d, LANES - 32), F32),
    ], axis=1).astype(BF16)

    tile2 = lambda g: jnp.concatenate([g, g])
    lo_only = lambda g: jnp.concatenate([g, jnp.zeros((LANES - HEAD_DIM,), F32)])
    q_bias_lanes = jnp.zeros((LANES,), F32).at[FOX_BIAS_LANE:FOX_BIAS_LANE + 3].set(-1.0)
    head_gains = jnp.stack([tile2(a_q_norm), tile2(a_k_norm), lo_only(fq_norm), lo_only(fk_norm), q_bias_lanes]
                           + [jnp.zeros((LANES,), F32)] * 3)
    cos, sin = _rope_tables(np.arange(s))
    qa, kv6, fox, gf = _proj_a(x2d, a_norm.reshape(1, d), w, head_gains, cos, sin, s)

    fox3 = fox.reshape(b, s, FOX_SLAB)
    bias_row = jnp.zeros((1, LANES), F32).at[0, 24:32].set(f_bias)
    cum, k_aug = _fox_cumlog(gf.reshape(b, s, LANES), bias_row, fox3)
    cum_row = cum[:, :, 24:32].transpose(0, 2, 1)
    o_b = _fox_attention_t(fox3, k_aug, _values_t(fox3[:, :, 2048:2560]), cum_row).transpose(0, 2, 1)

    ncp = s // CMP_STRIDE
    n_cmp = (s - CMP_LEN) // CMP_STRIDE + 1
    n_slc = s // SLC_BLOCK
    kv3 = kv6.reshape(b, s, 768)
    kx = kv3[:, :, 0:128].reshape(b, ncp, CMP_STRIDE * LANES)
    vx = kv3[:, :, 128:256].reshape(b, ncp, CMP_STRIDE * LANES)

    def pe_rows(pe):
        p2 = jnp.concatenate([pe, pe], axis=1)
        return p2.reshape(2, CMP_STRIDE * LANES)

    def w1_blocks(w1):
        w4 = w1.reshape(2, CMP_STRIDE, HEAD_DIM, HEAD_DIM)
        z = jnp.zeros_like(w4)
        top = jnp.concatenate([w4, z], axis=-1)
        bot = jnp.concatenate([z, w4], axis=-1)
        return jnp.stack([top, bot], axis=2).reshape(2, CMP_STRIDE * LANES, LANES).astype(BF16)

    def w2_block(w2):
        z = jnp.zeros_like(w2)
        return jnp.concatenate([jnp.concatenate([w2, z], 1), jnp.concatenate([z, w2], 1)], 0).astype(BF16)

    ccos, csin = _rope_tables(np.arange(ncp) * CMP_STRIDE + CMP_LEN - 1)
    kcmp, vcmp = _nsa_compress(kx, vx, pe_rows(a_pe_k), pe_rows(a_pe_v), w1_blocks(ck1), w2_block(ck2),
                               w1_blocks(cv1), w2_block(cv2), tile2(a_k_norm).reshape(1, LANES), ccos, csin)

    c_start = np.arange(ncp) * CMP_STRIDE
    s_start = np.arange(64) * SLC_BLOCK
    ov = np.maximum(np.minimum(c_start[:, None] + CMP_LEN, s_start[None, :] + SLC_BLOCK)
                    - np.maximum(c_start[:, None], s_start[None, :]), 0).astype(np.float32) / CMP_LEN
    ov[n_cmp:, :] = 0.0
    ov[:, n_slc:] = 0.0
    qa3 = qa.reshape(b, s, 512)
    o_c, sb, o_w = _nsa_cmp_select(qa3, kcmp, vcmp, jnp.asarray(ov.T, BF16), kv3[:, :, 512:640],
                                   _values_t(kv3[:, :, 640:768]), n_cmp, n_slc)

    onehot = jnp.asarray((np.arange(s)[:, None] // SLC_BLOCK) == np.arange(64)[None, :], BF16)
    onehot = jnp.broadcast_to(onehot[None], (b, s, 64))
    ksl = kv3[:, :, 256:384]
    k_aug0 = jnp.concatenate([ksl[:, :, :64], onehot], axis=-1)
    k_aug1 = jnp.concatenate([onehot, ksl[:, :, 64:]], axis=-1)
    o_s = _nsa_selected_t(qa3, sb, k_aug0, k_aug1, _values_t(kv3[:, :, 384:512]))

    gexp = np.zeros((3, LANES, 512), np.float32)
    for tile_i in range(4):
        for half, h in enumerate((tile_i, 4 + tile_i)):
            for j in range(3):
                gexp[j, h * 3 + j, tile_i * LANES + half * HEAD_DIM:tile_i * LANES + (half + 1) * HEAD_DIM] = 1.0
    gate_bias_row = jnp.zeros((1, LANES), F32).at[0, 0:24].set(gate_bias)
    w_out = jnp.concatenate([jnp.take(a_w_out[:512], jnp.asarray(qcols), axis=0), a_w_out[512:]], 0).astype(BF16)
    x2 = _out_ffn(o_c.reshape(t, 512), o_s.reshape(t, 512), o_w.reshape(t, 512), o_b.reshape(t, 512),
                  gf, gate_bias_row, jnp.asarray(gexp, BF16), x2d, w_out,
                  f_norm.reshape(1, d), f_wg.astype(BF16), f_wu.astype(BF16), f_wd.astype(BF16))
    return x2.reshape(b, s, d)


def _layer_odd(x, c_norm, c_w_in, c_w_a2, c_b_a, c_o_norm, c_w_out, m_norm, m_router, m_wg, m_wu, m_wd):
    b, s, d = x.shape
    t = b * s
    x2d = x.reshape(t, d)
    w = jnp.pad(c_w_in, ((0, 0), (0, 3200 - c_w_in.shape[1]))).astype(BF16)
    wa = jnp.pad(c_w_a2, ((0, LANES - GLA_RANK), (0, 0))).astype(BF16)
    qk, v, r, la = _proj_c(x2d, c_norm.reshape(1, d), w, wa, c_b_a.reshape(1, GLA_DK))
    o = _gla(qk.reshape(b, s, 1024), v.reshape(b, s, 1024), la.reshape(b, s, GLA_DK))

    rt = jnp.pad(m_router, ((0, 0), (0, LANES - N_EXPERTS)))
    rt_hi = rt.astype(BF16)
    rt_lo = (rt - rt_hi.astype(F32)).astype(BF16)
    x3, xn, e1, e2, w1b, w2b = _out_c(o.reshape(t, GLA_DV), r, x2d, c_o_norm.reshape(1, -1),
                                      c_w_out.astype(BF16), m_norm.reshape(1, d), jnp.stack([rt_hi, rt_lo]))

    tm = EXPERT_TILE
    n_slot = TOP_K * t + N_EXPERTS * tm
    n_tiles = n_slot // tm
    flat_e = jnp.concatenate([e1[:, 0], e2[:, 0]])
    onehot = (flat_e[:, None] == jnp.arange(N_EXPERTS)[None, :]).astype(jnp.int32)
    counts = jnp.sum(onehot, axis=0)
    padded = ((counts + tm - 1) // tm) * tm
    ends = jnp.cumsum(padded)
    offs = ends - padded
    n_used = (ends[-1] // tm).astype(jnp.int32)
    tile_start = jnp.arange(n_tiles, dtype=jnp.int32) * tm
    tile_e = jnp.sum((tile_start[:, None] >= ends[None, :]).astype(jnp.int32), axis=1)
    last_e = jnp.sum((((n_used - 1) * tm) >= ends).astype(jnp.int32))
    tile_e = jnp.where(jnp.arange(n_tiles) < n_used, tile_e, last_e).astype(jnp.int32)
    order = jnp.argsort(flat_e, stable=True).astype(jnp.int32)
    order = jnp.concatenate([order, jnp.zeros((n_slot,), jnp.int32)])
    starts = jnp.cumsum(counts) - counts
    row_of = jnp.zeros((n_slot + TOP_K * t,), jnp.int32)
    for e in range(N_EXPERTS):
        piece = lax.dynamic_slice(order, (starts[e],), (TOP_K * t,))
        row_of = lax.dynamic_update_slice(row_of, piece, (offs[e],))
    row_of = row_of[:n_slot]
    src_tok = (row_of % t).reshape(n_tiles, 1, tm)
    dst_row = row_of.reshape(n_tiles, 1, tm)
    tile_off = tile_start - jnp.sum(jnp.where(tile_start[:, None] >= ends[None, :], padded[None, :], 0), axis=1)
    tile_count = jnp.clip(counts[jnp.minimum(tile_e, N_EXPERTS - 1)] - tile_off, 0, tm).astype(jnp.int32)
    tile_count = jnp.where(jnp.arange(n_tiles) < n_used, tile_count, 0)

    ys = _ffn_experts(tile_e, n_used.reshape(1), tile_count, src_tok, dst_row, xn,
                      m_wg.astype(BF16), m_wu.astype(BF16), m_wd.astype(BF16))
    out = _moe_combine(ys, x3, w1b, w2b)
    return out.reshape(b, s, d)


def kernel(x, a_norm, a_w_in, a_q_norm, a_k_norm, a_pe_k, a_pe_v, a_cmp_k_w1, a_cmp_k_w2, a_cmp_v_w1, a_cmp_v_w2, a_gate_bias, a_fox_q_norm, a_fox_k_norm, a_fox_f_bias, a_w_out, f_norm, f_w_gate, f_w_up, f_w_down, c_norm, c_w_in, c_w_a2, c_b_a, c_o_norm, c_w_out, m_norm, m_router, m_w_gate, m_w_up, m_w_down):
    x = _layer_even(x, a_norm[0], a_w_in[0], a_q_norm[0], a_k_norm[0], a_pe_k[0], a_pe_v[0],
                    a_cmp_k_w1[0], a_cmp_k_w2[0], a_cmp_v_w1[0], a_cmp_v_w2[0], a_gate_bias[0],
                    a_fox_q_norm[0], a_fox_k_norm[0], a_fox_f_bias[0], a_w_out[0],
                    f_norm[0], f_w_gate[0], f_w_up[0], f_w_down[0])
    x = _layer_odd(x, c_norm[0], c_w_in[0], c_w_a2[0], c_b_a[0], c_o_norm[0], c_w_out[0],
                   m_norm[0], m_router[0], m_w_gate[0], m_w_up[0], m_w_down[0])
    return x
```
